```python
import math
import jax, jax.numpy as jnp
from jax import lax
import numpy as np

D_MODEL = 2048
BATCH = 4
SEQ = 2048
DEPTH = 2

CHUNK = 64
N_META = 16
D_FF = 4 * D_MODEL
MIX_WIDTH = D_MODEL
POOL_WIDTH = MIX_WIDTH // 2
POOL_WINDOWS = (2, 4, 8, 16)
POOL_GROUP = POOL_WIDTH // len(POOL_WINDOWS)
SSM_WIDTH = MIX_WIDTH - POOL_WIDTH
SSM_GROUP_CH = 16
SSM_GROUPS = SSM_WIDTH // SSM_GROUP_CH
SSM_STATE = 64
DT_MIN = 0.001
DT_MAX = 0.1
HEAD_DIM = 128
N_HEADS = D_MODEL // HEAD_DIM
N_KV_HEADS = 4
KV_GROUP = N_HEADS // N_KV_HEADS
IDX_HEADS = 16
IDX_DIM = 64
ROPE_THETA = 500000.0
ROPE_FRAC = 4
TOPK_MAX = 256
Q_BLOCK = 128
EPS = 1e-6
N_EVEN = (DEPTH + 1) // 2
N_ODD = DEPTH // 2
Q_W = N_HEADS * HEAD_DIM
KV_W = N_KV_HEADS * HEAD_DIM
IQ_W = IDX_HEADS * IDX_DIM
C_IN_WIDTH = Q_W + 2 * KV_W + IQ_W + IDX_DIM + IDX_HEADS

kernel_name = 'hybrid_pool_s5_dsa_block'


def rms_norm(x, g):
    xf = x.astype(jnp.float32)
    y = xf * lax.rsqrt(jnp.mean(xf * xf, axis=-1, keepdims=True) + EPS)
    return (y * g.astype(jnp.float32)).astype(x.dtype)


def chunk_ids(T):
    t = jnp.arange(T)
    return jnp.where(t < N_META, 0, (t - N_META) // CHUNK + 1)


def rope_partial(x, pos):
    d = x.shape[-1]
    r = d // ROPE_FRAC
    half = r // 2
    inv = ROPE_THETA ** (-jnp.arange(half, dtype=jnp.float32) / half)
    ang = pos.astype(jnp.float32)[:, None] * inv[None, :]
    cos = jnp.cos(ang)[:, None, :]
    sin = jnp.sin(ang)[:, None, :]
    x1 = x[..., :half].astype(jnp.float32)
    x2 = x[..., half:r].astype(jnp.float32)
    rot = jnp.concatenate([x1 * cos - x2 * sin, x2 * cos + x1 * sin], axis=-1).astype(x.dtype)
    return jnp.concatenate([rot, x[..., r:]], axis=-1)


def pool_mixer(u, w, scale):
    T = u.shape[1]
    uf = u.astype(jnp.float32)
    cs = jnp.pad(jnp.cumsum(uf, axis=1), ((0, 0), (1, 0), (0, 0)))
    count = jnp.arange(1, T + 1, dtype=jnp.float32)[None, :, None]
    outs = []
    for g, win in enumerate(POOL_WINDOWS):
        sl = slice(g * POOL_GROUP, (g + 1) * POOL_GROUP)
        c = cs[..., sl]
        lower = jnp.pad(c, ((0, 0), (win, 0), (0, 0)))[:, 1:T + 1]
        mean = (c[:, 1:] - lower) / jnp.minimum(count, float(win))
        diff = (mean - uf[..., sl]).astype(u.dtype)
        outs.append(diff @ w[g])
    return jnp.concatenate(outs, axis=-1) * scale


def _ssm_combine(e1, e2):
    a1r, a1i, b1r, b1i = e1
    a2r, a2i, b2r, b2i = e2
    return (a2r * a1r - a2i * a1i,
            a2r * a1i + a2i * a1r,
            a2r * b1r - a2i * b1i + b2r,
            a2r * b1i + a2i * b1r + b2i)


def s5_mixer(u, lam_re, lam_im, log_dt, b_re, b_im, c_re, c_im, d, w_glu):
    Bsz, T, _ = u.shape
    f32 = jnp.float32
    uf = u.astype(f32).reshape(Bsz, T, SSM_GROUPS, SSM_GROUP_CH)
    dt = jnp.exp(log_dt.astype(f32))[:, None]
    ar = lam_re.astype(f32)
    ai = lam_im.astype(f32)
    mag = jnp.exp(dt * ar)
    abar_re = mag * jnp.cos(dt * ai)
    abar_im = mag * jnp.sin(dt * ai)
    den = ar * ar + ai * ai
    zr = abar_re - 1.0
    zi = abar_im
    f_re = (zr * ar + zi * ai) / den
    f_im = (zi * ar - zr * ai) / den
    br = b_re.astype(f32)
    bi = b_im.astype(f32)
    bb_re = f_re[..., None] * br - f_im[..., None] * bi
    bb_im = f_re[..., None] * bi + f_im[..., None] * br
    bu_re = jnp.einsum('btgc,gnc->btgn', uf, bb_re)
    bu_im = jnp.einsum('btgc,gnc->btgn', uf, bb_im)
    a_re = jnp.broadcast_to(abar_re, bu_re.shape)
    a_im = jnp.broadcast_to(abar_im, bu_re.shape)
    _, _, h_re, h_im = lax.associative_scan(_ssm_combine, (a_re, a_im, bu_re, bu_im), axis=1)
    y = (jnp.einsum('btgn,gcn->btgc', h_re, c_re.astype(f32))
         - jnp.einsum('btgn,gcn->btgc', h_im, c_im.astype(f32))
         + d.astype(f32) * uf)
    y = jax.nn.gelu(y.reshape(Bsz, T, SSM_WIDTH)).astype(u.dtype)
    return y * jax.nn.sigmoid(y @ w_glu)


def dsa_mixer(h, w_in, w_out):
    Bsz, T, _ = h.shape
    f32 = jnp.float32
    kk = min(TOPK_MAX, (T - N_META) // 4)
    proj = h @ w_in
    o = 0
    q = proj[..., o:o + Q_W].reshape(Bsz, T, N_HEADS, HEAD_DIM); o += Q_W
    k = proj[..., o:o + KV_W].reshape(Bsz, T, N_KV_HEADS, HEAD_DIM); o += KV_W
    v = proj[..., o:o + KV_W].reshape(Bsz, T, N_KV_HEADS, HEAD_DIM); o += KV_W
    qi = proj[..., o:o + IQ_W].reshape(Bsz, T, IDX_HEADS, IDX_DIM); o += IQ_W
    ki = proj[..., o:o + IDX_DIM]; o += IDX_DIM
    wi = proj[..., o:o + IDX_HEADS]
    pos = jnp.arange(T)
    q = rope_partial(q, pos)
    k = rope_partial(k, pos)
    qi = rope_partial(qi, pos)
    ki = rope_partial(ki[:, :, None, :], pos)[:, :, 0, :].astype(f32)
    cid = chunk_ids(T)
    nb = -(-T // Q_BLOCK)
    pad = nb * Q_BLOCK - T

    def to_blocks(a):
        a = jnp.pad(a, [(0, 0), (0, pad)] + [(0, 0)] * (a.ndim - 2))
        return a.reshape((Bsz, nb, Q_BLOCK) + a.shape[2:]).swapaxes(0, 1)

    cid_q = jnp.pad(cid, (0, pad), mode='edge').reshape(nb, Q_BLOCK)
    w_scale = (IDX_HEADS ** -0.5) * (IDX_DIM ** -0.5)
    att_scale = HEAD_DIM ** -0.5

    def block(args):
        qb, qib, wib, cq = args
        s_idx = jax.nn.relu(jnp.einsum('bqhd,bsd->bqhs', qib.astype(f32), ki))
        score = jnp.einsum('bqh,bqhs->bqs', wib.astype(f32) * w_scale, s_idx)
        allowed = cid[None, :] <= cq[:, None]
        score = jnp.where(allowed[None], score, -jnp.inf)
        _, idx = lax.top_k(score, kk)
        valid = cid[idx] <= cq[None, :, None]
        k_sel = jax.vmap(lambda kb, ib: kb[ib])(k, idx)
        v_sel = jax.vmap(lambda vb, ib: vb[ib])(v, idx)
        qg = qb.reshape(Bsz, Q_BLOCK, N_KV_HEADS, KV_GROUP, HEAD_DIM)
        logits = jnp.einsum('bqhgd,bqshd->bqhgs', qg, k_sel).astype(f32) * att_scale
        logits = jnp.where(valid[:, :, None, None, :], logits, -jnp.inf)
        p = jax.nn.softmax(logits, axis=-1).astype(v.dtype)
        ob = jnp.einsum('bqhgs,bqshd->bqhgd', p, v_sel)
        return ob.reshape(Bsz, Q_BLOCK, Q_W)

    out = lax.map(block, (to_blocks(q), to_blocks(qi), to_blocks(wi), cid_q))
    out = out.swapaxes(0, 1).reshape(Bsz, nb * Q_BLOCK, Q_W)[:, :T]
    return out @ w_out


def setup_inputs(seed: int = 0) -> dict:
    key = jax.random.key(seed)
    ks = jax.random.split(key, 24)
    f32 = jnp.float32

    def nrm(k, shape, s):
        return jax.random.normal(k, shape, f32) * s

    n_idx = jnp.arange(SSM_STATE, dtype=f32)
    return {
        'x': nrm(ks[0], (BATCH, SEQ, D_MODEL), 1.0),
        'meta': nrm(ks[1], (N_META, D_MODEL), 1.0),
        'norm_g': 1.0 + nrm(ks[2], (DEPTH, 4, D_MODEL), 0.02),
        'ab_w_in': nrm(ks[3], (N_EVEN, D_MODEL, MIX_WIDTH), D_MODEL ** -0.5),
        'pool_w': nrm(ks[4], (N_EVEN, len(POOL_WINDOWS), POOL_GROUP, POOL_GROUP), POOL_GROUP ** -0.5),
        'pool_scale': 1.0 + nrm(ks[5], (N_EVEN, POOL_WIDTH), 0.02),
        's5_lambda_re': -0.5 + nrm(ks[6], (N_EVEN, SSM_GROUPS, SSM_STATE), 0.01),
        's5_lambda_im': math.pi * n_idx + nrm(ks[7], (N_EVEN, SSM_GROUPS, SSM_STATE), 0.01),
        's5_log_dt': jax.random.uniform(ks[8], (N_EVEN, SSM_GROUPS), f32, math.log(DT_MIN), math.log(DT_MAX)),
        's5_b_re': nrm(ks[9], (N_EVEN, SSM_GROUPS, SSM_STATE, SSM_GROUP_CH), (2.0 * SSM_GROUP_CH) ** -0.5),
        's5_b_im': nrm(ks[10], (N_EVEN, SSM_GROUPS, SSM_STATE, SSM_GROUP_CH), (2.0 * SSM_GROUP_CH) ** -0.5),
        's5_c_re': nrm(ks[11], (N_EVEN, SSM_GROUPS, SSM_GROUP_CH, SSM_STATE), (2.0 * SSM_STATE) ** -0.5),
        's5_c_im': nrm(ks[12], (N_EVEN, SSM_GROUPS, SSM_GROUP_CH, SSM_STATE), (2.0 * SSM_STATE) ** -0.5),
        's5_d': nrm(ks[13], (N_EVEN, SSM_GROUPS, SSM_GROUP_CH), 1.0),
        's5_w_glu': nrm(ks[14], (N_EVEN, SSM_WIDTH, SSM_WIDTH), SSM_WIDTH ** -0.5),
        'ab_w_out': nrm(ks[15], (N_EVEN, MIX_WIDTH, D_MODEL), MIX_WIDTH ** -0.5),
        'c_w_in': nrm(ks[16], (N_ODD, D_MODEL, C_IN_WIDTH), D_MODEL ** -0.5),
        'c_w_out': nrm(ks[17], (N_ODD, Q_W, D_MODEL), Q_W ** -0.5),
        'mlp_w1': nrm(ks[18], (DEPTH, D_MODEL, D_FF), D_MODEL ** -0.5),
        'mlp_w2': nrm(ks[19], (DEPTH, D_FF, D_MODEL), D_FF ** -0.5),
    }


def reference(x, meta, norm_g, ab_w_in, pool_w, pool_scale, s5_lambda_re, s5_lambda_im, s5_log_dt,
              s5_b_re, s5_b_im, s5_c_re, s5_c_im, s5_d, s5_w_glu, ab_w_out, c_w_in, c_w_out,
              mlp_w1, mlp_w2):
    Bsz = x.shape[0]
    meta_b = jnp.broadcast_to(meta[None].astype(x.dtype), (Bsz, N_META, x.shape[-1]))
    h = jnp.concatenate([meta_b, x], axis=1)
    for layer in range(DEPTH):
        g = norm_g[layer]
        a = rms_norm(h, g[0])
        if layer % 2 == 0:
            e = layer // 2
            u = a @ ab_w_in[e]
            y_pool = pool_mixer(u[..., :POOL_WIDTH], pool_w[e], pool_scale[e])
            y_ssm = s5_mixer(u[..., POOL_WIDTH:], s5_lambda_re[e], s5_lambda_im[e], s5_log_dt[e],
                             s5_b_re[e], s5_b_im[e], s5_c_re[e], s5_c_im[e], s5_d[e], s5_w_glu[e])
            mix = jnp.concatenate([y_pool, y_ssm], axis=-1) @ ab_w_out[e]
        else:
            o = layer // 2
            mix = dsa_mixer(a, c_w_in[o], c_w_out[o])
        h = h + rms_norm(mix, g[1])
        a = rms_norm(h, g[2])
        f = jnp.square(jax.nn.relu(a @ mlp_w1[layer])) @ mlp_w2[layer]
        h = h + rms_norm(f, g[3])
    return h[:, N_META:]
```

```python
import functools
import math

import jax
import jax.numpy as jnp
from jax import lax
from jax.experimental import pallas as pl
from jax.experimental.pallas import tpu as pltpu

F32 = jnp.float32
BF16 = jnp.bfloat16

D_MODEL = 2048
CHUNK = 64
N_META = 16
POOL_WIDTH = 1024
POOL_WINDOWS = (2, 4, 8, 16)
POOL_GROUP = 256
POOL_HALO = 16
SSM_WIDTH = 1024
SSM_GROUP_CH = 16
SSM_GROUPS = 64
SSM_STATE = 64
GROUPS_PER_TILE = 8
HEAD_DIM = 128
N_HEADS = 16
N_KV_HEADS = 4
KV_GROUP = 4
IDX_HEADS = 16
IDX_DIM = 64
ROPE_THETA = 500000.0
ROPE_FRAC = 4
TOPK = 256
EPS = 1e-6
Q_W = N_HEADS * HEAD_DIM
KV_W = N_KV_HEADS * HEAD_DIM
IQ_W = IDX_HEADS * IDX_DIM

LANES = 128
SUBLANES = 8
VMEM_LIMIT = 56 * 1024 * 1024

T_REAL = N_META + 2048
T_PAD = 2176
TM = 512
TQ = 128
L_SSM = 544
TF = 512

INT_MIN = -2 ** 31
NEG_INF_KEY = -2139095041


def _params(sem):
    return pltpu.CompilerParams(dimension_semantics=sem, vmem_limit_bytes=VMEM_LIMIT)


def _rms(x, g):
    return x * lax.rsqrt(jnp.mean(x * x, axis=-1, keepdims=True) + EPS) * g


def _rmsnorm_kernel(x_ref, g_ref, o_ref):
    o_ref[...] = _rms(x_ref[...], g_ref[...]).astype(o_ref.dtype)


def _rmsnorm(x, g):
    m, d = x.shape
    return pl.pallas_call(
        _rmsnorm_kernel,
        grid=(m // TM,),
        in_specs=[pl.BlockSpec((TM, d), lambda i: (i, 0)), pl.BlockSpec((1, d), lambda i: (0, 0))],
        out_specs=pl.BlockSpec((TM, d), lambda i: (i, 0)),
        out_shape=jax.ShapeDtypeStruct((m, d), BF16),
        compiler_params=_params(("parallel",)),
        name="rmsnorm",
    )(x, g.reshape(1, d))


def _proj_kernel(a_ref, w_ref, *rest, rope_half):
    o_ref = rest[-1]
    y = jnp.dot(a_ref[...], w_ref[...], preferred_element_type=F32)
    if not rope_half:
        o_ref[...] = y.astype(o_ref.dtype)
        return
    c, s1, s2 = rest[0][...], rest[1][...], rest[2][...]
    for s in range(y.shape[1] // LANES):
        ys = y[:, s * LANES:(s + 1) * LANES]
        r = ys * c + pltpu.roll(ys, rope_half, 1) * s1 + pltpu.roll(ys, LANES - rope_half, 1) * s2
        o_ref[:, s * LANES:(s + 1) * LANES] = r.astype(o_ref.dtype)


def _proj(a, w, out_dtype, tn, rope=None, pair_major=False):
    m, k = a.shape
    n = w.shape[1]
    in_specs = [pl.BlockSpec((TM, k), lambda i, j: (i, 0)), pl.BlockSpec((k, tn), lambda i, j: (0, j))]
    args = [a, w]
    half = 0
    if rope is not None:
        half = rope[0]
        in_specs += [pl.BlockSpec((TM, LANES), lambda i, j: (i, 0))] * 3
        args += list(rope[1:])
    if pair_major:
        assert tn == LANES
        out_spec = pl.BlockSpec((None, TM, LANES), lambda i, j: (j, i, 0))
        out_shape = jax.ShapeDtypeStruct((n // LANES, m, LANES), out_dtype)
    else:
        out_spec = pl.BlockSpec((TM, tn), lambda i, j: (i, j))
        out_shape = jax.ShapeDtypeStruct((m, n), out_dtype)
    return pl.pallas_call(
        functools.partial(_proj_kernel, rope_half=half),
        grid=(m // TM, n // tn),
        in_specs=in_specs,
        out_specs=out_spec,
        out_shape=out_shape,
        compiler_params=_params(("parallel", "parallel")),
        name="proj",
    )(*args)


def _rope_tables(head_dim, rows_per_batch, batch, scale=1.0, active_lanes=LANES):
    r = head_dim // ROPE_FRAC
    half = r // 2
    inv = ROPE_THETA ** (-jnp.arange(half, dtype=F32) / half)
    ang = jnp.arange(rows_per_batch, dtype=F32)[:, None] * inv[None, :]
    cos, sin = jnp.cos(ang), jnp.sin(ang)
    ones = jnp.ones((rows_per_batch, head_dim - r), F32)
    zeros_h = jnp.zeros((rows_per_batch, half), F32)
    zeros_t = jnp.zeros((rows_per_batch, head_dim - r), F32)
    c = jnp.concatenate([cos, cos, ones], axis=1)
    s_up = jnp.concatenate([zeros_h, sin, zeros_t], axis=1)
    s_dn = jnp.concatenate([-sin, zeros_h, zeros_t], axis=1)
    reps = LANES // head_dim
    lane = jnp.arange(LANES)[None, :]
    out = []
    for t, fill in ((c, 1.0), (s_up, 0.0), (s_dn, 0.0)):
        t = jnp.tile(t, (1, reps))
        t = jnp.where(lane < active_lanes, t, fill) * scale
        out.append(jnp.tile(t, (batch, 1)))
    return (half,) + tuple(out)


def _matmul_norm_res_kernel(a_ref, w_ref, g_ref, r_ref, o_ref):
    y = jnp.dot(a_ref[...], w_ref[...], preferred_element_type=F32)
    o_ref[...] = r_ref[...] + _rms(y, g_ref[...])


def _matmul_norm_res(a, w, g, res):
    m, k = a.shape
    n = w.shape[1]
    return pl.pallas_call(
        _matmul_norm_res_kernel,
        grid=(m // TM,),
        in_specs=[pl.BlockSpec((TM, k), lambda i: (i, 0)), pl.BlockSpec((k, n), lambda i: (0, 0)),
                  pl.BlockSpec((1, n), lambda i: (0, 0)), pl.BlockSpec((TM, n), lambda i: (i, 0))],
        out_specs=pl.BlockSpec((TM, n), lambda i: (i, 0)),
        out_shape=jax.ShapeDtypeStruct((m, n), F32),
        compiler_params=_params(("parallel",)),
        name="matmul_norm_res",
    )(a, w, g.reshape(1, n), res)


def _mlp_kernel(h_ref, g_in_ref, g_out_ref, w1_ref, w2_ref, o_ref, a_ref, acc_ref):
    f = pl.program_id(1)

    @pl.when(f == 0)
    def _():
        a_ref[...] = _rms(h_ref[...], g_in_ref[...]).astype(BF16)
        acc_ref[...] = jnp.zeros_like(acc_ref)

    z = jnp.dot(a_ref[...], w1_ref[...], preferred_element_type=F32)
    z = jnp.square(jnp.maximum(z, 0.0)).astype(BF16)
    acc_ref[...] += jnp.dot(z, w2_ref[...], preferred_element_type=F32)

    @pl.when(f == pl.num_programs(1) - 1)
    def _():
        o_ref[...] = h_ref[...] + _rms(acc_ref[...], g_out_ref[...])


def _mlp(h, g_in, g_out, w1, w2):
    m, d = h.shape
    ff = w1.shape[1]
    return pl.pallas_call(
        _mlp_kernel,
        grid=(m // TM, ff // TF),
        in_specs=[pl.BlockSpec((TM, d), lambda i, f: (i, 0)),
                  pl.BlockSpec((1, d), lambda i, f: (0, 0)), pl.BlockSpec((1, d), lambda i, f: (0, 0)),
                  pl.BlockSpec((d, TF), lambda i, f: (0, f)), pl.BlockSpec((TF, d), lambda i, f: (f, 0))],
        out_specs=pl.BlockSpec((TM, d), lambda i, f: (i, 0)),
        out_shape=jax.ShapeDtypeStruct((m, d), F32),
        scratch_shapes=[pltpu.VMEM((TM, d), BF16), pltpu.VMEM((TM, d), F32)],
        compiler_params=_params(("parallel", "arbitrary")),
        name="mlp",
    )(h, g_in.reshape(1, d), g_out.reshape(1, d), w1, w2)


def _ssm_prep_kernel(lre_ref, lim_ref, ldt_ref, bre_ref, bim_ref, abr_ref, abi_ref, bbr_ref, bbi_ref):
    ar, ai = lre_ref[...], lim_ref[...]
    dt = jnp.exp(ldt_ref[...])
    mag = jnp.exp(dt * ar)
    abr = mag * jnp.cos(dt * ai)
    abi = mag * jnp.sin(dt * ai)
    den = ar * ar + ai * ai
    zr, zi = abr - 1.0, abi
    fr = (zr * ar + zi * ai) / den
    fi = (zi * ar - zr * ai) / den
    br, bi = bre_ref[...], bim_ref[...]
    abr_ref[...] = abr
    abi_ref[...] = abi
    bbr_ref[...] = fr * br - fi * bi
    bbi_ref[...] = fr * bi + fi * br


def _ssm_prep(lam_re, lam_im, log_dt, b_re, b_im):
    g, n = lam_re.shape
    c = b_re.shape[-1]
    bt = lambda b: jnp.swapaxes(b, 1, 2)
    full = lambda *s: pl.BlockSpec(s, lambda: (0,) * len(s))
    return pl.pallas_call(
        _ssm_prep_kernel,
        in_specs=[full(g, 1, n), full(g, 1, n), full(g, 1, 1), full(g, c, n), full(g, c, n)],
        out_specs=[full(g, 1, n), full(g, 1, n), full(g, c, n), full(g, c, n)],
        out_shape=[jax.ShapeDtypeStruct((g, 1, n), F32)] * 2 + [jax.ShapeDtypeStruct((g, c, n), F32)] * 2,
        name="ssm_prep",
    )(lam_re.reshape(g, 1, n), lam_im.reshape(g, 1, n), log_dt.reshape(g, 1, 1), bt(b_re), bt(b_im))


def _block_diag_in(bb):
    g, c, n = bb.shape
    r = bb.reshape(g // GROUPS_PER_TILE, GROUPS_PER_TILE, c, n)
    eye = jnp.eye(GROUPS_PER_TILE, dtype=bb.dtype)
    return jnp.einsum('jgcn,gh->jgchn', r, eye).reshape(g // GROUPS_PER_TILE, GROUPS_PER_TILE * c, GROUPS_PER_TILE * n)


def _block_diag_out(cc):
    g, c, n = cc.shape
    r = cc.reshape(g // GROUPS_PER_TILE, GROUPS_PER_TILE, c, n)
    eye = jnp.eye(GROUPS_PER_TILE, dtype=cc.dtype)
    return jnp.einsum('jgcn,gh->jgnhc', r, eye).reshape(g // GROUPS_PER_TILE, GROUPS_PER_TILE * n, GROUPS_PER_TILE * c)


def _cmul(ar, ai, br, bi):
    return ar * br - ai * bi, ar * bi + ai * br


def _mixer0_kernel(u_ref, pw_ref, ps_ref, wb_ref, wc_ref, abr_ref, abi_ref, d_ref, wg_ref, o_ref,
                   pbuf, sbuf, ybuf, carry_ref):
    c = pl.program_id(1)
    L = u_ref.shape[0]
    sw = GROUPS_PER_TILE * SSM_STATE

    @pl.when(c == 0)
    def _():
        pbuf[0:POOL_HALO, :] = jnp.zeros((POOL_HALO, POOL_WIDTH), F32)
        carry_ref[...] = jnp.zeros_like(carry_ref)

    pbuf[POOL_HALO:POOL_HALO + L, :] = u_ref[:, 0:POOL_WIDTH]
    tpos = c * L + lax.broadcasted_iota(jnp.int32, (L, 1), 0)
    for g, win in enumerate(POOL_WINDOWS):
        sl = slice(g * POOL_GROUP, (g + 1) * POOL_GROUP)
        x = u_ref[:, sl]
        s = x
        for k in range(1, win):
            s = s + pbuf[POOL_HALO - k:POOL_HALO - k + L, sl]
        cnt = jnp.minimum(tpos + 1, win).astype(F32)
        diff = (s / cnt - x).astype(BF16)
        y = jnp.dot(diff, pw_ref[g], preferred_element_type=F32) * ps_ref[:, sl]
        o_ref[:, sl] = y.astype(o_ref.dtype)
    pbuf[0:POOL_HALO, :] = pbuf[L:L + POOL_HALO, :]

    row = lax.broadcasted_iota(jnp.int32, (SUBLANES, sw), 0)
    for j in range(SSM_WIDTH // LANES):
        lo = POOL_WIDTH + j * LANES
        uj = u_ref[:, lo:lo + LANES]
        sbuf[...] = jnp.dot(uj.astype(BF16), wb_ref[j], preferred_element_type=F32)
        a1r = jnp.broadcast_to(abr_ref[j], (SUBLANES, sw))
        a1i = jnp.broadcast_to(abi_ref[j], (SUBLANES, sw))
        pw_r, pw_i = [a1r], [a1i]
        for _ in range(SUBLANES - 1):
            nr, ni = _cmul(pw_r[-1], pw_i[-1], a1r, a1i)
            pw_r.append(nr)
            pw_i.append(ni)
        steps = []
        for k in (1, 2, 4):
            steps.append((k, jnp.where(row >= k, pw_r[k - 1], 0.0), jnp.where(row >= k, pw_i[k - 1], 0.0)))
        pcr, pci = pw_r[0], pw_i[0]
        for r in range(1, SUBLANES):
            pcr = jnp.where(row == r, pw_r[r], pcr)
            pci = jnp.where(row == r, pw_i[r], pci)

        def body(r, carry):
            cr, ci = carry
            rows = pl.ds(pl.multiple_of(r * SUBLANES, SUBLANES), SUBLANES)
            xr = sbuf[rows, 0:sw]
            xi = sbuf[rows, sw:2 * sw]
            for k, kr, ki in steps:
                sr, si = pltpu.roll(xr, k, 0), pltpu.roll(xi, k, 0)
                xr, xi = xr + kr * sr - ki * si, xi + kr * si + ki * sr
            xr, xi = xr + pcr * cr - pci * ci, xi + pcr * ci + pci * cr
            sbuf[rows, 0:sw] = xr
            sbuf[rows, sw:2 * sw] = xi
            last = slice(SUBLANES - 1, SUBLANES)
            return jnp.broadcast_to(xr[last], (SUBLANES, sw)), jnp.broadcast_to(xi[last], (SUBLANES, sw))

        cr0 = jnp.broadcast_to(carry_ref[j, 0:1, :], (SUBLANES, sw))
        ci0 = jnp.broadcast_to(carry_ref[j, 1:2, :], (SUBLANES, sw))
        cr, ci = lax.fori_loop(0, L // SUBLANES, body, (cr0, ci0))
        carry_ref[j, 0:1, :] = cr[0:1]
        carry_ref[j, 1:2, :] = ci[0:1]
        yj = jnp.dot(sbuf[...].astype(BF16), wc_ref[j], preferred_element_type=F32)
        ybuf[:, j * LANES:(j + 1) * LANES] = yj + d_ref[:, j * LANES:(j + 1) * LANES] * uj

    y = jax.nn.gelu(ybuf[...])
    gate = jax.nn.sigmoid(jnp.dot(y.astype(BF16), wg_ref[...], preferred_element_type=F32))
    o_ref[:, POOL_WIDTH:] = (y * gate).astype(o_ref.dtype)


def _mixer0(u, batch, pool_w, pool_scale, wb, wc, abr, abi, d, w_glu):
    m, width = u.shape
    nj = SSM_WIDTH // LANES
    sw = GROUPS_PER_TILE * SSM_STATE
    chunks = T_PAD // L_SSM
    const = lambda *s: pl.BlockSpec(s, lambda b, c: (0,) * len(s))
    return pl.pallas_call(
        _mixer0_kernel,
        grid=(batch, chunks),
        in_specs=[pl.BlockSpec((L_SSM, width), lambda b, c: (b * chunks + c, 0)),
                  const(len(POOL_WINDOWS), POOL_GROUP, POOL_GROUP), const(1, POOL_WIDTH),
                  const(nj, LANES, 2 * sw), const(nj, 2 * sw, LANES),
                  const(nj, 1, sw), const(nj, 1, sw), const(1, SSM_WIDTH), const(SSM_WIDTH, SSM_WIDTH)],
        out_specs=pl.BlockSpec((L_SSM, width), lambda b, c: (b * chunks + c, 0)),
        out_shape=jax.ShapeDtypeStruct((m, width), BF16),
        scratch_shapes=[pltpu.VMEM((POOL_HALO + L_SSM, POOL_WIDTH), F32),
                        pltpu.VMEM((L_SSM, 2 * sw), F32),
                        pltpu.VMEM((L_SSM, SSM_WIDTH), F32),
                        pltpu.VMEM((nj, 2, sw), F32)],
        compiler_params=_params(("parallel", "arbitrary")),
        name="mixer0",
    )(u, pool_w, pool_scale, wb, wc, abr, abi, d, w_glu)


def _chunk_id(t):
    return jnp.where(t < N_META, 0, ((t - N_META) >> 6) + 1)


def _dsa_kernel(qi_ref, wq_ref, q_ref, kw_ref, k_ref, v_ref, o_ref, kie_ref, kio_ref):
    i = pl.program_id(1)
    tq = q_ref.shape[0]
    tk = k_ref.shape[0]
    nt = (((1,), (1,)), ((), ()))

    @pl.when(i == 0)
    def _():
        kw = kw_ref[...]
        lane = lax.broadcasted_iota(jnp.int32, kw.shape, 1)
        kie_ref[...] = jnp.where(lane < IDX_DIM, kw, 0.0).astype(BF16)
        kio_ref[...] = jnp.where(lane >= IDX_DIM, pltpu.roll(kw, IDX_DIM, 1), 0.0).astype(BF16)

    w = wq_ref[...] * ((IDX_HEADS ** -0.5) * (IDX_DIM ** -0.5))
    score = jnp.zeros((tq, tk), F32)
    pairs = IDX_HEADS // 2
    for half in range(2):
        lhs = qi_ref[half * (pairs // 2):(half + 1) * (pairs // 2)].reshape((pairs // 2) * tq, LANES)
        se = lax.dot_general(lhs, kie_ref[...], nt, preferred_element_type=F32)
        so = lax.dot_general(lhs, kio_ref[...], nt, preferred_element_type=F32)
        for p in range(pairs // 2):
            h = 2 * (half * (pairs // 2) + p)
            we = w[:, IDX_DIM + h:IDX_DIM + h + 1]
            wo = w[:, IDX_DIM + h + 1:IDX_DIM + h + 2]
            score = score + we * jnp.maximum(se[p * tq:(p + 1) * tq], 0.0)
            score = score + wo * jnp.maximum(so[p * tq:(p + 1) * tq], 0.0)

    qpos = i * tq + lax.broadcasted_iota(jnp.int32, (tq, 1), 0)
    kpos = lax.broadcasted_iota(jnp.int32, (1, tk), 1)
    allowed = (_chunk_id(kpos) <= _chunk_id(qpos)) & (kpos < T_REAL)
    score = jnp.where(allowed, score, -jnp.inf)

    bits = pltpu.bitcast(score, jnp.int32)
    key = jnp.where(bits < 0, bits ^ 0x7FFFFFFF, bits)

    def count_ge(cand):
        return jnp.sum(jnp.where(key >= cand, 1.0, 0.0), axis=-1, keepdims=True)

    th = jnp.where(count_ge(0) >= TOPK, 0, INT_MIN).astype(jnp.int32)

    def search(j, th):
        cand = th | jnp.left_shift(jnp.int32(1), 30 - j)
        return jnp.where(count_ge(cand) >= TOPK, cand, th)

    th = lax.fori_loop(0, 31, search, th)
    th = jnp.maximum(th, NEG_INF_KEY + 1)
    bias = jnp.where(key >= th, 0.0, -jnp.inf)

    for kv in range(N_KV_HEADS):
        heads = [q_ref[:, (kv * KV_GROUP + g) * HEAD_DIM:(kv * KV_GROUP + g + 1) * HEAD_DIM] for g in range(KV_GROUP)]
        qs = jnp.concatenate(heads, axis=0)
        kk = k_ref[:, kv * HEAD_DIM:(kv + 1) * HEAD_DIM]
        vv = v_ref[:, kv * HEAD_DIM:(kv + 1) * HEAD_DIM]
        logit = lax.dot_general(qs, kk, nt, preferred_element_type=F32)
        logit = logit.reshape(KV_GROUP, tq, tk) + bias[None]
        mx = jnp.max(logit, axis=-1, keepdims=True)
        e = jnp.exp(logit - mx)
        den = jnp.sum(e, axis=-1, keepdims=True)
        pv = jnp.dot(e.astype(BF16).reshape(KV_GROUP * tq, tk), vv, preferred_element_type=F32)
        out = pv.reshape(KV_GROUP, tq, HEAD_DIM) / den
        for g in range(KV_GROUP):
            hd = kv * KV_GROUP + g
            o_ref[:, hd * HEAD_DIM:(hd + 1) * HEAD_DIM] = out[g].astype(o_ref.dtype)


def _dsa(qi, kw, q, k, v, batch):
    m = q.shape[0]
    nq = T_PAD // TQ
    pairs = qi.shape[0]
    return pl.pallas_call(
        _dsa_kernel,
        grid=(batch, nq),
        in_specs=[pl.BlockSpec((pairs, TQ, LANES), lambda b, i: (0, b * nq + i, 0)),
                  pl.BlockSpec((TQ, LANES), lambda b, i: (b * nq + i, 0)),
                  pl.BlockSpec((TQ, Q_W), lambda b, i: (b * nq + i, 0)),
                  pl.BlockSpec((T_PAD, LANES), lambda b, i: (b, 0)),
                  pl.BlockSpec((T_PAD, KV_W), lambda b, i: (b, 0)),
                  pl.BlockSpec((T_PAD, KV_W), lambda b, i: (b, 0))],
        out_specs=pl.BlockSpec((TQ, Q_W), lambda b, i: (b * nq + i, 0)),
        out_shape=jax.ShapeDtypeStruct((m, Q_W), BF16),
        scratch_shapes=[pltpu.VMEM((T_PAD, LANES), BF16), pltpu.VMEM((T_PAD, LANES), BF16)],
        compiler_params=_params(("parallel", "arbitrary")),
        name="dsa",
    )(qi, kw, q, kw, k, v)


def kernel(x, meta, norm_g, ab_w_in, pool_w, pool_scale, s5_lambda_re, s5_lambda_im, s5_log_dt,
           s5_b_re, s5_b_im, s5_c_re, s5_c_im, s5_d, s5_w_glu, ab_w_out, c_w_in, c_w_out, mlp_w1, mlp_w2):
    batch, seq, d = x.shape
    assert (seq + N_META, d) == (T_REAL, D_MODEL)
    depth = norm_g.shape[0]
    meta_b = jnp.broadcast_to(meta[None].astype(x.dtype), (batch, N_META, d))
    pad = jnp.zeros((batch, T_PAD - T_REAL, d), x.dtype)
    h = jnp.concatenate([meta_b, x, pad], axis=1).reshape(batch * T_PAD, d)

    rope_q = _rope_tables(HEAD_DIM, T_PAD, batch, scale=HEAD_DIM ** -0.5)
    rope_k = _rope_tables(HEAD_DIM, T_PAD, batch)
    rope_qi = _rope_tables(IDX_DIM, T_PAD, batch)
    rope_kw = _rope_tables(IDX_DIM, T_PAD, batch, active_lanes=IDX_DIM)

    for layer in range(depth):
        g = norm_g[layer]
        a = _rmsnorm(h, g[0])
        if layer % 2 == 0:
            e = layer // 2
            u = _proj(a, ab_w_in[e].astype(BF16), F32, 512)
            abr, abi, bbr, bbi = _ssm_prep(s5_lambda_re[e], s5_lambda_im[e], s5_log_dt[e], s5_b_re[e], s5_b_im[e])
            nj = SSM_GROUPS // GROUPS_PER_TILE
            wb = jnp.concatenate([_block_diag_in(bbr), _block_diag_in(bbi)], axis=-1).astype(BF16)
            wc = jnp.concatenate([_block_diag_out(s5_c_re[e]), -_block_diag_out(s5_c_im[e])], axis=1).astype(BF16)
            mix = _mixer0(u, batch, pool_w[e].astype(BF16), pool_scale[e].reshape(1, POOL_WIDTH), wb, wc,
                          abr.reshape(nj, 1, GROUPS_PER_TILE * SSM_STATE), abi.reshape(nj, 1, GROUPS_PER_TILE * SSM_STATE),
                          s5_d[e].reshape(1, SSM_WIDTH), s5_w_glu[e].astype(BF16))
            w_out = ab_w_out[e]
        else:
            o = layer // 2
            w_in = c_w_in[o].astype(BF16)
            c0, c1, c2, c3 = Q_W, Q_W + KV_W, Q_W + 2 * KV_W, Q_W + 2 * KV_W + IQ_W
            q = _proj(a, w_in[:, :c0], BF16, 512, rope=rope_q)
            k = _proj(a, w_in[:, c0:c1], BF16, 512, rope=rope_k)
            v = _proj(a, w_in[:, c1:c2], BF16, 512)
            qi = _proj(a, w_in[:, c2:c3], BF16, LANES, rope=rope_qi, pair_major=True)
            w_kw = jnp.pad(w_in[:, c3:], ((0, 0), (0, LANES - (IDX_DIM + IDX_HEADS))))
            kw = _proj(a, w_kw, F32, LANES, rope=rope_kw)
            mix = _dsa(qi, kw, q, k, v, batch)
            w_out = c_w_out[o]
        h = _matmul_norm_res(mix, w_out.astype(BF16), g[1], h)
        h = _mlp(h, g[2], g[3], mlp_w1[layer].astype(BF16), mlp_w2[layer].astype(BF16))
    return h.reshape(batch, T_PAD, d)[:, N_META:T_REAL]
```

```python
import functools

import jax
import jax.numpy as jnp
from jax import lax
from jax.experimental import pallas as pl
from jax.experimental.pallas import tpu as pltpu

F32 = jnp.float32
BF16 = jnp.bfloat16

D_MODEL = 2048
CHUNK = 64
N_META = 16
POOL_WIDTH = 1024
POOL_WINDOWS = (2, 4, 8, 16)
POOL_GROUP = 256
POOL_HALO = 16
SSM_WIDTH = 1024
SSM_GROUP_CH = 16
SSM_GROUPS = 64
SSM_STATE = 64
GROUPS_PER_TILE = 8
HEAD_DIM = 128
N_HEADS = 16
N_KV_HEADS = 4
KV_GROUP = 4
IDX_HEADS = 16
IDX_DIM = 64
ROPE_THETA = 500000.0
ROPE_FRAC = 4
TOPK = 256
EPS = 1e-6
Q_W = N_HEADS * HEAD_DIM
KV_W = N_KV_HEADS * HEAD_DIM
IQ_W = IDX_HEADS * IDX_DIM

LANES = 128
SUBLANES = 8
VMEM_LIMIT = 56 * 1024 * 1024

T_REAL = N_META + 2048
T_PAD = 2176
TM = 512
TQ = 128
L_SSM = 544
TF = 1024

INT_MIN = -2 ** 31
NEG_INF_KEY = -2139095041


def _params(sem):
    return pltpu.CompilerParams(dimension_semantics=sem, vmem_limit_bytes=VMEM_LIMIT)


def _rms(x, g):
    return x * lax.rsqrt(jnp.mean(x * x, axis=-1, keepdims=True) + EPS) * g


def _const_spec(shape):
    zeros = (0,) * len(shape)
    return pl.BlockSpec(shape, lambda *_: zeros, pipeline_mode=pl.Buffered(1))


def _norm_proj_kernel(h_ref, g_ref, w_ref, o_ref):
    a = _rms(h_ref[...], g_ref[...]).astype(BF16)
    o_ref[...] = jnp.dot(a, w_ref[...], preferred_element_type=F32)


def _norm_proj(h, g, w):
    m, d = h.shape
    n = w.shape[1]
    return pl.pallas_call(
        _norm_proj_kernel,
        grid=(m // TM,),
        in_specs=[pl.BlockSpec((TM, d), lambda i: (i, 0)), _const_spec((1, d)), _const_spec((d, n))],
        out_specs=pl.BlockSpec((TM, n), lambda i: (i, 0)),
        out_shape=jax.ShapeDtypeStruct((m, n), F32),
        compiler_params=_params(("parallel",)),
        name="norm_proj",
    )(h, g.reshape(1, d), w)


def _rope_tables(head_dim, rows_per_batch, batch):
    r = head_dim // ROPE_FRAC
    half = r // 2
    inv = ROPE_THETA ** (-jnp.arange(half, dtype=F32) / half)
    ang = jnp.arange(rows_per_batch, dtype=F32)[:, None] * inv[None, :]
    cos, sin = jnp.cos(ang), jnp.sin(ang)
    ones = jnp.ones((rows_per_batch, head_dim - r), F32)
    zeros_h = jnp.zeros((rows_per_batch, half), F32)
    zeros_t = jnp.zeros((rows_per_batch, head_dim - r), F32)
    c = jnp.concatenate([cos, cos, ones], axis=1)
    s_up = jnp.concatenate([zeros_h, sin, zeros_t], axis=1)
    s_dn = jnp.concatenate([-sin, zeros_h, zeros_t], axis=1)
    reps = (batch, LANES // head_dim)
    return half, jnp.tile(c, reps), jnp.tile(s_up, reps), jnp.tile(s_dn, reps)


def _rope(ys, half, c, s_up, s_dn):
    return ys * c + pltpu.roll(ys, half, 1) * s_up + pltpu.roll(ys, LANES - half, 1) * s_dn


def _dsa_proj_kernel(h_ref, g_ref, w_ref, c128_ref, u128_ref, d128_ref, c64_ref, u64_ref, d64_ref,
                     q_ref, k_ref, v_ref, qi_ref, kw_ref, *, half128, half64):
    a = _rms(h_ref[...], g_ref[...]).astype(BF16)
    t128 = (half128, c128_ref[...], u128_ref[...], d128_ref[...])
    t64 = (half64, c64_ref[...], u64_ref[...], d64_ref[...])
    c0, c1, c2, c3 = Q_W, Q_W + KV_W, Q_W + 2 * KV_W, Q_W + 2 * KV_W + IQ_W

    def slabs(lo, hi):
        y = jnp.dot(a, w_ref[:, lo:hi], preferred_element_type=F32)
        return [y[:, s * LANES:(s + 1) * LANES] for s in range((hi - lo) // LANES)]

    for s, ys in enumerate(slabs(0, c0)):
        q_ref[:, s * LANES:(s + 1) * LANES] = (_rope(ys, *t128) * (HEAD_DIM ** -0.5)).astype(q_ref.dtype)
    for s, ys in enumerate(slabs(c0, c1)):
        k_ref[:, s * LANES:(s + 1) * LANES] = _rope(ys, *t128).astype(k_ref.dtype)
    v_ref[...] = jnp.dot(a, w_ref[:, c1:c2], preferred_element_type=F32).astype(v_ref.dtype)
    for s, ys in enumerate(slabs(c2, c3)):
        qi_ref[s] = _rope(ys, *t64).astype(qi_ref.dtype)
    (ys,) = slabs(c3, c3 + LANES)
    lane = lax.broadcasted_iota(jnp.int32, ys.shape, 1)
    kw_ref[...] = jnp.where(lane < IDX_DIM, _rope(ys, *t64), ys)


def _dsa_proj(h, g, w, rope128, rope64):
    m, d = h.shape
    n = w.shape[1]
    pairs = IQ_W // LANES
    row = lambda width: pl.BlockSpec((TM, width), lambda i: (i, 0))
    return pl.pallas_call(
        functools.partial(_dsa_proj_kernel, half128=rope128[0], half64=rope64[0]),
        grid=(m // TM,),
        in_specs=[row(d), _const_spec((1, d)), _const_spec((d, n))] + [row(LANES)] * 6,
        out_specs=[row(Q_W), row(KV_W), row(KV_W), pl.BlockSpec((pairs, TM, LANES), lambda i: (0, i, 0)), row(LANES)],
        out_shape=[jax.ShapeDtypeStruct((m, Q_W), BF16), jax.ShapeDtypeStruct((m, KV_W), BF16),
                   jax.ShapeDtypeStruct((m, KV_W), BF16), jax.ShapeDtypeStruct((pairs, m, LANES), BF16),
                   jax.ShapeDtypeStruct((m, LANES), F32)],
        compiler_params=_params(("parallel",)),
        name="dsa_proj",
    )(h, g.reshape(1, d), w, *rope128[1:], *rope64[1:])


def _matmul_norm_res_kernel(a_ref, w_ref, g_ref, r_ref, o_ref):
    y = jnp.dot(a_ref[...], w_ref[...], preferred_element_type=F32)
    o_ref[...] = r_ref[...] + _rms(y, g_ref[...])


def _matmul_norm_res(a, w, g, res):
    m, k = a.shape
    n = w.shape[1]
    return pl.pallas_call(
        _matmul_norm_res_kernel,
        grid=(m // TM,),
        in_specs=[pl.BlockSpec((TM, k), lambda i: (i, 0)), _const_spec((k, n)),
                  _const_spec((1, n)), pl.BlockSpec((TM, n), lambda i: (i, 0))],
        out_specs=pl.BlockSpec((TM, n), lambda i: (i, 0)),
        out_shape=jax.ShapeDtypeStruct((m, n), F32),
        compiler_params=_params(("parallel",)),
        name="matmul_norm_res",
    )(a, w, g.reshape(1, n), res)


def _mlp_kernel(h_ref, g_in_ref, g_out_ref, w1_ref, w2_ref, o_ref, a_ref, acc_ref):
    f = pl.program_id(1)

    @pl.when(f == 0)
    def _():
        a_ref[...] = _rms(h_ref[...], g_in_ref[...]).astype(BF16)
        acc_ref[...] = jnp.zeros_like(acc_ref)

    z = jnp.dot(a_ref[...], w1_ref[...], preferred_element_type=F32)
    z = jnp.square(jnp.maximum(z, 0.0)).astype(BF16)
    acc_ref[...] += jnp.dot(z, w2_ref[...], preferred_element_type=F32)

    @pl.when(f == pl.num_programs(1) - 1)
    def _():
        o_ref[...] = h_ref[...] + _rms(acc_ref[...], g_out_ref[...])


def _mlp(h, g_in, g_out, w1, w2, layer):
    m, d = h.shape
    ff = w1.shape[2]
    return pl.pallas_call(
        _mlp_kernel,
        grid=(m // TM, ff // TF),
        in_specs=[pl.BlockSpec((TM, d), lambda i, f: (i, 0)),
                  pl.BlockSpec((1, d), lambda i, f: (0, 0)), pl.BlockSpec((1, d), lambda i, f: (0, 0)),
                  pl.BlockSpec((None, d, TF), lambda i, f: (layer, 0, f)),
                  pl.BlockSpec((None, TF, d), lambda i, f: (layer, f, 0))],
        out_specs=pl.BlockSpec((TM, d), lambda i, f: (i, 0)),
        out_shape=jax.ShapeDtypeStruct((m, d), F32),
        scratch_shapes=[pltpu.VMEM((TM, d), BF16), pltpu.VMEM((TM, d), F32)],
        compiler_params=_params(("parallel", "arbitrary")),
        name="mlp",
    )(h, g_in.reshape(1, d), g_out.reshape(1, d), w1, w2)


def _ssm_prep_kernel(lre_ref, lim_ref, ldt_ref, bre_ref, bim_ref, abr_ref, abi_ref, bbr_ref, bbi_ref):
    ar, ai = lre_ref[...], lim_ref[...]
    dt = jnp.exp(ldt_ref[...])
    mag = jnp.exp(dt * ar)
    abr = mag * jnp.cos(dt * ai)
    abi = mag * jnp.sin(dt * ai)
    den = ar * ar + ai * ai
    zr, zi = abr - 1.0, abi
    fr = (zr * ar + zi * ai) / den
    fi = (zi * ar - zr * ai) / den
    br, bi = bre_ref[...], bim_ref[...]
    abr_ref[...] = abr
    abi_ref[...] = abi
    bbr_ref[...] = fr * br - fi * bi
    bbi_ref[...] = fr * bi + fi * br


def _ssm_prep(lam_re, lam_im, log_dt, b_re, b_im):
    g, n = lam_re.shape
    c = b_re.shape[-1]
    bt = lambda b: jnp.swapaxes(b, 1, 2)
    full = lambda *s: pl.BlockSpec(s, lambda: (0,) * len(s))
    return pl.pallas_call(
        _ssm_prep_kernel,
        in_specs=[full(g, 1, n), full(g, 1, n), full(g, 1, 1), full(g, c, n), full(g, c, n)],
        out_specs=[full(g, 1, n), full(g, 1, n), full(g, c, n), full(g, c, n)],
        out_shape=[jax.ShapeDtypeStruct((g, 1, n), F32)] * 2 + [jax.ShapeDtypeStruct((g, c, n), F32)] * 2,
        name="ssm_prep",
    )(lam_re.reshape(g, 1, n), lam_im.reshape(g, 1, n), log_dt.reshape(g, 1, 1), bt(b_re), bt(b_im))


def _block_diag_in(bb):
    g, c, n = bb.shape
    r = bb.reshape(g // GROUPS_PER_TILE, GROUPS_PER_TILE, c, n)
    eye = jnp.eye(GROUPS_PER_TILE, dtype=bb.dtype)
    return jnp.einsum('jgcn,gh->jgchn', r, eye).reshape(g // GROUPS_PER_TILE, GROUPS_PER_TILE * c, GROUPS_PER_TILE * n)


def _block_diag_out(cc):
    g, c, n = cc.shape
    r = cc.reshape(g // GROUPS_PER_TILE, GROUPS_PER_TILE, c, n)
    eye = jnp.eye(GROUPS_PER_TILE, dtype=cc.dtype)
    return jnp.einsum('jgcn,gh->jgnhc', r, eye).reshape(g // GROUPS_PER_TILE, GROUPS_PER_TILE * n, GROUPS_PER_TILE * c)


def _cmul(ar, ai, br, bi):
    return ar * br - ai * bi, ar * bi + ai * br


def _mixer0_kernel(u_ref, pw_ref, ps_ref, wb_ref, wc_ref, abr_ref, abi_ref, d_ref, wg_ref, o_ref,
                   pbuf, sbuf, ybuf, carry_ref):
    c = pl.program_id(1)
    L = u_ref.shape[0]
    sw = GROUPS_PER_TILE * SSM_STATE

    @pl.when(c == 0)
    def _():
        pbuf[0:POOL_HALO, :] = jnp.zeros((POOL_HALO, POOL_WIDTH), F32)
        carry_ref[...] = jnp.zeros_like(carry_ref)

    pbuf[POOL_HALO:POOL_HALO + L, :] = u_ref[:, 0:POOL_WIDTH]
    tpos = c * L + lax.broadcasted_iota(jnp.int32, (L, 1), 0)
    for g, win in enumerate(POOL_WINDOWS):
        sl = slice(g * POOL_GROUP, (g + 1) * POOL_GROUP)
        x = u_ref[:, sl]
        s = x
        for k in range(1, win):
            s = s + pbuf[POOL_HALO - k:POOL_HALO - k + L, sl]
        cnt = jnp.minimum(tpos + 1, win).astype(F32)
        diff = (s / cnt - x).astype(BF16)
        y = jnp.dot(diff, pw_ref[g], preferred_element_type=F32) * ps_ref[:, sl]
        o_ref[:, sl] = y.astype(o_ref.dtype)
    pbuf[0:POOL_HALO, :] = pbuf[L:L + POOL_HALO, :]

    row = lax.broadcasted_iota(jnp.int32, (SUBLANES, sw), 0)
    for j in range(SSM_WIDTH // LANES):
        lo = POOL_WIDTH + j * LANES
        uj = u_ref[:, lo:lo + LANES]
        sbuf[...] = jnp.dot(uj.astype(BF16), wb_ref[j], preferred_element_type=F32)
        a1r = jnp.broadcast_to(abr_ref[j], (SUBLANES, sw))
        a1i = jnp.broadcast_to(abi_ref[j], (SUBLANES, sw))
        pw_r, pw_i = [a1r], [a1i]
        for _ in range(SUBLANES - 1):
            nr, ni = _cmul(pw_r[-1], pw_i[-1], a1r, a1i)
            pw_r.append(nr)
            pw_i.append(ni)
        steps = []
        for k in (1, 2, 4):
            steps.append((k, jnp.where(row >= k, pw_r[k - 1], 0.0), jnp.where(row >= k, pw_i[k - 1], 0.0)))
        pcr, pci = pw_r[0], pw_i[0]
        for r in range(1, SUBLANES):
            pcr = jnp.where(row == r, pw_r[r], pcr)
            pci = jnp.where(row == r, pw_i[r], pci)

        def body(r, carry):
            cr, ci = carry
            rows = pl.ds(pl.multiple_of(r * SUBLANES, SUBLANES), SUBLANES)
            xr = sbuf[rows, 0:sw]
            xi = sbuf[rows, sw:2 * sw]
            for k, kr, ki in steps:
                sr, si = pltpu.roll(xr, k, 0), pltpu.roll(xi, k, 0)
                xr, xi = xr + kr * sr - ki * si, xi + kr * si + ki * sr
            xr, xi = xr + pcr * cr - pci * ci, xi + pcr * ci + pci * cr
            sbuf[rows, 0:sw] = xr
            sbuf[rows, sw:2 * sw] = xi
            last = slice(SUBLANES - 1, SUBLANES)
            return jnp.broadcast_to(xr[last], (SUBLANES, sw)), jnp.broadcast_to(xi[last], (SUBLANES, sw))

        cr0 = jnp.broadcast_to(carry_ref[j, 0:1, :], (SUBLANES, sw))
        ci0 = jnp.broadcast_to(carry_ref[j, 1:2, :], (SUBLANES, sw))
        cr, ci = lax.fori_loop(0, L // SUBLANES, body, (cr0, ci0))
        carry_ref[j, 0:1, :] = cr[0:1]
        carry_ref[j, 1:2, :] = ci[0:1]
        yj = jnp.dot(sbuf[...].astype(BF16), wc_ref[j], preferred_element_type=F32)
        ybuf[:, j * LANES:(j + 1) * LANES] = yj + d_ref[:, j * LANES:(j + 1) * LANES] * uj

    y = jax.nn.gelu(ybuf[...])
    gate = jax.nn.sigmoid(jnp.dot(y.astype(BF16), wg_ref[...], preferred_element_type=F32))
    o_ref[:, POOL_WIDTH:] = (y * gate).astype(o_ref.dtype)


def _mixer0(u, batch, pool_w, pool_scale, wb, wc, abr, abi, d, w_glu):
    m, width = u.shape
    nj = SSM_WIDTH // LANES
    sw = GROUPS_PER_TILE * SSM_STATE
    chunks = T_PAD // L_SSM
    return pl.pallas_call(
        _mixer0_kernel,
        grid=(batch, chunks),
        in_specs=[pl.BlockSpec((L_SSM, width), lambda b, c: (b * chunks + c, 0)),
                  _const_spec((len(POOL_WINDOWS), POOL_GROUP, POOL_GROUP)), _const_spec((1, POOL_WIDTH)),
                  _const_spec((nj, LANES, 2 * sw)), _const_spec((nj, 2 * sw, LANES)),
                  _const_spec((nj, 1, sw)), _const_spec((nj, 1, sw)),
                  _const_spec((1, SSM_WIDTH)), _const_spec((SSM_WIDTH, SSM_WIDTH))],
        out_specs=pl.BlockSpec((L_SSM, width), lambda b, c: (b * chunks + c, 0)),
        out_shape=jax.ShapeDtypeStruct((m, width), BF16),
        scratch_shapes=[pltpu.VMEM((POOL_HALO + L_SSM, POOL_WIDTH), F32),
                        pltpu.VMEM((L_SSM, 2 * sw), F32),
                        pltpu.VMEM((L_SSM, SSM_WIDTH), F32),
                        pltpu.VMEM((nj, 2, sw), F32)],
        compiler_params=_params(("parallel", "arbitrary")),
        name="mixer0",
    )(u, pool_w, pool_scale, wb, wc, abr, abi, d, w_glu)


CHUNK_SHIFT = CHUNK.bit_length() - 1
assert 1 << CHUNK_SHIFT == CHUNK
Q_TILES_PER_CLASS = 2


def _chunk_id(t):
    return jnp.where(t < N_META, 0, ((t - N_META) >> CHUNK_SHIFT) + 1)


def _keys_needed(q_tile):
    last_q = (q_tile + 1) * TQ - 1
    chunk_end = N_META + CHUNK * (max(last_q - N_META, 0) // CHUNK + 1)
    return min(-(-chunk_end // LANES) * LANES, T_PAD)


def _dsa_tile(i, tk, qi_ref, wq_ref, q_ref, k_ref, v_ref, o_ref, kie_ref, kio_ref):
    tq = q_ref.shape[0]
    nt = (((1,), (1,)), ((), ()))

    w = wq_ref[...] * ((IDX_HEADS ** -0.5) * (IDX_DIM ** -0.5))
    score = jnp.zeros((tq, tk), F32)
    pairs = IDX_HEADS // 2
    for half in range(2):
        lhs = qi_ref[half * (pairs // 2):(half + 1) * (pairs // 2)].reshape((pairs // 2) * tq, LANES)
        se = lax.dot_general(lhs, kie_ref[0:tk, :], nt, preferred_element_type=F32)
        so = lax.dot_general(lhs, kio_ref[0:tk, :], nt, preferred_element_type=F32)
        for p in range(pairs // 2):
            h = 2 * (half * (pairs // 2) + p)
            we = w[:, IDX_DIM + h:IDX_DIM + h + 1]
            wo = w[:, IDX_DIM + h + 1:IDX_DIM + h + 2]
            score = score + we * jnp.maximum(se[p * tq:(p + 1) * tq], 0.0)
            score = score + wo * jnp.maximum(so[p * tq:(p + 1) * tq], 0.0)

    qpos = i * tq + lax.broadcasted_iota(jnp.int32, (tq, 1), 0)
    kpos = lax.broadcasted_iota(jnp.int32, (1, tk), 1)
    allowed = (_chunk_id(kpos) <= _chunk_id(qpos)) & (kpos < T_REAL)
    score = jnp.where(allowed, score, -jnp.inf)

    bits = pltpu.bitcast(score, jnp.int32)
    key = jnp.where(bits < 0, bits ^ 0x7FFFFFFF, bits)

    def count_ge(cand):
        return jnp.sum(jnp.where(key >= cand, 1.0, 0.0), axis=-1, keepdims=True)

    th = jnp.where(count_ge(0) >= TOPK, 0, INT_MIN).astype(jnp.int32)

    def search(j, th):
        cand = th | jnp.left_shift(jnp.int32(1), 30 - j)
        return jnp.where(count_ge(cand) >= TOPK, cand, th)

    th = lax.fori_loop(0, 31, search, th)
    th = jnp.maximum(th, NEG_INF_KEY + 1)
    bias = jnp.where(key >= th, 0.0, -jnp.inf)

    for kv in range(N_KV_HEADS):
        heads = [q_ref[:, (kv * KV_GROUP + g) * HEAD_DIM:(kv * KV_GROUP + g + 1) * HEAD_DIM] for g in range(KV_GROUP)]
        qs = jnp.concatenate(heads, axis=0)
        kk = k_ref[0:tk, kv * HEAD_DIM:(kv + 1) * HEAD_DIM]
        vv = v_ref[0:tk, kv * HEAD_DIM:(kv + 1) * HEAD_DIM]
        logit = lax.dot_general(qs, kk, nt, preferred_element_type=F32)
        logit = logit.reshape(KV_GROUP, tq, tk) + bias[None]
        mx = jnp.max(logit, axis=-1, keepdims=True)
        e = jnp.exp(logit - mx)
        den = jnp.sum(e, axis=-1, keepdims=True)
        pv = jnp.dot(e.astype(BF16).reshape(KV_GROUP * tq, tk), vv, preferred_element_type=F32)
        out = pv.reshape(KV_GROUP, tq, HEAD_DIM) / den
        for g in range(KV_GROUP):
            hd = kv * KV_GROUP + g
            o_ref[:, hd * HEAD_DIM:(hd + 1) * HEAD_DIM] = out[g].astype(o_ref.dtype)


def _dsa_kernel(qi_ref, wq_ref, q_ref, kw_ref, k_ref, v_ref, o_ref, kie_ref, kio_ref):
    i = pl.program_id(1)

    @pl.when(i == 0)
    def _():
        kw = kw_ref[...]
        lane = lax.broadcasted_iota(jnp.int32, kw.shape, 1)
        kie_ref[...] = jnp.where(lane < IDX_DIM, kw, 0.0).astype(BF16)
        kio_ref[...] = jnp.where(lane >= IDX_DIM, pltpu.roll(kw, IDX_DIM, 1), 0.0).astype(BF16)

    n_tiles = T_PAD // TQ
    classes = {}
    for t in range(n_tiles):
        last = min((t // Q_TILES_PER_CLASS + 1) * Q_TILES_PER_CLASS, n_tiles) - 1
        classes.setdefault(_keys_needed(last), []).append(t)
    for tk, tiles in classes.items():
        @pl.when((i >= tiles[0]) & (i <= tiles[-1]))
        def _(tk=tk):
            _dsa_tile(i, tk, qi_ref, wq_ref, q_ref, k_ref, v_ref, o_ref, kie_ref, kio_ref)


def _dsa(qi, kw, q, k, v, batch):
    m = q.shape[0]
    nq = T_PAD // TQ
    pairs = qi.shape[0]
    return pl.pallas_call(
        _dsa_kernel,
        grid=(batch, nq),
        in_specs=[pl.BlockSpec((pairs, TQ, LANES), lambda b, i: (0, b * nq + i, 0)),
                  pl.BlockSpec((TQ, LANES), lambda b, i: (b * nq + i, 0)),
                  pl.BlockSpec((TQ, Q_W), lambda b, i: (b * nq + i, 0)),
                  pl.BlockSpec((T_PAD, LANES), lambda b, i: (b, 0)),
                  pl.BlockSpec((T_PAD, KV_W), lambda b, i: (b, 0)),
                  pl.BlockSpec((T_PAD, KV_W), lambda b, i: (b, 0))],
        out_specs=pl.BlockSpec((TQ, Q_W), lambda b, i: (b * nq + i, 0)),
        out_shape=jax.ShapeDtypeStruct((m, Q_W), BF16),
        scratch_shapes=[pltpu.VMEM((T_PAD, LANES), BF16), pltpu.VMEM((T_PAD, LANES), BF16)],
        compiler_params=_params(("parallel", "arbitrary")),
        name="dsa",
    )(qi, kw, q, kw, k, v)


def kernel(x, meta, norm_g, ab_w_in, pool_w, pool_scale, s5_lambda_re, s5_lambda_im, s5_log_dt,
           s5_b_re, s5_b_im, s5_c_re, s5_c_im, s5_d, s5_w_glu, ab_w_out, c_w_in, c_w_out, mlp_w1, mlp_w2):
    batch, seq, d = x.shape
    assert (seq + N_META, d) == (T_REAL, D_MODEL)
    depth = norm_g.shape[0]
    meta_b = jnp.broadcast_to(meta[None].astype(x.dtype), (batch, N_META, d))
    pad = jnp.zeros((batch, T_PAD - T_REAL, d), x.dtype)
    h = jnp.concatenate([meta_b, x, pad], axis=1).reshape(batch * T_PAD, d)

    rope128 = _rope_tables(HEAD_DIM, T_PAD, batch)
    rope64 = _rope_tables(IDX_DIM, T_PAD, batch)
    w1, w2 = mlp_w1.astype(BF16), mlp_w2.astype(BF16)

    for layer in range(depth):
        g = norm_g[layer]
        if layer % 2 == 0:
            e = layer // 2
            u = _norm_proj(h, g[0], ab_w_in[e].astype(BF16))
            abr, abi, bbr, bbi = _ssm_prep(s5_lambda_re[e], s5_lambda_im[e], s5_log_dt[e], s5_b_re[e], s5_b_im[e])
            nj = SSM_GROUPS // GROUPS_PER_TILE
            wb = jnp.concatenate([_block_diag_in(bbr), _block_diag_in(bbi)], axis=-1).astype(BF16)
            wc = jnp.concatenate([_block_diag_out(s5_c_re[e]), -_block_diag_out(s5_c_im[e])], axis=1).astype(BF16)
            mix = _mixer0(u, batch, pool_w[e].astype(BF16), pool_scale[e].reshape(1, POOL_WIDTH), wb, wc,
                          abr.reshape(nj, 1, GROUPS_PER_TILE * SSM_STATE), abi.reshape(nj, 1, GROUPS_PER_TILE * SSM_STATE),
                          s5_d[e].reshape(1, SSM_WIDTH), s5_w_glu[e].astype(BF16))
            w_out = ab_w_out[e]
        else:
            o = layer // 2
            n_in = c_w_in.shape[-1]
            w_in = jnp.pad(c_w_in[o].astype(BF16), ((0, 0), (0, -n_in % LANES)))
            q, k, v, qi, kw = _dsa_proj(h, g[0], w_in, rope128, rope64)
            mix = _dsa(qi, kw, q, k, v, batch)
            w_out = c_w_out[o]
        h = _matmul_norm_res(mix, w_out.astype(BF16), g[1], h)
        h = _mlp(h, g[2], g[3], w1, w2, layer)
    return h.reshape(batch, T_PAD, d)[:, N_META:T_REAL]
```

```python
import functools

import jax
import jax.numpy as jnp
from jax import lax
from jax.experimental import pallas as pl
from jax.experimental.pallas import tpu as pltpu

F32 = jnp.float32
BF16 = jnp.bfloat16

D_MODEL = 2048
CHUNK = 64
N_META = 16
POOL_WIDTH = 1024
POOL_WINDOWS = (2, 4, 8, 16)
POOL_GROUP = 256
POOL_HALO = 16
SSM_WIDTH = 1024
SSM_GROUP_CH = 16
SSM_GROUPS = 64
SSM_STATE = 64
GROUPS_PER_TILE = 8
HEAD_DIM = 128
N_HEADS = 16
N_KV_HEADS = 4
KV_GROUP = 4
IDX_HEADS = 16
IDX_DIM = 64
ROPE_THETA = 500000.0
ROPE_FRAC = 4
TOPK = 256
EPS = 1e-6
Q_W = N_HEADS * HEAD_DIM
KV_W = N_KV_HEADS * HEAD_DIM
IQ_W = IDX_HEADS * IDX_DIM

LANES = 128
SUBLANES = 8
VMEM_LIMIT = 56 * 1024 * 1024

T_REAL = N_META + 2048
T_PAD = 2176
TM = 512
TQ = 128
L_SSM = 544
TF = 1024

INT_MIN = -2 ** 31
NEG_INF_KEY = -2139095041


def _params(sem):
    return pltpu.CompilerParams(dimension_semantics=sem, vmem_limit_bytes=VMEM_LIMIT)


def _rms(x, g):
    return x * lax.rsqrt(jnp.mean(x * x, axis=-1, keepdims=True) + EPS) * g


def _const_spec(shape):
    zeros = (0,) * len(shape)
    return pl.BlockSpec(shape, lambda *_: zeros, pipeline_mode=pl.Buffered(1))


def _norm_proj_kernel(h_ref, g_ref, w_ref, o_ref):
    a = _rms(h_ref[...], g_ref[...]).astype(BF16)
    o_ref[...] = jnp.dot(a, w_ref[...], preferred_element_type=F32)


def _norm_proj(h, g, w):
    m, d = h.shape
    n = w.shape[1]
    return pl.pallas_call(
        _norm_proj_kernel,
        grid=(m // TM,),
        in_specs=[pl.BlockSpec((TM, d), lambda i: (i, 0)), _const_spec((1, d)), _const_spec((d, n))],
        out_specs=pl.BlockSpec((TM, n), lambda i: (i, 0)),
        out_shape=jax.ShapeDtypeStruct((m, n), F32),
        compiler_params=_params(("parallel",)),
        name="norm_proj",
    )(h, g.reshape(1, d), w)


def _rope_tables(head_dim, rows_per_batch, batch):
    r = head_dim // ROPE_FRAC
    half = r // 2
    inv = ROPE_THETA ** (-jnp.arange(half, dtype=F32) / half)
    ang = jnp.arange(rows_per_batch, dtype=F32)[:, None] * inv[None, :]
    cos, sin = jnp.cos(ang), jnp.sin(ang)
    ones = jnp.ones((rows_per_batch, head_dim - r), F32)
    zeros_h = jnp.zeros((rows_per_batch, half), F32)
    zeros_t = jnp.zeros((rows_per_batch, head_dim - r), F32)
    c = jnp.concatenate([cos, cos, ones], axis=1)
    s_up = jnp.concatenate([zeros_h, sin, zeros_t], axis=1)
    s_dn = jnp.concatenate([-sin, zeros_h, zeros_t], axis=1)
    reps = (batch, LANES // head_dim)
    return half, jnp.tile(c, reps), jnp.tile(s_up, reps), jnp.tile(s_dn, reps)


def _rope(ys, half, c, s_up, s_dn):
    return ys * c + pltpu.roll(ys, half, 1) * s_up + pltpu.roll(ys, LANES - half, 1) * s_dn


def _dsa_proj_kernel(h_ref, g_ref, w_ref, c128_ref, u128_ref, d128_ref, c64_ref, u64_ref, d64_ref,
                     q_ref, k_ref, v_ref, qi_ref, kw_ref, *, half128, half64):
    a = _rms(h_ref[...], g_ref[...]).astype(BF16)
    t128 = (half128, c128_ref[...], u128_ref[...], d128_ref[...])
    t64 = (half64, c64_ref[...], u64_ref[...], d64_ref[...])
    c0, c1, c2, c3 = Q_W, Q_W + KV_W, Q_W + 2 * KV_W, Q_W + 2 * KV_W + IQ_W

    def slabs(lo, hi):
        y = jnp.dot(a, w_ref[:, lo:hi], preferred_element_type=F32)
        return [y[:, s * LANES:(s + 1) * LANES] for s in range((hi - lo) // LANES)]

    for s, ys in enumerate(slabs(0, c0)):
        q_ref[:, s * LANES:(s + 1) * LANES] = (_rope(ys, *t128) * (HEAD_DIM ** -0.5)).astype(q_ref.dtype)
    for s, ys in enumerate(slabs(c0, c1)):
        k_ref[:, s * LANES:(s + 1) * LANES] = _rope(ys, *t128).astype(k_ref.dtype)
    v_ref[...] = jnp.dot(a, w_ref[:, c1:c2], preferred_element_type=F32).astype(v_ref.dtype)
    for s, ys in enumerate(slabs(c2, c3)):
        qi_ref[s] = _rope(ys, *t64).astype(qi_ref.dtype)
    (ys,) = slabs(c3, c3 + LANES)
    lane = lax.broadcasted_iota(jnp.int32, ys.shape, 1)
    kw_ref[...] = jnp.where(lane < IDX_DIM, _rope(ys, *t64), ys)


def _dsa_proj(h, g, w, rope128, rope64):
    m, d = h.shape
    n = w.shape[1]
    pairs = IQ_W // LANES
    row = lambda width: pl.BlockSpec((TM, width), lambda i: (i, 0))
    return pl.pallas_call(
        functools.partial(_dsa_proj_kernel, half128=rope128[0], half64=rope64[0]),
        grid=(m // TM,),
        in_specs=[row(d), _const_spec((1, d)), _const_spec((d, n))] + [row(LANES)] * 6,
        out_specs=[row(Q_W), row(KV_W), row(KV_W), pl.BlockSpec((pairs, TM, LANES), lambda i: (0, i, 0)), row(LANES)],
        out_shape=[jax.ShapeDtypeStruct((m, Q_W), BF16), jax.ShapeDtypeStruct((m, KV_W), BF16),
                   jax.ShapeDtypeStruct((m, KV_W), BF16), jax.ShapeDtypeStruct((pairs, m, LANES), BF16),
                   jax.ShapeDtypeStruct((m, LANES), F32)],
        compiler_params=_params(("parallel",)),
        name="dsa_proj",
    )(h, g.reshape(1, d), w, *rope128[1:], *rope64[1:])


def _matmul_norm_res_kernel(a_ref, w_ref, g_ref, r_ref, o_ref):
    y = jnp.dot(a_ref[...], w_ref[...], preferred_element_type=F32)
    o_ref[...] = r_ref[...] + _rms(y, g_ref[...])


def _matmul_norm_res(a, w, g, res):
    m, k = a.shape
    n = w.shape[1]
    return pl.pallas_call(
        _matmul_norm_res_kernel,
        grid=(m // TM,),
        in_specs=[pl.BlockSpec((TM, k), lambda i: (i, 0)), _const_spec((k, n)),
                  _const_spec((1, n)), pl.BlockSpec((TM, n), lambda i: (i, 0))],
        out_specs=pl.BlockSpec((TM, n), lambda i: (i, 0)),
        out_shape=jax.ShapeDtypeStruct((m, n), F32),
        compiler_params=_params(("parallel",)),
        name="matmul_norm_res",
    )(a, w, g.reshape(1, n), res)


def _mlp_kernel(h_ref, g_in_ref, g_out_ref, w1_ref, w2_ref, o_ref, a_ref, acc_ref):
    f = pl.program_id(1)

    @pl.when(f == 0)
    def _():
        a_ref[...] = _rms(h_ref[...], g_in_ref[...]).astype(BF16)
        acc_ref[...] = jnp.zeros_like(acc_ref)

    z = jnp.dot(a_ref[...], w1_ref[...], preferred_element_type=F32)
    z = jnp.square(jnp.maximum(z, 0.0)).astype(BF16)
    acc_ref[...] += jnp.dot(z, w2_ref[...], preferred_element_type=F32)

    @pl.when(f == pl.num_programs(1) - 1)
    def _():
        o_ref[...] = h_ref[...] + _rms(acc_ref[...], g_out_ref[...])


def _mlp(h, g_in, g_out, w1, w2, layer):
    m, d = h.shape
    ff = w1.shape[2]
    return pl.pallas_call(
        _mlp_kernel,
        grid=(m // TM, ff // TF),
        in_specs=[pl.BlockSpec((TM, d), lambda i, f: (i, 0)),
                  pl.BlockSpec((1, d), lambda i, f: (0, 0)), pl.BlockSpec((1, d), lambda i, f: (0, 0)),
                  pl.BlockSpec((None, d, TF), lambda i, f: (layer, 0, f)),
                  pl.BlockSpec((None, TF, d), lambda i, f: (layer, f, 0))],
        out_specs=pl.BlockSpec((TM, d), lambda i, f: (i, 0)),
        out_shape=jax.ShapeDtypeStruct((m, d), F32),
        scratch_shapes=[pltpu.VMEM((TM, d), BF16), pltpu.VMEM((TM, d), F32)],
        compiler_params=_params(("parallel", "arbitrary")),
        name="mlp",
    )(h, g_in.reshape(1, d), g_out.reshape(1, d), w1, w2)


def _ssm_prep_kernel(lre_ref, lim_ref, ldt_ref, bre_ref, bim_ref, abr_ref, abi_ref, bbr_ref, bbi_ref):
    ar, ai = lre_ref[...], lim_ref[...]
    dt = jnp.exp(ldt_ref[...])
    mag = jnp.exp(dt * ar)
    abr = mag * jnp.cos(dt * ai)
    abi = mag * jnp.sin(dt * ai)
    den = ar * ar + ai * ai
    zr, zi = abr - 1.0, abi
    fr = (zr * ar + zi * ai) / den
    fi = (zi * ar - zr * ai) / den
    br, bi = bre_ref[...], bim_ref[...]
    abr_ref[...] = abr
    abi_ref[...] = abi
    bbr_ref[...] = fr * br - fi * bi
    bbi_ref[...] = fr * bi + fi * br


def _ssm_prep(lam_re, lam_im, log_dt, b_re, b_im):
    g, n = lam_re.shape
    c = b_re.shape[-1]
    bt = lambda b: jnp.swapaxes(b, 1, 2)
    full = lambda *s: pl.BlockSpec(s, lambda: (0,) * len(s))
    return pl.pallas_call(
        _ssm_prep_kernel,
        in_specs=[full(g, 1, n), full(g, 1, n), full(g, 1, 1), full(g, c, n), full(g, c, n)],
        out_specs=[full(g, 1, n), full(g, 1, n), full(g, c, n), full(g, c, n)],
        out_shape=[jax.ShapeDtypeStruct((g, 1, n), F32)] * 2 + [jax.ShapeDtypeStruct((g, c, n), F32)] * 2,
        name="ssm_prep",
    )(lam_re.reshape(g, 1, n), lam_im.reshape(g, 1, n), log_dt.reshape(g, 1, 1), bt(b_re), bt(b_im))


def _block_diag_in(bb):
    g, c, n = bb.shape
    r = bb.reshape(g // GROUPS_PER_TILE, GROUPS_PER_TILE, c, n)
    eye = jnp.eye(GROUPS_PER_TILE, dtype=bb.dtype)
    return jnp.einsum('jgcn,gh->jgchn', r, eye).reshape(g // GROUPS_PER_TILE, GROUPS_PER_TILE * c, GROUPS_PER_TILE * n)


def _block_diag_out(cc):
    g, c, n = cc.shape
    r = cc.reshape(g // GROUPS_PER_TILE, GROUPS_PER_TILE, c, n)
    eye = jnp.eye(GROUPS_PER_TILE, dtype=cc.dtype)
    return jnp.einsum('jgcn,gh->jgnhc', r, eye).reshape(g // GROUPS_PER_TILE, GROUPS_PER_TILE * n, GROUPS_PER_TILE * c)


def _cmul(ar, ai, br, bi):
    return ar * br - ai * bi, ar * bi + ai * br


def _mixer0_kernel(u_ref, pw_ref, ps_ref, wb_ref, wc_ref, abr_ref, abi_ref, d_ref, wg_ref, o_ref,
                   pbuf, sbuf, ybuf, carry_ref):
    c = pl.program_id(1)
    L = u_ref.shape[0]
    sw = GROUPS_PER_TILE * SSM_STATE

    @pl.when(c == 0)
    def _():
        pbuf[0:POOL_HALO, :] = jnp.zeros((POOL_HALO, POOL_WIDTH), F32)
        carry_ref[...] = jnp.zeros_like(carry_ref)

    pbuf[POOL_HALO:POOL_HALO + L, :] = u_ref[:, 0:POOL_WIDTH]
    tpos = c * L + lax.broadcasted_iota(jnp.int32, (L, 1), 0)
    for g, win in enumerate(POOL_WINDOWS):
        sl = slice(g * POOL_GROUP, (g + 1) * POOL_GROUP)
        x = u_ref[:, sl]
        s = x
        for k in range(1, win):
            s = s + pbuf[POOL_HALO - k:POOL_HALO - k + L, sl]
        cnt = jnp.minimum(tpos + 1, win).astype(F32)
        diff = (s / cnt - x).astype(BF16)
        y = jnp.dot(diff, pw_ref[g], preferred_element_type=F32) * ps_ref[:, sl]
        o_ref[:, sl] = y.astype(o_ref.dtype)
    pbuf[0:POOL_HALO, :] = pbuf[L:L + POOL_HALO, :]

    row = lax.broadcasted_iota(jnp.int32, (SUBLANES, sw), 0)
    for j in range(SSM_WIDTH // LANES):
        lo = POOL_WIDTH + j * LANES
        uj = u_ref[:, lo:lo + LANES]
        sbuf[...] = jnp.dot(uj.astype(BF16), wb_ref[j], preferred_element_type=F32)
        a1r = jnp.broadcast_to(abr_ref[j], (SUBLANES, sw))
        a1i = jnp.broadcast_to(abi_ref[j], (SUBLANES, sw))
        pw_r, pw_i = [a1r], [a1i]
        for _ in range(SUBLANES - 1):
            nr, ni = _cmul(pw_r[-1], pw_i[-1], a1r, a1i)
            pw_r.append(nr)
            pw_i.append(ni)
        steps = []
        for k in (1, 2, 4):
            steps.append((k, jnp.where(row >= k, pw_r[k - 1], 0.0), jnp.where(row >= k, pw_i[k - 1], 0.0)))
        pcr, pci = pw_r[0], pw_i[0]
        for r in range(1, SUBLANES):
            pcr = jnp.where(row == r, pw_r[r], pcr)
            pci = jnp.where(row == r, pw_i[r], pci)

        def body(r, carry):
            cr, ci = carry
            rows = pl.ds(pl.multiple_of(r * SUBLANES, SUBLANES), SUBLANES)
            xr = sbuf[rows, 0:sw]
            xi = sbuf[rows, sw:2 * sw]
            for k, kr, ki in steps:
                sr, si = pltpu.roll(xr, k, 0), pltpu.roll(xi, k, 0)
                xr, xi = xr + kr * sr - ki * si, xi + kr * si + ki * sr
            xr, xi = xr + pcr * cr - pci * ci, xi + pcr * ci + pci * cr
            sbuf[rows, 0:sw] = xr
            sbuf[rows, sw:2 * sw] = xi
            last = slice(SUBLANES - 1, SUBLANES)
            return jnp.broadcast_to(xr[last], (SUBLANES, sw)), jnp.broadcast_to(xi[last], (SUBLANES, sw))

        cr0 = jnp.broadcast_to(carry_ref[j, 0:1, :], (SUBLANES, sw))
        ci0 = jnp.broadcast_to(carry_ref[j, 1:2, :], (SUBLANES, sw))
        cr, ci = lax.fori_loop(0, L // SUBLANES, body, (cr0, ci0))
        carry_ref[j, 0:1, :] = cr[0:1]
        carry_ref[j, 1:2, :] = ci[0:1]
        yj = jnp.dot(sbuf[...].astype(BF16), wc_ref[j], preferred_element_type=F32)
        ybuf[:, j * LANES:(j + 1) * LANES] = yj + d_ref[:, j * LANES:(j + 1) * LANES] * uj

    y = jax.nn.gelu(ybuf[...])
    gate = jax.nn.sigmoid(jnp.dot(y.astype(BF16), wg_ref[...], preferred_element_type=F32))
    o_ref[:, POOL_WIDTH:] = (y * gate).astype(o_ref.dtype)


def _mixer0(u, batch, pool_w, pool_scale, wb, wc, abr, abi, d, w_glu):
    m, width = u.shape
    nj = SSM_WIDTH // LANES
    sw = GROUPS_PER_TILE * SSM_STATE
    chunks = T_PAD // L_SSM
    return pl.pallas_call(
        _mixer0_kernel,
        grid=(batch, chunks),
        in_specs=[pl.BlockSpec((L_SSM, width), lambda b, c: (b * chunks + c, 0)),
                  _const_spec((len(POOL_WINDOWS), POOL_GROUP, POOL_GROUP)), _const_spec((1, POOL_WIDTH)),
                  _const_spec((nj, LANES, 2 * sw)), _const_spec((nj, 2 * sw, LANES)),
                  _const_spec((nj, 1, sw)), _const_spec((nj, 1, sw)),
                  _const_spec((1, SSM_WIDTH)), _const_spec((SSM_WIDTH, SSM_WIDTH))],
        out_specs=pl.BlockSpec((L_SSM, width), lambda b, c: (b * chunks + c, 0)),
        out_shape=jax.ShapeDtypeStruct((m, width), BF16),
        scratch_shapes=[pltpu.VMEM((POOL_HALO + L_SSM, POOL_WIDTH), F32),
                        pltpu.VMEM((L_SSM, 2 * sw), F32),
                        pltpu.VMEM((L_SSM, SSM_WIDTH), F32),
                        pltpu.VMEM((nj, 2, sw), F32)],
        compiler_params=_params(("parallel", "arbitrary")),
        name="mixer0",
    )(u, pool_w, pool_scale, wb, wc, abr, abi, d, w_glu)


CHUNK_SHIFT = CHUNK.bit_length() - 1
assert 1 << CHUNK_SHIFT == CHUNK
KB = 512
KB_SHIFT = KB.bit_length() - 1
assert 1 << KB_SHIFT == KB
N_KB = -(-T_PAD // KB)
KV_PER_PASS = 2


def _chunk_id(t):
    return jnp.where(t < N_META, 0, ((t - N_META) >> CHUNK_SHIFT) + 1)


def _fold_lanes(x, op):
    out = x[:, 0:LANES]
    for s in range(1, x.shape[1] // LANES):
        out = op(out, x[:, s * LANES:(s + 1) * LANES])
    return out


def _stage_keys(src, dst):
    for kb in range(N_KB):
        n = min(KB, T_PAD - kb * KB)
        dst[kb, 0:n, :] = src[kb * KB:kb * KB + n, :].astype(dst.dtype)
        if n < KB:
            dst[kb, n:KB, :] = jnp.zeros((KB - n, dst.shape[2]), dst.dtype)


def _dsa_kernel(qi_ref, wq_ref, q_ref, kw_ref, k_ref, v_ref, o_ref,
                kie, kio, kbuf, vbuf, key_buf, lg_buf, m_acc, l_acc, o_acc):
    i = pl.program_id(1)
    tq = q_ref.shape[0]
    nt = (((1,), (1,)), ((), ()))

    @pl.when(i == 0)
    def _():
        kw = kw_ref[...]
        lane = lax.broadcasted_iota(jnp.int32, kw.shape, 1)
        _stage_keys(jnp.where(lane < IDX_DIM, kw, 0.0), kie)
        _stage_keys(jnp.where(lane >= IDX_DIM, pltpu.roll(kw, IDX_DIM, 1), 0.0), kio)
        _stage_keys(k_ref, kbuf)
        _stage_keys(v_ref, vbuf)

    last_q = (i + 1) * tq - 1
    key_end = N_META + CHUNK * (((last_q - N_META) >> CHUNK_SHIFT) + 1)
    n_kb = jnp.minimum((key_end + KB - 1) >> KB_SHIFT, N_KB)

    w = wq_ref[...] * ((IDX_HEADS ** -0.5) * (IDX_DIM ** -0.5))
    pairs = IDX_HEADS // 2
    lhs = qi_ref[...].reshape(pairs * tq, LANES)
    qpos = i * tq + lax.broadcasted_iota(jnp.int32, (tq, 1), 0)

    def index_block(kb, carry):
        se = lax.dot_general(lhs, kie[kb], nt, preferred_element_type=F32)
        so = lax.dot_general(lhs, kio[kb], nt, preferred_element_type=F32)
        score = jnp.zeros((tq, KB), F32)
        for p in range(pairs):
            we = w[:, IDX_DIM + 2 * p:IDX_DIM + 2 * p + 1]
            wo = w[:, IDX_DIM + 2 * p + 1:IDX_DIM + 2 * p + 2]
            score = score + we * jnp.maximum(se[p * tq:(p + 1) * tq], 0.0)
            score = score + wo * jnp.maximum(so[p * tq:(p + 1) * tq], 0.0)
        kpos = kb * KB + lax.broadcasted_iota(jnp.int32, (1, KB), 1)
        allowed = (_chunk_id(kpos) <= _chunk_id(qpos)) & (kpos < T_REAL)
        bits = pltpu.bitcast(jnp.where(allowed, score, -jnp.inf), jnp.int32)
        key_buf[kb] = jnp.where(bits < 0, bits ^ 0x7FFFFFFF, bits)
        return carry

    lax.fori_loop(0, n_kb, index_block, 0)

    def count_ge(cand):
        def block(kb, acc):
            return acc + _fold_lanes(jnp.where(key_buf[kb] >= cand, 1.0, 0.0), jnp.add)
        acc = lax.fori_loop(0, n_kb, block, jnp.zeros((tq, LANES), F32))
        return jnp.sum(acc, axis=-1, keepdims=True)

    th = jnp.where(count_ge(0) >= TOPK, 0, INT_MIN).astype(jnp.int32)

    def search(j, th):
        cand = th | jnp.left_shift(jnp.int32(1), 30 - j)
        return jnp.where(count_ge(cand) >= TOPK, cand, th)

    th = lax.fori_loop(0, 31, search, th)
    th = jnp.maximum(th, NEG_INF_KEY + 1)

    rows = KV_GROUP * tq
    for first_kv in range(0, N_KV_HEADS, KV_PER_PASS):
        qs = []
        for c in range(KV_PER_PASS):
            h0 = (first_kv + c) * KV_GROUP
            qs.append(jnp.concatenate([q_ref[:, (h0 + g) * HEAD_DIM:(h0 + g + 1) * HEAD_DIM]
                                       for g in range(KV_GROUP)], axis=0))
        m_acc[...] = jnp.full(m_acc.shape, -jnp.inf, F32)
        l_acc[...] = jnp.zeros(l_acc.shape, F32)
        o_acc[...] = jnp.zeros(o_acc.shape, F32)

        def logits_block(kb, carry):
            sel = (key_buf[kb] >= th)[None]
            for c in range(KV_PER_PASS):
                kv = first_kv + c
                kk = kbuf[kb, :, kv * HEAD_DIM:(kv + 1) * HEAD_DIM]
                lg = lax.dot_general(qs[c], kk, nt, preferred_element_type=F32)
                lg = jnp.where(sel, lg.reshape(KV_GROUP, tq, KB), -jnp.inf).reshape(rows, KB)
                lg_buf[c, kb] = lg
                m_acc[c] = jnp.maximum(m_acc[c], _fold_lanes(lg, jnp.maximum))
            return carry

        lax.fori_loop(0, n_kb, logits_block, 0)
        mx = [jnp.max(m_acc[c], axis=-1, keepdims=True) for c in range(KV_PER_PASS)]

        def value_block(kb, carry):
            for c in range(KV_PER_PASS):
                kv = first_kv + c
                e = jnp.exp(lg_buf[c, kb] - mx[c])
                l_acc[c] += _fold_lanes(e, jnp.add)
                vv = vbuf[kb, :, kv * HEAD_DIM:(kv + 1) * HEAD_DIM]
                o_acc[c] += jnp.dot(e.astype(BF16), vv, preferred_element_type=F32)
            return carry

        lax.fori_loop(0, n_kb, value_block, 0)
        for c in range(KV_PER_PASS):
            out = o_acc[c] / jnp.sum(l_acc[c], axis=-1, keepdims=True)
            for g in range(KV_GROUP):
                hd = (first_kv + c) * KV_GROUP + g
                o_ref[:, hd * HEAD_DIM:(hd + 1) * HEAD_DIM] = out[g * tq:(g + 1) * tq].astype(o_ref.dtype)


def _dsa(qi, kw, q, k, v, batch):
    m = q.shape[0]
    nq = T_PAD // TQ
    pairs = qi.shape[0]
    rows = KV_GROUP * TQ
    return pl.pallas_call(
        _dsa_kernel,
        grid=(batch, nq),
        in_specs=[pl.BlockSpec((pairs, TQ, LANES), lambda b, i: (0, b * nq + i, 0)),
                  pl.BlockSpec((TQ, LANES), lambda b, i: (b * nq + i, 0)),
                  pl.BlockSpec((TQ, Q_W), lambda b, i: (b * nq + i, 0)),
                  pl.BlockSpec((T_PAD, LANES), lambda b, i: (b, 0)),
                  pl.BlockSpec((T_PAD, KV_W), lambda b, i: (b, 0)),
                  pl.BlockSpec((T_PAD, KV_W), lambda b, i: (b, 0))],
        out_specs=pl.BlockSpec((TQ, Q_W), lambda b, i: (b * nq + i, 0)),
        out_shape=jax.ShapeDtypeStruct((m, Q_W), BF16),
        scratch_shapes=[pltpu.VMEM((N_KB, KB, LANES), BF16), pltpu.VMEM((N_KB, KB, LANES), BF16),
                        pltpu.VMEM((N_KB, KB, KV_W), BF16), pltpu.VMEM((N_KB, KB, KV_W), BF16),
                        pltpu.VMEM((N_KB, TQ, KB), jnp.int32),
                        pltpu.VMEM((KV_PER_PASS, N_KB, rows, KB), F32),
                        pltpu.VMEM((KV_PER_PASS, rows, LANES), F32),
                        pltpu.VMEM((KV_PER_PASS, rows, LANES), F32),
                        pltpu.VMEM((KV_PER_PASS, rows, HEAD_DIM), F32)],
        compiler_params=_params(("parallel", "arbitrary")),
        name="dsa",
    )(qi, kw, q, kw, k, v)


def kernel(x, meta, norm_g, ab_w_in, pool_w, pool_scale, s5_lambda_re, s5_lambda_im, s5_log_dt,
           s5_b_re, s5_b_im, s5_c_re, s5_c_im, s5_d, s5_w_glu, ab_w_out, c_w_in, c_w_out, mlp_w1, mlp_w2):
    batch, seq, d = x.shape
    assert (seq + N_META, d) == (T_REAL, D_MODEL)
    depth = norm_g.shape[0]
    meta_b = jnp.broadcast_to(meta[None].astype(x.dtype), (batch, N_META, d))
    pad = jnp.zeros((batch, T_PAD - T_REAL, d), x.dtype)
    h = jnp.concatenate([meta_b, x, pad], axis=1).reshape(batch * T_PAD, d)

    rope128 = _rope_tables(HEAD_DIM, T_PAD, batch)
    rope64 = _rope_tables(IDX_DIM, T_PAD, batch)
    w1, w2 = mlp_w1.astype(BF16), mlp_w2.astype(BF16)

    for layer in range(depth):
        g = norm_g[layer]
        if layer % 2 == 0:
            e = layer // 2
            u = _norm_proj(h, g[0], ab_w_in[e].astype(BF16))
            abr, abi, bbr, bbi = _ssm_prep(s5_lambda_re[e], s5_lambda_im[e], s5_log_dt[e], s5_b_re[e], s5_b_im[e])
            nj = SSM_GROUPS // GROUPS_PER_TILE
            wb = jnp.concatenate([_block_diag_in(bbr), _block_diag_in(bbi)], axis=-1).astype(BF16)
            wc = jnp.concatenate([_block_diag_out(s5_c_re[e]), -_block_diag_out(s5_c_im[e])], axis=1).astype(BF16)
            mix = _mixer0(u, batch, pool_w[e].astype(BF16), pool_scale[e].reshape(1, POOL_WIDTH), wb, wc,
                          abr.reshape(nj, 1, GROUPS_PER_TILE * SSM_STATE), abi.reshape(nj, 1, GROUPS_PER_TILE * SSM_STATE),
                          s5_d[e].reshape(1, SSM_WIDTH), s5_w_glu[e].astype(BF16))
            w_out = ab_w_out[e]
        else:
            o = layer // 2
            n_in = c_w_in.shape[-1]
            w_in = jnp.pad(c_w_in[o].astype(BF16), ((0, 0), (0, -n_in % LANES)))
            q, k, v, qi, kw = _dsa_proj(h, g[0], w_in, rope128, rope64)
            mix = _dsa(qi, kw, q, k, v, batch)
            w_out = c_w_out[o]
        h = _matmul_norm_res(mix, w_out.astype(BF16), g[1], h)
        h = _mlp(h, g[2], g[3], w1, w2, layer)
    return h.reshape(batch, T_PAD, d)[:, N_META:T_REAL]
```

```python
import functools

import jax
import jax.numpy as jnp
from jax import lax
from jax.experimental import pallas as pl
from jax.experimental.pallas import tpu as pltpu

F32 = jnp.float32
BF16 = jnp.bfloat16

D_MODEL = 2048
CHUNK = 64
N_META = 16
POOL_WIDTH = 1024
POOL_WINDOWS = (2, 4, 8, 16)
POOL_GROUP = 256
POOL_HALO = 16
SSM_WIDTH = 1024
SSM_GROUP_CH = 16
SSM_GROUPS = 64
SSM_STATE = 64
GROUPS_PER_TILE = 8
HEAD_DIM = 128
N_HEADS = 16
N_KV_HEADS = 4
KV_GROUP = 4
IDX_HEADS = 16
IDX_DIM = 64
ROPE_THETA = 500000.0
ROPE_FRAC = 4
TOPK = 256
EPS = 1e-6
Q_W = N_HEADS * HEAD_DIM
KV_W = N_KV_HEADS * HEAD_DIM
IQ_W = IDX_HEADS * IDX_DIM

LANES = 128
SUBLANES = 8
VMEM_LIMIT = 56 * 1024 * 1024

T_REAL = N_META + 2048
T_PAD = 2176
TM = 512
TQ = 128
L_SSM = 544
TF = 1024

LOG2_E = 1.4426950408889634
INT_MIN = -2 ** 31
NEG_INF_KEY = -2139095041


def _params(sem):
    return pltpu.CompilerParams(dimension_semantics=sem, vmem_limit_bytes=VMEM_LIMIT)


def _rms(x, g):
    return x * lax.rsqrt(jnp.mean(x * x, axis=-1, keepdims=True) + EPS) * g


def _const_spec(shape):
    zeros = (0,) * len(shape)
    return pl.BlockSpec(shape, lambda *_: zeros, pipeline_mode=pl.Buffered(1))


def _norm_proj_kernel(h_ref, g_ref, w_ref, o_ref):
    a = _rms(h_ref[...], g_ref[...]).astype(BF16)
    o_ref[...] = jnp.dot(a, w_ref[...], preferred_element_type=F32)


def _norm_proj(h, g, w):
    m, d = h.shape
    n = w.shape[1]
    return pl.pallas_call(
        _norm_proj_kernel,
        grid=(m // TM,),
        in_specs=[pl.BlockSpec((TM, d), lambda i: (i, 0)), _const_spec((1, d)), _const_spec((d, n))],
        out_specs=pl.BlockSpec((TM, n), lambda i: (i, 0)),
        out_shape=jax.ShapeDtypeStruct((m, n), F32),
        compiler_params=_params(("parallel",)),
        name="norm_proj",
    )(h, g.reshape(1, d), w)


def _rope_tables(head_dim, rows_per_batch, batch):
    r = head_dim // ROPE_FRAC
    half = r // 2
    inv = ROPE_THETA ** (-jnp.arange(half, dtype=F32) / half)
    ang = jnp.arange(rows_per_batch, dtype=F32)[:, None] * inv[None, :]
    cos, sin = jnp.cos(ang), jnp.sin(ang)
    ones = jnp.ones((rows_per_batch, head_dim - r), F32)
    zeros_h = jnp.zeros((rows_per_batch, half), F32)
    zeros_t = jnp.zeros((rows_per_batch, head_dim - r), F32)
    c = jnp.concatenate([cos, cos, ones], axis=1)
    s_up = jnp.concatenate([zeros_h, sin, zeros_t], axis=1)
    s_dn = jnp.concatenate([-sin, zeros_h, zeros_t], axis=1)
    reps = (batch, LANES // head_dim)
    return half, jnp.tile(c, reps), jnp.tile(s_up, reps), jnp.tile(s_dn, reps)


def _rope(ys, half, c, s_up, s_dn):
    return ys * c + pltpu.roll(ys, half, 1) * s_up + pltpu.roll(ys, LANES - half, 1) * s_dn


def _dsa_proj_kernel(h_ref, g_ref, w_ref, wvt_ref, c128_ref, u128_ref, d128_ref, c64_ref, u64_ref, d64_ref,
                     q_ref, k_ref, vt_ref, qi_ref, kw_ref, *, half128, half64):
    a = _rms(h_ref[...], g_ref[...]).astype(BF16)
    t128 = (half128, c128_ref[...], u128_ref[...], d128_ref[...])
    t64 = (half64, c64_ref[...], u64_ref[...], d64_ref[...])
    c0, c1, c2, c3 = Q_W, Q_W + KV_W, Q_W + 2 * KV_W, Q_W + 2 * KV_W + IQ_W

    def slabs(lo, hi):
        y = jnp.dot(a, w_ref[:, lo:hi], preferred_element_type=F32)
        return [y[:, s * LANES:(s + 1) * LANES] for s in range((hi - lo) // LANES)]

    for s, ys in enumerate(slabs(0, c0)):
        q_ref[:, s * LANES:(s + 1) * LANES] = (_rope(ys, *t128) * (HEAD_DIM ** -0.5 * LOG2_E)).astype(q_ref.dtype)
    for s, ys in enumerate(slabs(c0, c1)):
        k_ref[:, s * LANES:(s + 1) * LANES] = _rope(ys, *t128).astype(k_ref.dtype)
    vt_ref[...] = lax.dot_general(wvt_ref[...], a, (((1,), (1,)), ((), ())),
                                  preferred_element_type=F32).astype(vt_ref.dtype)
    for s, ys in enumerate(slabs(c2, c3)):
        qi_ref[s] = _rope(ys, *t64).astype(qi_ref.dtype)
    (ys,) = slabs(c3, c3 + LANES)
    lane = lax.broadcasted_iota(jnp.int32, ys.shape, 1)
    kw_ref[...] = jnp.where(lane < IDX_DIM, _rope(ys, *t64), ys)


def _dsa_proj(h, g, w, w_vt, rope128, rope64):
    m, d = h.shape
    n = w.shape[1]
    pairs = IQ_W // LANES
    row = lambda width: pl.BlockSpec((TM, width), lambda i: (i, 0))
    return pl.pallas_call(
        functools.partial(_dsa_proj_kernel, half128=rope128[0], half64=rope64[0]),
        grid=(m // TM,),
        in_specs=[row(d), _const_spec((1, d)), _const_spec((d, n)), _const_spec((KV_W, d))] + [row(LANES)] * 6,
        out_specs=[row(Q_W), row(KV_W), pl.BlockSpec((KV_W, TM), lambda i: (0, i)),
                   pl.BlockSpec((pairs, TM, LANES), lambda i: (0, i, 0)), row(LANES)],
        out_shape=[jax.ShapeDtypeStruct((m, Q_W), BF16), jax.ShapeDtypeStruct((m, KV_W), BF16),
                   jax.ShapeDtypeStruct((KV_W, m), BF16), jax.ShapeDtypeStruct((pairs, m, LANES), BF16),
                   jax.ShapeDtypeStruct((m, LANES), F32)],
        compiler_params=_params(("parallel",)),
        name="dsa_proj",
    )(h, g.reshape(1, d), w, w_vt, *rope128[1:], *rope64[1:])


def _matmul_norm_res_kernel(a_ref, w_ref, g_ref, r_ref, o_ref):
    y = jnp.dot(a_ref[...], w_ref[...], preferred_element_type=F32)
    o_ref[...] = r_ref[...] + _rms(y, g_ref[...])


def _matmul_norm_res(a, w, g, res):
    m, k = a.shape
    n = w.shape[1]
    return pl.pallas_call(
        _matmul_norm_res_kernel,
        grid=(m // TM,),
        in_specs=[pl.BlockSpec((TM, k), lambda i: (i, 0)), _const_spec((k, n)),
                  _const_spec((1, n)), pl.BlockSpec((TM, n), lambda i: (i, 0))],
        out_specs=pl.BlockSpec((TM, n), lambda i: (i, 0)),
        out_shape=jax.ShapeDtypeStruct((m, n), F32),
        compiler_params=_params(("parallel",)),
        name="matmul_norm_res",
    )(a, w, g.reshape(1, n), res)


def _mlp_kernel(h_ref, g_in_ref, g_out_ref, w1_ref, w2_ref, o_ref, a_ref, acc_ref):
    f = pl.program_id(1)

    @pl.when(f == 0)
    def _():
        a_ref[...] = _rms(h_ref[...], g_in_ref[...]).astype(BF16)
        acc_ref[...] = jnp.zeros_like(acc_ref)

    z = jnp.dot(a_ref[...], w1_ref[...], preferred_element_type=F32)
    z = jnp.square(jnp.maximum(z, 0.0)).astype(BF16)
    acc_ref[...] += jnp.dot(z, w2_ref[...], preferred_element_type=F32)

    @pl.when(f == pl.num_programs(1) - 1)
    def _():
        o_ref[...] = h_ref[...] + _rms(acc_ref[...], g_out_ref[...])


def _mlp(h, g_in, g_out, w1, w2, layer):
    m, d = h.shape
    ff = w1.shape[2]
    return pl.pallas_call(
        _mlp_kernel,
        grid=(m // TM, ff // TF),
        in_specs=[pl.BlockSpec((TM, d), lambda i, f: (i, 0)),
                  pl.BlockSpec((1, d), lambda i, f: (0, 0)), pl.BlockSpec((1, d), lambda i, f: (0, 0)),
                  pl.BlockSpec((None, d, TF), lambda i, f: (layer, 0, f)),
                  pl.BlockSpec((None, TF, d), lambda i, f: (layer, f, 0))],
        out_specs=pl.BlockSpec((TM, d), lambda i, f: (i, 0)),
        out_shape=jax.ShapeDtypeStruct((m, d), F32),
        scratch_shapes=[pltpu.VMEM((TM, d), BF16), pltpu.VMEM((TM, d), F32)],
        compiler_params=_params(("parallel", "arbitrary")),
        name="mlp",
    )(h, g_in.reshape(1, d), g_out.reshape(1, d), w1, w2)


def _ssm_prep_kernel(lre_ref, lim_ref, ldt_ref, bre_ref, bim_ref, abr_ref, abi_ref, bbr_ref, bbi_ref):
    ar, ai = lre_ref[...], lim_ref[...]
    dt = jnp.exp(ldt_ref[...])
    mag = jnp.exp(dt * ar)
    abr = mag * jnp.cos(dt * ai)
    abi = mag * jnp.sin(dt * ai)
    den = ar * ar + ai * ai
    zr, zi = abr - 1.0, abi
    fr = (zr * ar + zi * ai) / den
    fi = (zi * ar - zr * ai) / den
    br, bi = bre_ref[...], bim_ref[...]
    abr_ref[...] = abr
    abi_ref[...] = abi
    bbr_ref[...] = fr * br - fi * bi
    bbi_ref[...] = fr * bi + fi * br


def _ssm_prep(lam_re, lam_im, log_dt, b_re, b_im):
    g, n = lam_re.shape
    c = b_re.shape[-1]
    bt = lambda b: jnp.swapaxes(b, 1, 2)
    full = lambda *s: pl.BlockSpec(s, lambda: (0,) * len(s))
    return pl.pallas_call(
        _ssm_prep_kernel,
        in_specs=[full(g, 1, n), full(g, 1, n), full(g, 1, 1), full(g, c, n), full(g, c, n)],
        out_specs=[full(g, 1, n), full(g, 1, n), full(g, c, n), full(g, c, n)],
        out_shape=[jax.ShapeDtypeStruct((g, 1, n), F32)] * 2 + [jax.ShapeDtypeStruct((g, c, n), F32)] * 2,
        name="ssm_prep",
    )(lam_re.reshape(g, 1, n), lam_im.reshape(g, 1, n), log_dt.reshape(g, 1, 1), bt(b_re), bt(b_im))


def _block_diag_in(bb):
    g, c, n = bb.shape
    r = bb.reshape(g // GROUPS_PER_TILE, GROUPS_PER_TILE, c, n)
    eye = jnp.eye(GROUPS_PER_TILE, dtype=bb.dtype)
    return jnp.einsum('jgcn,gh->jgchn', r, eye).reshape(g // GROUPS_PER_TILE, GROUPS_PER_TILE * c, GROUPS_PER_TILE * n)


def _block_diag_out(cc):
    g, c, n = cc.shape
    r = cc.reshape(g // GROUPS_PER_TILE, GROUPS_PER_TILE, c, n)
    eye = jnp.eye(GROUPS_PER_TILE, dtype=cc.dtype)
    return jnp.einsum('jgcn,gh->jgnhc', r, eye).reshape(g // GROUPS_PER_TILE, GROUPS_PER_TILE * n, GROUPS_PER_TILE * c)


def _cmul(ar, ai, br, bi):
    return ar * br - ai * bi, ar * bi + ai * br


def _mixer0_kernel(u_ref, pw_ref, ps_ref, wb_ref, wc_ref, abr_ref, abi_ref, d_ref, wg_ref, o_ref,
                   pbuf, sbuf, ybuf, carry_ref):
    c = pl.program_id(1)
    L = u_ref.shape[0]
    sw = GROUPS_PER_TILE * SSM_STATE

    @pl.when(c == 0)
    def _():
        pbuf[0:POOL_HALO, :] = jnp.zeros((POOL_HALO, POOL_WIDTH), F32)
        carry_ref[...] = jnp.zeros_like(carry_ref)

    pbuf[POOL_HALO:POOL_HALO + L, :] = u_ref[:, 0:POOL_WIDTH]
    tpos = c * L + lax.broadcasted_iota(jnp.int32, (L, 1), 0)
    for g, win in enumerate(POOL_WINDOWS):
        sl = slice(g * POOL_GROUP, (g + 1) * POOL_GROUP)
        x = u_ref[:, sl]
        s = x
        for k in range(1, win):
            s = s + pbuf[POOL_HALO - k:POOL_HALO - k + L, sl]
        cnt = jnp.minimum(tpos + 1, win).astype(F32)
        diff = (s / cnt - x).astype(BF16)
        y = jnp.dot(diff, pw_ref[g], preferred_element_type=F32) * ps_ref[:, sl]
        o_ref[:, sl] = y.astype(o_ref.dtype)
    pbuf[0:POOL_HALO, :] = pbuf[L:L + POOL_HALO, :]

    row = lax.broadcasted_iota(jnp.int32, (SUBLANES, sw), 0)
    for j in range(SSM_WIDTH // LANES):
        lo = POOL_WIDTH + j * LANES
        uj = u_ref[:, lo:lo + LANES]
        sbuf[...] = jnp.dot(uj.astype(BF16), wb_ref[j], preferred_element_type=F32)
        a1r = jnp.broadcast_to(abr_ref[j], (SUBLANES, sw))
        a1i = jnp.broadcast_to(abi_ref[j], (SUBLANES, sw))
        pw_r, pw_i = [a1r], [a1i]
        for _ in range(SUBLANES - 1):
            nr, ni = _cmul(pw_r[-1], pw_i[-1], a1r, a1i)
            pw_r.append(nr)
            pw_i.append(ni)
        steps = []
        for k in (1, 2, 4):
            steps.append((k, jnp.where(row >= k, pw_r[k - 1], 0.0), jnp.where(row >= k, pw_i[k - 1], 0.0)))
        pcr, pci = pw_r[0], pw_i[0]
        for r in range(1, SUBLANES):
            pcr = jnp.where(row == r, pw_r[r], pcr)
            pci = jnp.where(row == r, pw_i[r], pci)

        def body(r, carry):
            cr, ci = carry
            rows = pl.ds(pl.multiple_of(r * SUBLANES, SUBLANES), SUBLANES)
            xr = sbuf[rows, 0:sw]
            xi = sbuf[rows, sw:2 * sw]
            for k, kr, ki in steps:
                sr, si = pltpu.roll(xr, k, 0), pltpu.roll(xi, k, 0)
                xr, xi = xr + kr * sr - ki * si, xi + kr * si + ki * sr
            xr, xi = xr + pcr * cr - pci * ci, xi + pcr * ci + pci * cr
            sbuf[rows, 0:sw] = xr
            sbuf[rows, sw:2 * sw] = xi
            last = slice(SUBLANES - 1, SUBLANES)
            return jnp.broadcast_to(xr[last], (SUBLANES, sw)), jnp.broadcast_to(xi[last], (SUBLANES, sw))

        cr0 = jnp.broadcast_to(carry_ref[j, 0:1, :], (SUBLANES, sw))
        ci0 = jnp.broadcast_to(carry_ref[j, 1:2, :], (SUBLANES, sw))
        cr, ci = lax.fori_loop(0, L // SUBLANES, body, (cr0, ci0))
        carry_ref[j, 0:1, :] = cr[0:1]
        carry_ref[j, 1:2, :] = ci[0:1]
        yj = jnp.dot(sbuf[...].astype(BF16), wc_ref[j], preferred_element_type=F32)
        ybuf[:, j * LANES:(j + 1) * LANES] = yj + d_ref[:, j * LANES:(j + 1) * LANES] * uj

    y = jax.nn.gelu(ybuf[...])
    gate = jax.nn.sigmoid(jnp.dot(y.astype(BF16), wg_ref[...], preferred_element_type=F32))
    o_ref[:, POOL_WIDTH:] = (y * gate).astype(o_ref.dtype)


def _mixer0(u, batch, pool_w, pool_scale, wb, wc, abr, abi, d, w_glu):
    m, width = u.shape
    nj = SSM_WIDTH // LANES
    sw = GROUPS_PER_TILE * SSM_STATE
    chunks = T_PAD // L_SSM
    return pl.pallas_call(
        _mixer0_kernel,
        grid=(batch, chunks),
        in_specs=[pl.BlockSpec((L_SSM, width), lambda b, c: (b * chunks + c, 0)),
                  _const_spec((len(POOL_WINDOWS), POOL_GROUP, POOL_GROUP)), _const_spec((1, POOL_WIDTH)),
                  _const_spec((nj, LANES, 2 * sw)), _const_spec((nj, 2 * sw, LANES)),
                  _const_spec((nj, 1, sw)), _const_spec((nj, 1, sw)),
                  _const_spec((1, SSM_WIDTH)), _const_spec((SSM_WIDTH, SSM_WIDTH))],
        out_specs=pl.BlockSpec((L_SSM, width), lambda b, c: (b * chunks + c, 0)),
        out_shape=jax.ShapeDtypeStruct((m, width), BF16),
        scratch_shapes=[pltpu.VMEM((POOL_HALO + L_SSM, POOL_WIDTH), F32),
                        pltpu.VMEM((L_SSM, 2 * sw), F32),
                        pltpu.VMEM((L_SSM, SSM_WIDTH), F32),
                        pltpu.VMEM((nj, 2, sw), F32)],
        compiler_params=_params(("parallel", "arbitrary")),
        name="mixer0",
    )(u, pool_w, pool_scale, wb, wc, abr, abi, d, w_glu)


CHUNK_SHIFT = CHUNK.bit_length() - 1
assert 1 << CHUNK_SHIFT == CHUNK
KB = 512
KB_SHIFT = KB.bit_length() - 1
assert 1 << KB_SHIFT == KB
N_KB = -(-T_PAD // KB)
COUNT_ROWS = 64


def _chunk_id(t):
    return jnp.where(t < N_META, 0, ((t - N_META) >> CHUNK_SHIFT) + 1)


def _fold_rows(x, op):
    return op(x.reshape(x.shape[0] // SUBLANES, SUBLANES, x.shape[1]), axis=0)


def _stage_rows(src, dst):
    for kb in range(N_KB):
        n = min(KB, T_PAD - kb * KB)
        dst[kb, 0:n, :] = src[kb * KB:kb * KB + n, :].astype(dst.dtype)
        if n < KB:
            dst[kb, n:KB, :] = jnp.zeros((KB - n, dst.shape[2]), dst.dtype)


def _stage_cols(src, dst):
    for kb in range(N_KB):
        n = min(KB, T_PAD - kb * KB)
        dst[kb, :, 0:n] = src[:, kb * KB:kb * KB + n]
        if n < KB:
            dst[kb, :, n:KB] = jnp.zeros((dst.shape[1], KB - n), dst.dtype)


def _dsa_kernel(qi_ref, wq_ref, q_ref, kw_ref, k_ref, vt_ref, o_ref,
                kie, kio, kbuf, vtbuf, key_buf, lg_buf, m_acc, l_acc, o_acc):
    i = pl.program_id(1)
    tq = q_ref.shape[0]
    nt = (((1,), (1,)), ((), ()))

    @pl.when(i == 0)
    def _():
        kw = kw_ref[...]
        lane = lax.broadcasted_iota(jnp.int32, kw.shape, 1)
        _stage_rows(jnp.where(lane < IDX_DIM, kw, 0.0), kie)
        _stage_rows(jnp.where(lane >= IDX_DIM, pltpu.roll(kw, IDX_DIM, 1), 0.0), kio)
        _stage_rows(k_ref, kbuf)
        _stage_cols(vt_ref, vtbuf)

    last_q = (i + 1) * tq - 1
    key_end = N_META + CHUNK * (((last_q - N_META) >> CHUNK_SHIFT) + 1)
    n_kb = jnp.minimum((key_end + KB - 1) >> KB_SHIFT, N_KB)

    w_t = wq_ref[...].T * ((IDX_HEADS ** -0.5) * (IDX_DIM ** -0.5))
    pairs = IDX_HEADS // 2
    qi_all = qi_ref[...].reshape(pairs * tq, LANES)
    qpos = i * tq + lax.broadcasted_iota(jnp.int32, (1, tq), 1)

    def index_block(kb, carry):
        se = lax.dot_general(kie[kb], qi_all, nt, preferred_element_type=F32)
        so = lax.dot_general(kio[kb], qi_all, nt, preferred_element_type=F32)
        score = jnp.zeros((KB, tq), F32)
        for p in range(pairs):
            we = w_t[IDX_DIM + 2 * p:IDX_DIM + 2 * p + 1, :]
            wo = w_t[IDX_DIM + 2 * p + 1:IDX_DIM + 2 * p + 2, :]
            score = score + we * jnp.maximum(se[:, p * tq:(p + 1) * tq], 0.0)
            score = score + wo * jnp.maximum(so[:, p * tq:(p + 1) * tq], 0.0)
        kpos = kb * KB + lax.broadcasted_iota(jnp.int32, (KB, 1), 0)
        allowed = (_chunk_id(kpos) <= _chunk_id(qpos)) & (kpos < T_REAL)
        bits = pltpu.bitcast(jnp.where(allowed, score, -jnp.inf), jnp.int32)
        key_buf[kb] = jnp.where(bits < 0, bits ^ 0x7FFFFFFF, bits)
        return carry

    lax.fori_loop(0, n_kb, index_block, 0)

    def count_ge(cand):
        def block(kb, acc):
            ind = jnp.where(key_buf[kb] >= cand, 1.0, 0.0)
            return acc + jnp.sum(ind.reshape(KB // COUNT_ROWS, COUNT_ROWS, tq), axis=0)
        acc = lax.fori_loop(0, n_kb, block, jnp.zeros((COUNT_ROWS, tq), F32))
        return jnp.sum(acc, axis=0, keepdims=True)

    th = jnp.where(count_ge(0) >= TOPK, 0, INT_MIN).astype(jnp.int32)

    def search(j, th):
        cand = th | jnp.left_shift(jnp.int32(1), 30 - j)
        return jnp.where(count_ge(cand) >= TOPK, cand, th)

    th = lax.fori_loop(0, 31, search, th)
    th = jnp.maximum(th, NEG_INF_KEY + 1)

    qs = []
    for kv in range(N_KV_HEADS):
        h0 = kv * KV_GROUP
        qs.append(jnp.concatenate([q_ref[:, (h0 + g) * HEAD_DIM:(h0 + g + 1) * HEAD_DIM]
                                   for g in range(KV_GROUP)], axis=0))
    m_acc[...] = jnp.full(m_acc.shape, -jnp.inf, F32)
    l_acc[...] = jnp.zeros(l_acc.shape, F32)
    o_acc[...] = jnp.zeros(o_acc.shape, F32)

    def logits_block(kb, carry):
        sel = key_buf[kb] >= th
        for kv in range(N_KV_HEADS):
            kk = kbuf[kb, :, kv * HEAD_DIM:(kv + 1) * HEAD_DIM]
            lg = lax.dot_general(kk, qs[kv], nt, preferred_element_type=F32)
            lg = jnp.concatenate([jnp.where(sel, lg[:, g * tq:(g + 1) * tq], -jnp.inf)
                                  for g in range(KV_GROUP)], axis=1)
            lg_buf[kv, kb] = lg
            m_acc[kv] = jnp.maximum(m_acc[kv], _fold_rows(lg, jnp.max))
        return carry

    lax.fori_loop(0, n_kb, logits_block, 0)
    mx = [jnp.max(m_acc[kv], axis=0, keepdims=True) for kv in range(N_KV_HEADS)]

    def value_block(kb, carry):
        for kv in range(N_KV_HEADS):
            e = jnp.exp2(lg_buf[kv, kb] - mx[kv])
            l_acc[kv] += _fold_rows(e, jnp.sum)
            vt = vtbuf[kb, kv * HEAD_DIM:(kv + 1) * HEAD_DIM, :]
            o_acc[kv] += jnp.dot(vt, e.astype(BF16), preferred_element_type=F32)
        return carry

    lax.fori_loop(0, n_kb, value_block, 0)
    for kv in range(N_KV_HEADS):
        out_t = o_acc[kv] / jnp.sum(l_acc[kv], axis=0, keepdims=True)
        for g in range(KV_GROUP):
            hd = kv * KV_GROUP + g
            o_ref[:, hd * HEAD_DIM:(hd + 1) * HEAD_DIM] = out_t[:, g * tq:(g + 1) * tq].T.astype(o_ref.dtype)


def _dsa(qi, kw, q, k, vt, batch):
    m = q.shape[0]
    nq = T_PAD // TQ
    pairs = qi.shape[0]
    cols = KV_GROUP * TQ
    return pl.pallas_call(
        _dsa_kernel,
        grid=(batch, nq),
        in_specs=[pl.BlockSpec((pairs, TQ, LANES), lambda b, i: (0, b * nq + i, 0)),
                  pl.BlockSpec((TQ, LANES), lambda b, i: (b * nq + i, 0)),
                  pl.BlockSpec((TQ, Q_W), lambda b, i: (b * nq + i, 0)),
                  pl.BlockSpec((T_PAD, LANES), lambda b, i: (b, 0)),
                  pl.BlockSpec((T_PAD, KV_W), lambda b, i: (b, 0)),
                  pl.BlockSpec((KV_W, T_PAD), lambda b, i: (0, b))],
        out_specs=pl.BlockSpec((TQ, Q_W), lambda b, i: (b * nq + i, 0)),
        out_shape=jax.ShapeDtypeStruct((m, Q_W), BF16),
        scratch_shapes=[pltpu.VMEM((N_KB, KB, LANES), BF16), pltpu.VMEM((N_KB, KB, LANES), BF16),
                        pltpu.VMEM((N_KB, KB, KV_W), BF16), pltpu.VMEM((N_KB, KV_W, KB), BF16),
                        pltpu.VMEM((N_KB, KB, TQ), jnp.int32),
                        pltpu.VMEM((N_KV_HEADS, N_KB, KB, cols), F32),
                        pltpu.VMEM((N_KV_HEADS, SUBLANES, cols), F32),
                        pltpu.VMEM((N_KV_HEADS, SUBLANES, cols), F32),
                        pltpu.VMEM((N_KV_HEADS, HEAD_DIM, cols), F32)],
        compiler_params=_params(("parallel", "arbitrary")),
        name="dsa",
    )(qi, kw, q, kw, k, vt)


def kernel(x, meta, norm_g, ab_w_in, pool_w, pool_scale, s5_lambda_re, s5_lambda_im, s5_log_dt,
           s5_b_re, s5_b_im, s5_c_re, s5_c_im, s5_d, s5_w_glu, ab_w_out, c_w_in, c_w_out, mlp_w1, mlp_w2):
    batch, seq, d = x.shape
    assert (seq + N_META, d) == (T_REAL, D_MODEL)
    depth = norm_g.shape[0]
    meta_b = jnp.broadcast_to(meta[None].astype(x.dtype), (batch, N_META, d))
    pad = jnp.zeros((batch, T_PAD - T_REAL, d), x.dtype)
    h = jnp.concatenate([meta_b, x, pad], axis=1).reshape(batch * T_PAD, d)

    rope128 = _rope_tables(HEAD_DIM, T_PAD, batch)
    rope64 = _rope_tables(IDX_DIM, T_PAD, batch)
    w1, w2 = mlp_w1.astype(BF16), mlp_w2.astype(BF16)

    for layer in range(depth):
        g = norm_g[layer]
        if layer % 2 == 0:
            e = layer // 2
            u = _norm_proj(h, g[0], ab_w_in[e].astype(BF16))
            abr, abi, bbr, bbi = _ssm_prep(s5_lambda_re[e], s5_lambda_im[e], s5_log_dt[e], s5_b_re[e], s5_b_im[e])
            nj = SSM_GROUPS // GROUPS_PER_TILE
            wb = jnp.concatenate([_block_diag_in(bbr), _block_diag_in(bbi)], axis=-1).astype(BF16)
            wc = jnp.concatenate([_block_diag_out(s5_c_re[e]), -_block_diag_out(s5_c_im[e])], axis=1).astype(BF16)
            mix = _mixer0(u, batch, pool_w[e].astype(BF16), pool_scale[e].reshape(1, POOL_WIDTH), wb, wc,
                          abr.reshape(nj, 1, GROUPS_PER_TILE * SSM_STATE), abi.reshape(nj, 1, GROUPS_PER_TILE * SSM_STATE),
                          s5_d[e].reshape(1, SSM_WIDTH), s5_w_glu[e].astype(BF16))
            w_out = ab_w_out[e]
        else:
            o = layer // 2
            n_in = c_w_in.shape[-1]
            w_in = jnp.pad(c_w_in[o].astype(BF16), ((0, 0), (0, -n_in % LANES)))
            w_vt = w_in[:, Q_W + KV_W:Q_W + 2 * KV_W].T
            q, k, vt, qi, kw = _dsa_proj(h, g[0], w_in, w_vt, rope128, rope64)
            mix = _dsa(qi, kw, q, k, vt, batch)
            w_out = c_w_out[o]
        h = _matmul_norm_res(mix, w_out.astype(BF16), g[1], h)
        h = _mlp(h, g[2], g[3], w1, w2, layer)
    return h.reshape(batch, T_PAD, d)[:, N_META:T_REAL]
```

```python
import functools

import jax
import jax.numpy as jnp
from jax import lax
from jax.experimental import pallas as pl
from jax.experimental.pallas import tpu as pltpu

F32 = jnp.float32
BF16 = jnp.bfloat16

D_MODEL = 2048
CHUNK = 64
N_META = 16
POOL_WIDTH = 1024
POOL_WINDOWS = (2, 4, 8, 16)
POOL_GROUP = 256
POOL_HALO = 16
SSM_WIDTH = 1024
SSM_GROUP_CH = 16
SSM_GROUPS = 64
SSM_STATE = 64
GROUPS_PER_TILE = 8
HEAD_DIM = 128
N_HEADS = 16
N_KV_HEADS = 4
KV_GROUP = 4
IDX_HEADS = 16
IDX_DIM = 64
ROPE_THETA = 500000.0
ROPE_FRAC = 4
TOPK = 256
EPS = 1e-6
Q_W = N_HEADS * HEAD_DIM
KV_W = N_KV_HEADS * HEAD_DIM
IQ_W = IDX_HEADS * IDX_DIM

LANES = 128
SUBLANES = 8
VMEM_LIMIT = 56 * 1024 * 1024

T_REAL = N_META + 2048
T_PAD = 2176
TM = 512
TQ = 128
L_SSM = 544
SCAN_ROWS = L_SSM // 8
assert SCAN_ROWS * 8 == L_SSM
TF = 1024

LOG2_E = 1.4426950408889634
INT_MIN = -2 ** 31
NEG_INF_KEY = -2139095041


def _params(sem):
    return pltpu.CompilerParams(dimension_semantics=sem, vmem_limit_bytes=VMEM_LIMIT)


def _rms(x, g):
    return x * lax.rsqrt(jnp.mean(x * x, axis=-1, keepdims=True) + EPS) * g


def _const_spec(shape):
    zeros = (0,) * len(shape)
    return pl.BlockSpec(shape, lambda *_: zeros, pipeline_mode=pl.Buffered(1))


def _norm_proj_kernel(h_ref, g_ref, w_ref, o_ref):
    a = _rms(h_ref[...], g_ref[...]).astype(BF16)
    o_ref[...] = jnp.dot(a, w_ref[...], preferred_element_type=F32)


def _norm_proj(h, g, w):
    m, d = h.shape
    n = w.shape[1]
    return pl.pallas_call(
        _norm_proj_kernel,
        grid=(m // TM,),
        in_specs=[pl.BlockSpec((TM, d), lambda i: (i, 0)), _const_spec((1, d)), _const_spec((d, n))],
        out_specs=pl.BlockSpec((TM, n), lambda i: (i, 0)),
        out_shape=jax.ShapeDtypeStruct((m, n), F32),
        compiler_params=_params(("parallel",)),
        name="norm_proj",
    )(h, g.reshape(1, d), w)


def _rope_tables(head_dim, rows_per_batch, batch):
    r = head_dim // ROPE_FRAC
    half = r // 2
    inv = ROPE_THETA ** (-jnp.arange(half, dtype=F32) / half)
    ang = jnp.arange(rows_per_batch, dtype=F32)[:, None] * inv[None, :]
    cos, sin = jnp.cos(ang), jnp.sin(ang)
    ones = jnp.ones((rows_per_batch, head_dim - r), F32)
    zeros_h = jnp.zeros((rows_per_batch, half), F32)
    zeros_t = jnp.zeros((rows_per_batch, head_dim - r), F32)
    c = jnp.concatenate([cos, cos, ones], axis=1)
    s_up = jnp.concatenate([zeros_h, sin, zeros_t], axis=1)
    s_dn = jnp.concatenate([-sin, zeros_h, zeros_t], axis=1)
    reps = (batch, LANES // head_dim)
    return half, jnp.tile(c, reps), jnp.tile(s_up, reps), jnp.tile(s_dn, reps)


def _rope(ys, half, c, s_up, s_dn):
    return ys * c + pltpu.roll(ys, half, 1) * s_up + pltpu.roll(ys, LANES - half, 1) * s_dn


def _dsa_proj_kernel(h_ref, g_ref, w_ref, wvt_ref, c128_ref, u128_ref, d128_ref, c64_ref, u64_ref, d64_ref,
                     q_ref, k_ref, vt_ref, qi_ref, kw_ref, *, half128, half64):
    a = _rms(h_ref[...], g_ref[...]).astype(BF16)
    t128 = (half128, c128_ref[...], u128_ref[...], d128_ref[...])
    t64 = (half64, c64_ref[...], u64_ref[...], d64_ref[...])
    c0, c1, c2, c3 = Q_W, Q_W + KV_W, Q_W + 2 * KV_W, Q_W + 2 * KV_W + IQ_W

    def slabs(lo, hi):
        y = jnp.dot(a, w_ref[:, lo:hi], preferred_element_type=F32)
        return [y[:, s * LANES:(s + 1) * LANES] for s in range((hi - lo) // LANES)]

    for s, ys in enumerate(slabs(0, c0)):
        q_ref[:, s * LANES:(s + 1) * LANES] = (_rope(ys, *t128) * (HEAD_DIM ** -0.5 * LOG2_E)).astype(q_ref.dtype)
    for s, ys in enumerate(slabs(c0, c1)):
        k_ref[:, s * LANES:(s + 1) * LANES] = _rope(ys, *t128).astype(k_ref.dtype)
    vt_ref[...] = lax.dot_general(wvt_ref[...], a, (((1,), (1,)), ((), ())),
                                  preferred_element_type=F32).astype(vt_ref.dtype)
    for s, ys in enumerate(slabs(c2, c3)):
        qi_ref[s] = _rope(ys, *t64).astype(qi_ref.dtype)
    (ys,) = slabs(c3, c3 + LANES)
    lane = lax.broadcasted_iota(jnp.int32, ys.shape, 1)
    kw_ref[...] = jnp.where(lane < IDX_DIM, _rope(ys, *t64), ys)


def _dsa_proj(h, g, w, w_vt, rope128, rope64):
    m, d = h.shape
    n = w.shape[1]
    pairs = IQ_W // LANES
    row = lambda width: pl.BlockSpec((TM, width), lambda i: (i, 0))
    return pl.pallas_call(
        functools.partial(_dsa_proj_kernel, half128=rope128[0], half64=rope64[0]),
        grid=(m // TM,),
        in_specs=[row(d), _const_spec((1, d)), _const_spec((d, n)), _const_spec((KV_W, d))] + [row(LANES)] * 6,
        out_specs=[row(Q_W), row(KV_W), pl.BlockSpec((KV_W, TM), lambda i: (0, i)),
                   pl.BlockSpec((pairs, TM, LANES), lambda i: (0, i, 0)), row(LANES)],
        out_shape=[jax.ShapeDtypeStruct((m, Q_W), BF16), jax.ShapeDtypeStruct((m, KV_W), BF16),
                   jax.ShapeDtypeStruct((KV_W, m), BF16), jax.ShapeDtypeStruct((pairs, m, LANES), BF16),
                   jax.ShapeDtypeStruct((m, LANES), F32)],
        compiler_params=_params(("parallel",)),
        name="dsa_proj",
    )(h, g.reshape(1, d), w, w_vt, *rope128[1:], *rope64[1:])


def _matmul_norm_res_kernel(a_ref, w_ref, g_ref, r_ref, o_ref):
    y = jnp.dot(a_ref[...], w_ref[...], preferred_element_type=F32)
    o_ref[...] = r_ref[...] + _rms(y, g_ref[...])


def _matmul_norm_res(a, w, g, res):
    m, k = a.shape
    n = w.shape[1]
    return pl.pallas_call(
        _matmul_norm_res_kernel,
        grid=(m // TM,),
        in_specs=[pl.BlockSpec((TM, k), lambda i: (i, 0)), _const_spec((k, n)),
                  _const_spec((1, n)), pl.BlockSpec((TM, n), lambda i: (i, 0))],
        out_specs=pl.BlockSpec((TM, n), lambda i: (i, 0)),
        out_shape=jax.ShapeDtypeStruct((m, n), F32),
        compiler_params=_params(("parallel",)),
        name="matmul_norm_res",
    )(a, w, g.reshape(1, n), res)


def _mlp_kernel(h_ref, g_in_ref, g_out_ref, w1_ref, w2_ref, o_ref, a_ref, acc_ref):
    f = pl.program_id(1)

    @pl.when(f == 0)
    def _():
        a_ref[...] = _rms(h_ref[...], g_in_ref[...]).astype(BF16)
        acc_ref[...] = jnp.zeros_like(acc_ref)

    z = jnp.dot(a_ref[...], w1_ref[...], preferred_element_type=F32)
    z = jnp.square(jnp.maximum(z, 0.0)).astype(BF16)
    acc_ref[...] += jnp.dot(z, w2_ref[...], preferred_element_type=F32)

    @pl.when(f == pl.num_programs(1) - 1)
    def _():
        o_ref[...] = h_ref[...] + _rms(acc_ref[...], g_out_ref[...])


def _mlp(h, g_in, g_out, w1, w2, layer):
    m, d = h.shape
    ff = w1.shape[2]
    return pl.pallas_call(
        _mlp_kernel,
        grid=(m // TM, ff // TF),
        in_specs=[pl.BlockSpec((TM, d), lambda i, f: (i, 0)),
                  pl.BlockSpec((1, d), lambda i, f: (0, 0)), pl.BlockSpec((1, d), lambda i, f: (0, 0)),
                  pl.BlockSpec((None, d, TF), lambda i, f: (layer, 0, f)),
                  pl.BlockSpec((None, TF, d), lambda i, f: (layer, f, 0))],
        out_specs=pl.BlockSpec((TM, d), lambda i, f: (i, 0)),
        out_shape=jax.ShapeDtypeStruct((m, d), F32),
        scratch_shapes=[pltpu.VMEM((TM, d), BF16), pltpu.VMEM((TM, d), F32)],
        compiler_params=_params(("parallel", "arbitrary")),
        name="mlp",
    )(h, g_in.reshape(1, d), g_out.reshape(1, d), w1, w2)


def _cmul(ar, ai, br, bi):
    return ar * br - ai * bi, ar * bi + ai * br


def _discretise(ar, ai, log_dt):
    dt = jnp.exp(log_dt)
    mag = jnp.exp(dt * ar)
    return mag * jnp.cos(dt * ai), mag * jnp.sin(dt * ai)


def _ssm_prep_kernel(lre_ref, lim_ref, ldt_ref, lre2_ref, lim2_ref, ldt2_ref, bre_ref, bim_ref,
                     bbr_ref, bbi_ref, pwr_ref, pwi_ref):
    ar, ai = lre_ref[...], lim_ref[...]
    abr, abi = _discretise(ar, ai, ldt_ref[...])
    den = ar * ar + ai * ai
    zr, zi = abr - 1.0, abi
    fr = (zr * ar + zi * ai) / den
    fi = (zi * ar - zr * ai) / den
    br, bi = bre_ref[...], bim_ref[...]
    bbr_ref[...] = fr * br - fi * bi
    bbi_ref[...] = fr * bi + fi * br
    a1r, a1i = _discretise(lre2_ref[...], lim2_ref[...], ldt2_ref[...])
    pr, pi = a1r, a1i
    for _ in range(SCAN_ROWS - 1):
        pr, pi = _cmul(pr, pi, a1r, a1i)
    pwr_ref[0], pwi_ref[0] = a1r, a1i
    pwr_ref[1], pwi_ref[1] = pr, pi


def _ssm_prep(lam_re, lam_im, log_dt, b_re, b_im):
    g, n = lam_re.shape
    c = b_re.shape[-1]
    bt = lambda b: jnp.swapaxes(b, 1, 2)
    full = lambda *s: pl.BlockSpec(s, lambda: (0,) * len(s))
    return pl.pallas_call(
        _ssm_prep_kernel,
        in_specs=[full(g, 1, n), full(g, 1, n), full(g, 1, 1), full(g, n), full(g, n), full(g, 1),
                  full(g, c, n), full(g, c, n)],
        out_specs=[full(g, c, n), full(g, c, n), full(2, g, n), full(2, g, n)],
        out_shape=[jax.ShapeDtypeStruct((g, c, n), F32)] * 2 + [jax.ShapeDtypeStruct((2, g, n), F32)] * 2,
        name="ssm_prep",
    )(lam_re.reshape(g, 1, n), lam_im.reshape(g, 1, n), log_dt.reshape(g, 1, 1), lam_re, lam_im,
      log_dt.reshape(g, 1), bt(b_re), bt(b_im))


def _block_diag_in(bb):
    g, c, n = bb.shape
    r = bb.reshape(g // GROUPS_PER_TILE, GROUPS_PER_TILE, c, n)
    eye = jnp.eye(GROUPS_PER_TILE, dtype=bb.dtype)
    return jnp.einsum('jgcn,gh->jgchn', r, eye).reshape(g // GROUPS_PER_TILE, GROUPS_PER_TILE * c, GROUPS_PER_TILE * n)


def _block_diag_out(cc):
    g, c, n = cc.shape
    r = cc.reshape(g // GROUPS_PER_TILE, GROUPS_PER_TILE, c, n)
    eye = jnp.eye(GROUPS_PER_TILE, dtype=cc.dtype)
    return jnp.einsum('jgcn,gh->jgnhc', r, eye).reshape(g // GROUPS_PER_TILE, GROUPS_PER_TILE * n, GROUPS_PER_TILE * c)


def _row_on_sublanes(ref, j, k):
    return jnp.broadcast_to(ref[j, pl.ds(k, 1), :], (SUBLANES, ref.shape[2]))


def _mixer0_kernel(u_ref, pw_ref, ps_ref, wb_ref, wc_ref, pwr_ref, pwi_ref, d_ref, wg_ref, o_ref,
                   pbuf, uperm, ynat, sbuf, carry_ref):
    c = pl.program_id(1)
    L = u_ref.shape[0]
    sw = GROUPS_PER_TILE * SSM_STATE

    @pl.when(c == 0)
    def _():
        pbuf[0:POOL_HALO, :] = jnp.zeros((POOL_HALO, POOL_WIDTH), F32)
        carry_ref[...] = jnp.zeros_like(carry_ref)

    pbuf[POOL_HALO:POOL_HALO + L, :] = u_ref[:, 0:POOL_WIDTH]
    tpos = c * L + lax.broadcasted_iota(jnp.int32, (L, 1), 0)
    for g, win in enumerate(POOL_WINDOWS):
        sl = slice(g * POOL_GROUP, (g + 1) * POOL_GROUP)
        x = u_ref[:, sl]
        s = x
        for k in range(1, win):
            s = s + pbuf[POOL_HALO - k:POOL_HALO - k + L, sl]
        cnt = jnp.minimum(tpos + 1, win).astype(F32)
        diff = (s / cnt - x).astype(BF16)
        y = jnp.dot(diff, pw_ref[g], preferred_element_type=F32) * ps_ref[:, sl]
        o_ref[:, sl] = y.astype(o_ref.dtype)
    pbuf[0:POOL_HALO, :] = pbuf[L:L + POOL_HALO, :]

    n_tiles = SSM_WIDTH // LANES
    for j in range(n_tiles):
        ynat[j] = u_ref[:, POOL_WIDTH + j * LANES:POOL_WIDTH + (j + 1) * LANES]

    def permute(k, carry):
        for j in range(n_tiles):
            uperm[j, pl.ds(pl.multiple_of(k * SUBLANES, SUBLANES), SUBLANES), :] = \
                ynat[j, pl.ds(k, SUBLANES, stride=SCAN_ROWS), :]
        return carry

    lax.fori_loop(0, SCAN_ROWS, permute, 0)
    row = lax.broadcasted_iota(jnp.int32, (SUBLANES, sw), 0)
    for j in range(n_tiles):
        uj = uperm[j]
        sbuf[...] = jnp.dot(uj.astype(BF16), wb_ref[j], preferred_element_type=F32)
        a1r = _row_on_sublanes(pwr_ref, j, 0)
        a1i = _row_on_sublanes(pwi_ref, j, 0)

        def scan_step(k, x, store):
            xr, xi = x
            rows = pl.ds(pl.multiple_of(k * SUBLANES, SUBLANES), SUBLANES)
            xr, xi = a1r * xr - a1i * xi + sbuf[rows, 0:sw], a1r * xi + a1i * xr + sbuf[rows, sw:2 * sw]
            if store:
                sbuf[rows, 0:sw] = xr
                sbuf[rows, sw:2 * sw] = xi
            return xr, xi

        zero = jnp.zeros((SUBLANES, sw), F32)
        er, ei = lax.fori_loop(0, SCAN_ROWS, functools.partial(scan_step, store=False), (zero, zero))

        cr = _row_on_sublanes(carry_ref, j, 0)
        ci = _row_on_sublanes(carry_ref, j, 1)
        sr = jnp.where(row == 0, cr, pltpu.roll(er, 1, 0))
        si = jnp.where(row == 0, ci, pltpu.roll(ei, 1, 0))
        qr = _row_on_sublanes(pwr_ref, j, 1)
        qi = _row_on_sublanes(pwi_ref, j, 1)
        for k in (1, 2, 4):
            mr, mi = jnp.where(row >= k, qr, 0.0), jnp.where(row >= k, qi, 0.0)
            tr, ti = pltpu.roll(sr, k, 0), pltpu.roll(si, k, 0)
            sr, si = sr + mr * tr - mi * ti, si + mr * ti + mi * tr
            qr, qi = _cmul(qr, qi, qr, qi)

        lax.fori_loop(0, SCAN_ROWS, functools.partial(scan_step, store=True), (sr, si))
        carry_ref[j, 0:1, :] = sbuf[L - 1:L, 0:sw]
        carry_ref[j, 1:2, :] = sbuf[L - 1:L, sw:2 * sw]
        yj = jnp.dot(sbuf[...].astype(BF16), wc_ref[j], preferred_element_type=F32)
        uperm[j] = yj + d_ref[:, j * LANES:(j + 1) * LANES] * uj

    def unpermute(k, carry):
        for j in range(n_tiles):
            ynat[j, pl.ds(k, SUBLANES, stride=SCAN_ROWS), :] = \
                uperm[j, pl.ds(pl.multiple_of(k * SUBLANES, SUBLANES), SUBLANES), :]
        return carry

    lax.fori_loop(0, SCAN_ROWS, unpermute, 0)
    y = jax.nn.gelu(jnp.concatenate([ynat[j] for j in range(n_tiles)], axis=1))
    gate = jax.nn.sigmoid(jnp.dot(y.astype(BF16), wg_ref[...], preferred_element_type=F32))
    o_ref[:, POOL_WIDTH:] = (y * gate).astype(o_ref.dtype)


def _mixer0(u, batch, pool_w, pool_scale, wb, wc, pwr, pwi, d, w_glu):
    m, width = u.shape
    nj = SSM_WIDTH // LANES
    sw = GROUPS_PER_TILE * SSM_STATE
    chunks = T_PAD // L_SSM
    return pl.pallas_call(
        _mixer0_kernel,
        grid=(batch, chunks),
        in_specs=[pl.BlockSpec((L_SSM, width), lambda b, c: (b * chunks + c, 0)),
                  _const_spec((len(POOL_WINDOWS), POOL_GROUP, POOL_GROUP)), _const_spec((1, POOL_WIDTH)),
                  _const_spec((nj, LANES, 2 * sw)), _const_spec((nj, 2 * sw, LANES)),
                  _const_spec((nj, 2, sw)), _const_spec((nj, 2, sw)),
                  _const_spec((1, SSM_WIDTH)), _const_spec((SSM_WIDTH, SSM_WIDTH))],
        out_specs=pl.BlockSpec((L_SSM, width), lambda b, c: (b * chunks + c, 0)),
        out_shape=jax.ShapeDtypeStruct((m, width), BF16),
        scratch_shapes=[pltpu.VMEM((POOL_HALO + L_SSM, POOL_WIDTH), F32),
                        pltpu.VMEM((nj, L_SSM, LANES), F32),
                        pltpu.VMEM((nj, L_SSM, LANES), F32),
                        pltpu.VMEM((L_SSM, 2 * sw), F32),
                        pltpu.VMEM((nj, 2, sw), F32)],
        compiler_params=_params(("parallel", "arbitrary")),
        name="mixer0",
    )(u, pool_w, pool_scale, wb, wc, pwr, pwi, d, w_glu)


CHUNK_SHIFT = CHUNK.bit_length() - 1
assert 1 << CHUNK_SHIFT == CHUNK
KB = 512
KB_SHIFT = KB.bit_length() - 1
assert 1 << KB_SHIFT == KB
N_KB = -(-T_PAD // KB)
COUNT_ROWS = 64
PAIRS_PER_DOT = 2
ONES_ROWS = 16
VT_ROWS = HEAD_DIM + ONES_ROWS


def _chunk_id(t):
    return jnp.where(t < N_META, 0, ((t - N_META) >> CHUNK_SHIFT) + 1)


def _fold_rows(x, op):
    return op(x.reshape(x.shape[0] // SUBLANES, SUBLANES, x.shape[1]), axis=0)


def _stage_rows(src, dst):
    for kb in range(N_KB):
        n = min(KB, T_PAD - kb * KB)
        dst[kb, 0:n, :] = src[kb * KB:kb * KB + n, :].astype(dst.dtype)
        if n < KB:
            dst[kb, n:KB, :] = jnp.zeros((KB - n, dst.shape[2]), dst.dtype)


def _stage_values(src, dst):
    for kb in range(N_KB):
        n = min(KB, T_PAD - kb * KB)
        for kv in range(N_KV_HEADS):
            r0 = kv * VT_ROWS
            dst[kb, r0:r0 + HEAD_DIM, 0:n] = src[kv * HEAD_DIM:(kv + 1) * HEAD_DIM, kb * KB:kb * KB + n]
            if n < KB:
                dst[kb, r0:r0 + HEAD_DIM, n:KB] = jnp.zeros((HEAD_DIM, KB - n), dst.dtype)
            dst[kb, r0 + HEAD_DIM:r0 + VT_ROWS, :] = jnp.ones((ONES_ROWS, KB), dst.dtype)


def _dsa_kernel(qi_ref, wq_ref, q_ref, kw_ref, k_ref, vt_ref, o_ref,
                kie, kio, kbuf, vtbuf, key_buf, lg_buf, m_acc, o_acc):
    i = pl.program_id(1)
    tq = q_ref.shape[0]
    nt = (((1,), (1,)), ((), ()))

    @pl.when(i == 0)
    def _():
        kw = kw_ref[...]
        lane = lax.broadcasted_iota(jnp.int32, kw.shape, 1)
        _stage_rows(jnp.where(lane < IDX_DIM, kw, 0.0), kie)
        _stage_rows(jnp.where(lane >= IDX_DIM, pltpu.roll(kw, IDX_DIM, 1), 0.0), kio)
        _stage_rows(k_ref, kbuf)
        _stage_values(vt_ref, vtbuf)

    last_q = (i + 1) * tq - 1
    key_end = N_META + CHUNK * (((last_q - N_META) >> CHUNK_SHIFT) + 1)
    n_kb = jnp.minimum((key_end + KB - 1) >> KB_SHIFT, N_KB)

    w_t = wq_ref[...].T * ((IDX_HEADS ** -0.5) * (IDX_DIM ** -0.5))
    pairs = IDX_HEADS // 2
    qpos = i * tq + lax.broadcasted_iota(jnp.int32, (1, tq), 1)

    def index_block(kb, carry):
        score = jnp.zeros((KB, tq), F32)
        for p0 in range(0, pairs, PAIRS_PER_DOT):
            qg = qi_ref[p0:p0 + PAIRS_PER_DOT].reshape(PAIRS_PER_DOT * tq, LANES)
            se = lax.dot_general(kie[kb], qg, nt, preferred_element_type=F32)
            so = lax.dot_general(kio[kb], qg, nt, preferred_element_type=F32)
            for c in range(PAIRS_PER_DOT):
                p = p0 + c
                we = w_t[IDX_DIM + 2 * p:IDX_DIM + 2 * p + 1, :]
                wo = w_t[IDX_DIM + 2 * p + 1:IDX_DIM + 2 * p + 2, :]
                score = score + we * jnp.maximum(se[:, c * tq:(c + 1) * tq], 0.0)
                score = score + wo * jnp.maximum(so[:, c * tq:(c + 1) * tq], 0.0)
        kpos = kb * KB + lax.broadcasted_iota(jnp.int32, (KB, 1), 0)
        allowed = (_chunk_id(kpos) <= _chunk_id(qpos)) & (kpos < T_REAL)
        bits = pltpu.bitcast(jnp.where(allowed, score, -jnp.inf), jnp.int32)
        key_buf[kb] = jnp.where(bits < 0, bits ^ 0x7FFFFFFF, bits)
        return carry

    lax.fori_loop(0, n_kb, index_block, 0)

    def count_ge(cand):
        def block(kb, acc):
            ind = jnp.where(key_buf[kb] >= cand, 1.0, 0.0)
            return acc + jnp.sum(ind.reshape(KB // COUNT_ROWS, COUNT_ROWS, tq), axis=0)
        acc = lax.fori_loop(0, n_kb, block, jnp.zeros((COUNT_ROWS, tq), F32))
        return jnp.sum(acc, axis=0, keepdims=True)

    th = jnp.where(count_ge(0) >= TOPK, 0, INT_MIN).astype(jnp.int32)

    def search(j, th):
        cand = th | jnp.left_shift(jnp.int32(1), 30 - j)
        return jnp.where(count_ge(cand) >= TOPK, cand, th)

    th = lax.fori_loop(0, 31, search, th)
    th = jnp.maximum(th, NEG_INF_KEY + 1)

    qs = []
    for kv in range(N_KV_HEADS):
        h0 = kv * KV_GROUP
        qs.append(jnp.concatenate([q_ref[:, (h0 + g) * HEAD_DIM:(h0 + g + 1) * HEAD_DIM]
                                   for g in range(KV_GROUP)], axis=0))
    m_acc[...] = jnp.full(m_acc.shape, -jnp.inf, F32)
    o_acc[...] = jnp.zeros(o_acc.shape, F32)

    def logits_block(kb, carry):
        sel = key_buf[kb] >= th
        for kv in range(N_KV_HEADS):
            kk = kbuf[kb, :, kv * HEAD_DIM:(kv + 1) * HEAD_DIM]
            lg = lax.dot_general(kk, qs[kv], nt, preferred_element_type=F32)
            lg = jnp.concatenate([jnp.where(sel, lg[:, g * tq:(g + 1) * tq], -jnp.inf)
                                  for g in range(KV_GROUP)], axis=1)
            lg_buf[kv, kb] = lg
            m_acc[kv] = jnp.maximum(m_acc[kv], _fold_rows(lg, jnp.max))
        return carry

    lax.fori_loop(0, n_kb, logits_block, 0)
    mx = [jnp.max(m_acc[kv], axis=0, keepdims=True) for kv in range(N_KV_HEADS)]

    def value_block(kb, carry):
        for kv in range(N_KV_HEADS):
            e = jnp.exp2((lg_buf[kv, kb] - mx[kv]).astype(BF16))
            vt = vtbuf[kb, kv * VT_ROWS:(kv + 1) * VT_ROWS, :]
            o_acc[kv] += jnp.dot(vt, e, preferred_element_type=F32)
        return carry

    lax.fori_loop(0, n_kb, value_block, 0)
    for kv in range(N_KV_HEADS):
        out_t = o_acc[kv, 0:HEAD_DIM, :] / o_acc[kv, HEAD_DIM:HEAD_DIM + 1, :]
        for g in range(KV_GROUP):
            hd = kv * KV_GROUP + g
            o_ref[:, hd * HEAD_DIM:(hd + 1) * HEAD_DIM] = out_t[:, g * tq:(g + 1) * tq].T.astype(o_ref.dtype)


def _dsa(qi, kw, q, k, vt, batch):
    m = q.shape[0]
    nq = T_PAD // TQ
    pairs = qi.shape[0]
    cols = KV_GROUP * TQ
    return pl.pallas_call(
        _dsa_kernel,
        grid=(batch, nq),
        in_specs=[pl.BlockSpec((pairs, TQ, LANES), lambda b, i: (0, b * nq + i, 0)),
                  pl.BlockSpec((TQ, LANES), lambda b, i: (b * nq + i, 0)),
                  pl.BlockSpec((TQ, Q_W), lambda b, i: (b * nq + i, 0)),
                  pl.BlockSpec((T_PAD, LANES), lambda b, i: (b, 0)),
                  pl.BlockSpec((T_PAD, KV_W), lambda b, i: (b, 0)),
                  pl.BlockSpec((KV_W, T_PAD), lambda b, i: (0, b))],
        out_specs=pl.BlockSpec((TQ, Q_W), lambda b, i: (b * nq + i, 0)),
        out_shape=jax.ShapeDtypeStruct((m, Q_W), BF16),
        scratch_shapes=[pltpu.VMEM((N_KB, KB, LANES), BF16), pltpu.VMEM((N_KB, KB, LANES), BF16),
                        pltpu.VMEM((N_KB, KB, KV_W), BF16), pltpu.VMEM((N_KB, N_KV_HEADS * VT_ROWS, KB), BF16),
                        pltpu.VMEM((N_KB, KB, TQ), jnp.int32),
                        pltpu.VMEM((N_KV_HEADS, N_KB, KB, cols), F32),
                        pltpu.VMEM((N_KV_HEADS, SUBLANES, cols), F32),
                        pltpu.VMEM((N_KV_HEADS, VT_ROWS, cols), F32)],
        compiler_params=_params(("parallel", "arbitrary")),
        name="dsa",
    )(qi, kw, q, kw, k, vt)


def kernel(x, meta, norm_g, ab_w_in, pool_w, pool_scale, s5_lambda_re, s5_lambda_im, s5_log_dt,
           s5_b_re, s5_b_im, s5_c_re, s5_c_im, s5_d, s5_w_glu, ab_w_out, c_w_in, c_w_out, mlp_w1, mlp_w2):
    batch, seq, d = x.shape
    assert (seq + N_META, d) == (T_REAL, D_MODEL)
    depth = norm_g.shape[0]
    meta_b = jnp.broadcast_to(meta[None].astype(x.dtype), (batch, N_META, d))
    pad = jnp.zeros((batch, T_PAD - T_REAL, d), x.dtype)
    h = jnp.concatenate([meta_b, x, pad], axis=1).reshape(batch * T_PAD, d)

    rope128 = _rope_tables(HEAD_DIM, T_PAD, batch)
    rope64 = _rope_tables(IDX_DIM, T_PAD, batch)
    w1, w2 = mlp_w1.astype(BF16), mlp_w2.astype(BF16)

    for layer in range(depth):
        g = norm_g[layer]
        if layer % 2 == 0:
            e = layer // 2
            u = _norm_proj(h, g[0], ab_w_in[e].astype(BF16))
            bbr, bbi, pwr, pwi = _ssm_prep(s5_lambda_re[e], s5_lambda_im[e], s5_log_dt[e], s5_b_re[e], s5_b_im[e])
            nj = SSM_GROUPS // GROUPS_PER_TILE
            tile_major = lambda p: p.reshape(2, nj, GROUPS_PER_TILE * SSM_STATE).swapaxes(0, 1)
            wb = jnp.concatenate([_block_diag_in(bbr), _block_diag_in(bbi)], axis=-1).astype(BF16)
            wc = jnp.concatenate([_block_diag_out(s5_c_re[e]), -_block_diag_out(s5_c_im[e])], axis=1).astype(BF16)
            mix = _mixer0(u, batch, pool_w[e].astype(BF16), pool_scale[e].reshape(1, POOL_WIDTH), wb, wc,
                          tile_major(pwr), tile_major(pwi),
                          s5_d[e].reshape(1, SSM_WIDTH), s5_w_glu[e].astype(BF16))
            w_out = ab_w_out[e]
        else:
            o = layer // 2
            n_in = c_w_in.shape[-1]
            w_in = jnp.pad(c_w_in[o].astype(BF16), ((0, 0), (0, -n_in % LANES)))
            w_vt = w_in[:, Q_W + KV_W:Q_W + 2 * KV_W].T
            q, k, vt, qi, kw = _dsa_proj(h, g[0], w_in, w_vt, rope128, rope64)
            mix = _dsa(qi, kw, q, k, vt, batch)
            w_out = c_w_out[o]
        h = _matmul_norm_res(mix, w_out.astype(BF16), g[1], h)
        h = _mlp(h, g[2], g[3], w1, w2, layer)
    return h.reshape(batch, T_PAD, d)[:, N_META:T_REAL]
```

```python
import functools

import jax
import jax.numpy as jnp
from jax import lax
from jax.experimental import pallas as pl
from jax.experimental.pallas import tpu as pltpu

F32 = jnp.float32
BF16 = jnp.bfloat16

D_MODEL = 2048
CHUNK = 64
N_META = 16
POOL_WIDTH = 1024
POOL_WINDOWS = (2, 4, 8, 16)
POOL_GROUP = 256
POOL_HALO = 16
SSM_WIDTH = 1024
SSM_GROUP_CH = 16
SSM_GROUPS = 64
SSM_STATE = 64
GROUPS_PER_TILE = 8
HEAD_DIM = 128
N_HEADS = 16
N_KV_HEADS = 4
KV_GROUP = 4
IDX_HEADS = 16
IDX_DIM = 64
ROPE_THETA = 500000.0
ROPE_FRAC = 4
TOPK = 256
EPS = 1e-6
Q_W = N_HEADS * HEAD_DIM
KV_W = N_KV_HEADS * HEAD_DIM
IQ_W = IDX_HEADS * IDX_DIM

LANES = 128
SUBLANES = 8
VMEM_LIMIT = 56 * 1024 * 1024

T_REAL = N_META + 2048
T_PAD = 2176
TM = 512
TQ = 128
L_SSM = 544
SCAN_ROWS = L_SSM // 8
assert SCAN_ROWS * 8 == L_SSM
MLP_TM = 1088
MLP_TF = 512

LOG2_E = 1.4426950408889634
INT_MIN = -2 ** 31
NEG_INF_KEY = -2139095041


def _params(sem):
    return pltpu.CompilerParams(dimension_semantics=sem, vmem_limit_bytes=VMEM_LIMIT)


def _rms(x, g):
    return x * lax.rsqrt(jnp.mean(x * x, axis=-1, keepdims=True) + EPS) * g


def _const_spec(shape):
    zeros = (0,) * len(shape)
    return pl.BlockSpec(shape, lambda *_: zeros, pipeline_mode=pl.Buffered(1))


def _norm_proj_kernel(h_ref, g_ref, w_ref, o_ref):
    a = _rms(h_ref[...], g_ref[...]).astype(BF16)
    o_ref[...] = jnp.dot(a, w_ref[...], preferred_element_type=F32)


def _norm_proj(h, g, w):
    m, d = h.shape
    n = w.shape[1]
    return pl.pallas_call(
        _norm_proj_kernel,
        grid=(m // TM,),
        in_specs=[pl.BlockSpec((TM, d), lambda i: (i, 0)), _const_spec((1, d)), _const_spec((d, n))],
        out_specs=pl.BlockSpec((TM, n), lambda i: (i, 0)),
        out_shape=jax.ShapeDtypeStruct((m, n), F32),
        compiler_params=_params(("parallel",)),
        name="norm_proj",
    )(h, g.reshape(1, d), w)


def _rope_tables(head_dim, rows_per_batch, batch):
    r = head_dim // ROPE_FRAC
    half = r // 2
    inv = ROPE_THETA ** (-jnp.arange(half, dtype=F32) / half)
    ang = jnp.arange(rows_per_batch, dtype=F32)[:, None] * inv[None, :]
    cos, sin = jnp.cos(ang), jnp.sin(ang)
    ones = jnp.ones((rows_per_batch, head_dim - r), F32)
    zeros_h = jnp.zeros((rows_per_batch, half), F32)
    zeros_t = jnp.zeros((rows_per_batch, head_dim - r), F32)
    c = jnp.concatenate([cos, cos, ones], axis=1)
    s_up = jnp.concatenate([zeros_h, sin, zeros_t], axis=1)
    s_dn = jnp.concatenate([-sin, zeros_h, zeros_t], axis=1)
    reps = (batch, LANES // head_dim)
    return half, jnp.tile(c, reps), jnp.tile(s_up, reps), jnp.tile(s_dn, reps)


def _rope(ys, half, c, s_up, s_dn):
    return ys * c + pltpu.roll(ys, half, 1) * s_up + pltpu.roll(ys, LANES - half, 1) * s_dn


def _dsa_proj_kernel(h_ref, g_ref, w_ref, wvt_ref, c128_ref, u128_ref, d128_ref, c64_ref, u64_ref, d64_ref,
                     q_ref, k_ref, vt_ref, qi_ref, kw_ref, *, half128, half64):
    a = _rms(h_ref[...], g_ref[...]).astype(BF16)
    t128 = (half128, c128_ref[...], u128_ref[...], d128_ref[...])
    t64 = (half64, c64_ref[...], u64_ref[...], d64_ref[...])
    c0, c1, c2, c3 = Q_W, Q_W + KV_W, Q_W + 2 * KV_W, Q_W + 2 * KV_W + IQ_W

    def slabs(lo, hi):
        y = jnp.dot(a, w_ref[:, lo:hi], preferred_element_type=F32)
        return [y[:, s * LANES:(s + 1) * LANES] for s in range((hi - lo) // LANES)]

    for s, ys in enumerate(slabs(0, c0)):
        q_ref[:, s * LANES:(s + 1) * LANES] = (_rope(ys, *t128) * (HEAD_DIM ** -0.5 * LOG2_E)).astype(q_ref.dtype)
    for s, ys in enumerate(slabs(c0, c1)):
        k_ref[:, s * LANES:(s + 1) * LANES] = _rope(ys, *t128).astype(k_ref.dtype)
    vt_ref[...] = lax.dot_general(wvt_ref[...], a, (((1,), (1,)), ((), ())),
                                  preferred_element_type=F32).astype(vt_ref.dtype)
    for s, ys in enumerate(slabs(c2, c3)):
        qi_ref[s] = _rope(ys, *t64).astype(qi_ref.dtype)
    (ys,) = slabs(c3, c3 + LANES)
    lane = lax.broadcasted_iota(jnp.int32, ys.shape, 1)
    kw_ref[...] = jnp.where(lane < IDX_DIM, _rope(ys, *t64), ys)


def _dsa_proj(h, g, w, w_vt, rope128, rope64):
    m, d = h.shape
    n = w.shape[1]
    pairs = IQ_W // LANES
    row = lambda width: pl.BlockSpec((TM, width), lambda i: (i, 0))
    return pl.pallas_call(
        functools.partial(_dsa_proj_kernel, half128=rope128[0], half64=rope64[0]),
        grid=(m // TM,),
        in_specs=[row(d), _const_spec((1, d)), _const_spec((d, n)), _const_spec((KV_W, d))] + [row(LANES)] * 6,
        out_specs=[row(Q_W), row(KV_W), pl.BlockSpec((KV_W, TM), lambda i: (0, i)),
                   pl.BlockSpec((pairs, TM, LANES), lambda i: (0, i, 0)), row(LANES)],
        out_shape=[jax.ShapeDtypeStruct((m, Q_W), BF16), jax.ShapeDtypeStruct((m, KV_W), BF16),
                   jax.ShapeDtypeStruct((KV_W, m), BF16), jax.ShapeDtypeStruct((pairs, m, LANES), BF16),
                   jax.ShapeDtypeStruct((m, LANES), F32)],
        compiler_params=_params(("parallel",)),
        name="dsa_proj",
    )(h, g.reshape(1, d), w, w_vt, *rope128[1:], *rope64[1:])


def _matmul_norm_res_kernel(a_ref, w_ref, g_ref, r_ref, o_ref):
    y = jnp.dot(a_ref[...], w_ref[...], preferred_element_type=F32)
    o_ref[...] = r_ref[...] + _rms(y, g_ref[...])


def _matmul_norm_res(a, w, g, res):
    m, k = a.shape
    n = w.shape[1]
    return pl.pallas_call(
        _matmul_norm_res_kernel,
        grid=(m // TM,),
        in_specs=[pl.BlockSpec((TM, k), lambda i: (i, 0)), _const_spec((k, n)),
                  _const_spec((1, n)), pl.BlockSpec((TM, n), lambda i: (i, 0))],
        out_specs=pl.BlockSpec((TM, n), lambda i: (i, 0)),
        out_shape=jax.ShapeDtypeStruct((m, n), F32),
        compiler_params=_params(("parallel",)),
        name="matmul_norm_res",
    )(a, w, g.reshape(1, n), res)


def _mlp_kernel(h_ref, g_in_ref, g_out_ref, w1_ref, w2_ref, o_ref, a_ref):
    f = pl.program_id(1)

    @pl.when(f == 0)
    def _():
        a_ref[...] = _rms(h_ref[...], g_in_ref[...]).astype(BF16)
        o_ref[...] = jnp.zeros_like(o_ref)

    z = jnp.dot(a_ref[...], w1_ref[...].astype(BF16), preferred_element_type=F32)
    z = jnp.square(jnp.maximum(z, 0.0)).astype(BF16)
    o_ref[...] += jnp.dot(z, w2_ref[...].astype(BF16), preferred_element_type=F32)

    @pl.when(f == pl.num_programs(1) - 1)
    def _():
        o_ref[...] = h_ref[...] + _rms(o_ref[...], g_out_ref[...])


def _mlp(h, g_in, g_out, w1, w2, layer):
    m, d = h.shape
    ff = w1.shape[2]
    return pl.pallas_call(
        _mlp_kernel,
        grid=(m // MLP_TM, ff // MLP_TF),
        in_specs=[pl.BlockSpec((MLP_TM, d), lambda i, f: (i, 0), pipeline_mode=pl.Buffered(1)),
                  _const_spec((1, d)), _const_spec((1, d)),
                  pl.BlockSpec((None, d, MLP_TF), lambda i, f: (layer, 0, f)),
                  pl.BlockSpec((None, MLP_TF, d), lambda i, f: (layer, f, 0))],
        out_specs=pl.BlockSpec((MLP_TM, d), lambda i, f: (i, 0), pipeline_mode=pl.Buffered(1)),
        out_shape=jax.ShapeDtypeStruct((m, d), F32),
        scratch_shapes=[pltpu.VMEM((MLP_TM, d), BF16)],
        compiler_params=_params(("parallel", "arbitrary")),
        name="mlp",
    )(h, g_in.reshape(1, d), g_out.reshape(1, d), w1, w2)


def _cmul(ar, ai, br, bi):
    return ar * br - ai * bi, ar * bi + ai * br


def _discretise(ar, ai, log_dt):
    dt = jnp.exp(log_dt)
    mag = jnp.exp(dt * ar)
    return mag * jnp.cos(dt * ai), mag * jnp.sin(dt * ai)


def _ssm_prep_kernel(lre_ref, lim_ref, ldt_ref, lre2_ref, lim2_ref, ldt2_ref, bre_ref, bim_ref,
                     bbr_ref, bbi_ref, pwr_ref, pwi_ref):
    ar, ai = lre_ref[...], lim_ref[...]
    abr, abi = _discretise(ar, ai, ldt_ref[...])
    den = ar * ar + ai * ai
    zr, zi = abr - 1.0, abi
    fr = (zr * ar + zi * ai) / den
    fi = (zi * ar - zr * ai) / den
    br, bi = bre_ref[...], bim_ref[...]
    bbr_ref[...] = fr * br - fi * bi
    bbi_ref[...] = fr * bi + fi * br
    a1r, a1i = _discretise(lre2_ref[...], lim2_ref[...], ldt2_ref[...])
    pr, pi = a1r, a1i
    for _ in range(SCAN_ROWS - 1):
        pr, pi = _cmul(pr, pi, a1r, a1i)
    pwr_ref[0], pwi_ref[0] = a1r, a1i
    pwr_ref[1], pwi_ref[1] = pr, pi


def _ssm_prep(lam_re, lam_im, log_dt, b_re, b_im):
    g, n = lam_re.shape
    c = b_re.shape[-1]
    bt = lambda b: jnp.swapaxes(b, 1, 2)
    full = lambda *s: pl.BlockSpec(s, lambda: (0,) * len(s))
    return pl.pallas_call(
        _ssm_prep_kernel,
        in_specs=[full(g, 1, n), full(g, 1, n), full(g, 1, 1), full(g, n), full(g, n), full(g, 1),
                  full(g, c, n), full(g, c, n)],
        out_specs=[full(g, c, n), full(g, c, n), full(2, g, n), full(2, g, n)],
        out_shape=[jax.ShapeDtypeStruct((g, c, n), F32)] * 2 + [jax.ShapeDtypeStruct((2, g, n), F32)] * 2,
        name="ssm_prep",
    )(lam_re.reshape(g, 1, n), lam_im.reshape(g, 1, n), log_dt.reshape(g, 1, 1), lam_re, lam_im,
      log_dt.reshape(g, 1), bt(b_re), bt(b_im))


def _block_diag_in(bb):
    g, c, n = bb.shape
    r = bb.reshape(g // GROUPS_PER_TILE, GROUPS_PER_TILE, c, n)
    eye = jnp.eye(GROUPS_PER_TILE, dtype=bb.dtype)
    return jnp.einsum('jgcn,gh->jgchn', r, eye).reshape(g // GROUPS_PER_TILE, GROUPS_PER_TILE * c, GROUPS_PER_TILE * n)


def _block_diag_out(cc):
    g, c, n = cc.shape
    r = cc.reshape(g // GROUPS_PER_TILE, GROUPS_PER_TILE, c, n)
    eye = jnp.eye(GROUPS_PER_TILE, dtype=cc.dtype)
    return jnp.einsum('jgcn,gh->jgnhc', r, eye).reshape(g // GROUPS_PER_TILE, GROUPS_PER_TILE * n, GROUPS_PER_TILE * c)


def _row_on_sublanes(ref, j, k):
    return jnp.broadcast_to(ref[j, pl.ds(k, 1), :], (SUBLANES, ref.shape[2]))


def _mixer0_kernel(u_ref, pw_ref, ps_ref, wb_ref, wc_ref, pwr_ref, pwi_ref, d_ref, wg_ref, o_ref,
                   pbuf, uperm, ynat, sbuf, carry_ref):
    c = pl.program_id(1)
    L = u_ref.shape[0]
    sw = GROUPS_PER_TILE * SSM_STATE

    @pl.when(c == 0)
    def _():
        pbuf[0:POOL_HALO, :] = jnp.zeros((POOL_HALO, POOL_WIDTH), F32)
        carry_ref[...] = jnp.zeros_like(carry_ref)

    pbuf[POOL_HALO:POOL_HALO + L, :] = u_ref[:, 0:POOL_WIDTH]
    tpos = c * L + lax.broadcasted_iota(jnp.int32, (L, 1), 0)
    for g, win in enumerate(POOL_WINDOWS):
        sl = slice(g * POOL_GROUP, (g + 1) * POOL_GROUP)
        x = u_ref[:, sl]
        s = x
        for k in range(1, win):
            s = s + pbuf[POOL_HALO - k:POOL_HALO - k + L, sl]
        cnt = jnp.minimum(tpos + 1, win).astype(F32)
        diff = (s / cnt - x).astype(BF16)
        y = jnp.dot(diff, pw_ref[g], preferred_element_type=F32) * ps_ref[:, sl]
        o_ref[:, sl] = y.astype(o_ref.dtype)
    pbuf[0:POOL_HALO, :] = pbuf[L:L + POOL_HALO, :]

    n_tiles = SSM_WIDTH // LANES
    for j in range(n_tiles):
        ynat[j] = u_ref[:, POOL_WIDTH + j * LANES:POOL_WIDTH + (j + 1) * LANES]

    def permute(k, carry):
        for j in range(n_tiles):
            uperm[j, pl.ds(pl.multiple_of(k * SUBLANES, SUBLANES), SUBLANES), :] = \
                ynat[j, pl.ds(k, SUBLANES, stride=SCAN_ROWS), :]
        return carry

    lax.fori_loop(0, SCAN_ROWS, permute, 0)
    row = lax.broadcasted_iota(jnp.int32, (SUBLANES, sw), 0)
    for j in range(n_tiles):
        uj = uperm[j]
        sbuf[...] = jnp.dot(uj.astype(BF16), wb_ref[j], preferred_element_type=F32)
        a1r = _row_on_sublanes(pwr_ref, j, 0)
        a1i = _row_on_sublanes(pwi_ref, j, 0)

        def scan_step(k, x, store):
            xr, xi = x
            rows = pl.ds(pl.multiple_of(k * SUBLANES, SUBLANES), SUBLANES)
            xr, xi = a1r * xr - a1i * xi + sbuf[rows, 0:sw], a1r * xi + a1i * xr + sbuf[rows, sw:2 * sw]
            if store:
                sbuf[rows, 0:sw] = xr
                sbuf[rows, sw:2 * sw] = xi
            return xr, xi

        zero = jnp.zeros((SUBLANES, sw), F32)
        er, ei = lax.fori_loop(0, SCAN_ROWS, functools.partial(scan_step, store=False), (zero, zero))

        cr = _row_on_sublanes(carry_ref, j, 0)
        ci = _row_on_sublanes(carry_ref, j, 1)
        sr = jnp.where(row == 0, cr, pltpu.roll(er, 1, 0))
        si = jnp.where(row == 0, ci, pltpu.roll(ei, 1, 0))
        qr = _row_on_sublanes(pwr_ref, j, 1)
        qi = _row_on_sublanes(pwi_ref, j, 1)
        for k in (1, 2, 4):
            mr, mi = jnp.where(row >= k, qr, 0.0), jnp.where(row >= k, qi, 0.0)
            tr, ti = pltpu.roll(sr, k, 0), pltpu.roll(si, k, 0)
            sr, si = sr + mr * tr - mi * ti, si + mr * ti + mi * tr
            qr, qi = _cmul(qr, qi, qr, qi)

        lax.fori_loop(0, SCAN_ROWS, functools.partial(scan_step, store=True), (sr, si))
        carry_ref[j, 0:1, :] = sbuf[L - 1:L, 0:sw]
        carry_ref[j, 1:2, :] = sbuf[L - 1:L, sw:2 * sw]
        yj = jnp.dot(sbuf[...].astype(BF16), wc_ref[j], preferred_element_type=F32)
        uperm[j] = yj + d_ref[:, j * LANES:(j + 1) * LANES] * uj

    def unpermute(k, carry):
        for j in range(n_tiles):
            ynat[j, pl.ds(k, SUBLANES, stride=SCAN_ROWS), :] = \
                uperm[j, pl.ds(pl.multiple_of(k * SUBLANES, SUBLANES), SUBLANES), :]
        return carry

    lax.fori_loop(0, SCAN_ROWS, unpermute, 0)
    y = jax.nn.gelu(jnp.concatenate([ynat[j] for j in range(n_tiles)], axis=1))
    gate = jax.nn.sigmoid(jnp.dot(y.astype(BF16), wg_ref[...], preferred_element_type=F32))
    o_ref[:, POOL_WIDTH:] = (y * gate).astype(o_ref.dtype)


def _mixer0(u, batch, pool_w, pool_scale, wb, wc, pwr, pwi, d, w_glu):
    m, width = u.shape
    nj = SSM_WIDTH // LANES
    sw = GROUPS_PER_TILE * SSM_STATE
    chunks = T_PAD // L_SSM
    return pl.pallas_call(
        _mixer0_kernel,
        grid=(batch, chunks),
        in_specs=[pl.BlockSpec((L_SSM, width), lambda b, c: (b * chunks + c, 0)),
                  _const_spec((len(POOL_WINDOWS), POOL_GROUP, POOL_GROUP)), _const_spec((1, POOL_WIDTH)),
                  _const_spec((nj, LANES, 2 * sw)), _const_spec((nj, 2 * sw, LANES)),
                  _const_spec((nj, 2, sw)), _const_spec((nj, 2, sw)),
                  _const_spec((1, SSM_WIDTH)), _const_spec((SSM_WIDTH, SSM_WIDTH))],
        out_specs=pl.BlockSpec((L_SSM, width), lambda b, c: (b * chunks + c, 0)),
        out_shape=jax.ShapeDtypeStruct((m, width), BF16),
        scratch_shapes=[pltpu.VMEM((POOL_HALO + L_SSM, POOL_WIDTH), F32),
                        pltpu.VMEM((nj, L_SSM, LANES), F32),
                        pltpu.VMEM((nj, L_SSM, LANES), F32),
                        pltpu.VMEM((L_SSM, 2 * sw), F32),
                        pltpu.VMEM((nj, 2, sw), F32)],
        compiler_params=_params(("parallel", "arbitrary")),
        name="mixer0",
    )(u, pool_w, pool_scale, wb, wc, pwr, pwi, d, w_glu)


CHUNK_SHIFT = CHUNK.bit_length() - 1
assert 1 << CHUNK_SHIFT == CHUNK
KB = 512
KB_SHIFT = KB.bit_length() - 1
assert 1 << KB_SHIFT == KB
N_KB = -(-T_PAD // KB)
COUNT_ROWS = 64
PAIRS_PER_DOT = 2
ONES_ROWS = 16
VT_ROWS = HEAD_DIM + ONES_ROWS


def _chunk_id(t):
    return jnp.where(t < N_META, 0, ((t - N_META) >> CHUNK_SHIFT) + 1)


def _fold_rows(x, op):
    return op(x.reshape(x.shape[0] // SUBLANES, SUBLANES, x.shape[1]), axis=0)


def _stage_rows(src, dst):
    for kb in range(N_KB):
        n = min(KB, T_PAD - kb * KB)
        dst[kb, 0:n, :] = src[kb * KB:kb * KB + n, :].astype(dst.dtype)
        if n < KB:
            dst[kb, n:KB, :] = jnp.zeros((KB - n, dst.shape[2]), dst.dtype)


def _stage_values(src, dst):
    for kb in range(N_KB):
        n = min(KB, T_PAD - kb * KB)
        for kv in range(N_KV_HEADS):
            r0 = kv * VT_ROWS
            dst[kb, r0:r0 + HEAD_DIM, 0:n] = src[kv * HEAD_DIM:(kv + 1) * HEAD_DIM, kb * KB:kb * KB + n]
            if n < KB:
                dst[kb, r0:r0 + HEAD_DIM, n:KB] = jnp.zeros((HEAD_DIM, KB - n), dst.dtype)
            dst[kb, r0 + HEAD_DIM:r0 + VT_ROWS, :] = jnp.ones((ONES_ROWS, KB), dst.dtype)


def _dsa_kernel(qi_ref, wq_ref, q_ref, kw_ref, k_ref, vt_ref, o_ref,
                kie, kio, kbuf, vtbuf, key_buf, lg_buf, m_acc, o_acc):
    i = pl.program_id(1)
    tq = q_ref.shape[0]
    nt = (((1,), (1,)), ((), ()))

    @pl.when(i == 0)
    def _():
        kw = kw_ref[...]
        lane = lax.broadcasted_iota(jnp.int32, kw.shape, 1)
        _stage_rows(jnp.where(lane < IDX_DIM, kw, 0.0), kie)
        _stage_rows(jnp.where(lane >= IDX_DIM, pltpu.roll(kw, IDX_DIM, 1), 0.0), kio)
        _stage_rows(k_ref, kbuf)
        _stage_values(vt_ref, vtbuf)

    last_q = (i + 1) * tq - 1
    key_end = N_META + CHUNK * (((last_q - N_META) >> CHUNK_SHIFT) + 1)
    n_kb = jnp.minimum((key_end + KB - 1) >> KB_SHIFT, N_KB)

    w_t = wq_ref[...].T * ((IDX_HEADS ** -0.5) * (IDX_DIM ** -0.5))
    pairs = IDX_HEADS // 2
    qpos = i * tq + lax.broadcasted_iota(jnp.int32, (1, tq), 1)

    def index_block(kb, carry):
        score = jnp.zeros((KB, tq), F32)
        for p0 in range(0, pairs, PAIRS_PER_DOT):
            qg = qi_ref[p0:p0 + PAIRS_PER_DOT].reshape(PAIRS_PER_DOT * tq, LANES)
            se = lax.dot_general(kie[kb], qg, nt, preferred_element_type=F32)
            so = lax.dot_general(kio[kb], qg, nt, preferred_element_type=F32)
            for c in range(PAIRS_PER_DOT):
                p = p0 + c
                we = w_t[IDX_DIM + 2 * p:IDX_DIM + 2 * p + 1, :]
                wo = w_t[IDX_DIM + 2 * p + 1:IDX_DIM + 2 * p + 2, :]
                score = score + we * jnp.maximum(se[:, c * tq:(c + 1) * tq], 0.0)
                score = score + wo * jnp.maximum(so[:, c * tq:(c + 1) * tq], 0.0)
        kpos = kb * KB + lax.broadcasted_iota(jnp.int32, (KB, 1), 0)
        allowed = (_chunk_id(kpos) <= _chunk_id(qpos)) & (kpos < T_REAL)
        bits = pltpu.bitcast(jnp.where(allowed, score, -jnp.inf), jnp.int32)
        key_buf[kb] = jnp.where(bits < 0, bits ^ 0x7FFFFFFF, bits)
        return carry

    lax.fori_loop(0, n_kb, index_block, 0)

    def count_ge(cand):
        def block(kb, acc):
            ind = jnp.where(key_buf[kb] >= cand, 1.0, 0.0)
            return acc + jnp.sum(ind.reshape(KB // COUNT_ROWS, COUNT_ROWS, tq), axis=0)
        acc = lax.fori_loop(0, n_kb, block, jnp.zeros((COUNT_ROWS, tq), F32))
        return jnp.sum(acc, axis=0, keepdims=True)

    th = jnp.where(count_ge(0) >= TOPK, 0, INT_MIN).astype(jnp.int32)

    def search(j, th):
        cand = th | jnp.left_shift(jnp.int32(1), 30 - j)
        return jnp.where(count_ge(cand) >= TOPK, cand, th)

    th = lax.fori_loop(0, 31, search, th)
    th = jnp.maximum(th, NEG_INF_KEY + 1)

    qs = []
    for kv in range(N_KV_HEADS):
        h0 = kv * KV_GROUP
        qs.append(jnp.concatenate([q_ref[:, (h0 + g) * HEAD_DIM:(h0 + g + 1) * HEAD_DIM]
                                   for g in range(KV_GROUP)], axis=0))
    m_acc[...] = jnp.full(m_acc.shape, -jnp.inf, F32)
    o_acc[...] = jnp.zeros(o_acc.shape, F32)

    def logits_block(kb, carry):
        sel = key_buf[kb] >= th
        for kv in range(N_KV_HEADS):
            kk = kbuf[kb, :, kv * HEAD_DIM:(kv + 1) * HEAD_DIM]
            lg = lax.dot_general(kk, qs[kv], nt, preferred_element_type=F32)
            lg = jnp.concatenate([jnp.where(sel, lg[:, g * tq:(g + 1) * tq], -jnp.inf)
                                  for g in range(KV_GROUP)], axis=1)
            lg_buf[kv, kb] = lg
            m_acc[kv] = jnp.maximum(m_acc[kv], _fold_rows(lg, jnp.max))
        return carry

    lax.fori_loop(0, n_kb, logits_block, 0)
    mx = [jnp.max(m_acc[kv], axis=0, keepdims=True) for kv in range(N_KV_HEADS)]

    def value_block(kb, carry):
        for kv in range(N_KV_HEADS):
            e = jnp.exp2((lg_buf[kv, kb] - mx[kv]).astype(BF16))
            vt = vtbuf[kb, kv * VT_ROWS:(kv + 1) * VT_ROWS, :]
            o_acc[kv] += jnp.dot(vt, e, preferred_element_type=F32)
        return carry

    lax.fori_loop(0, n_kb, value_block, 0)
    for kv in range(N_KV_HEADS):
        out_t = o_acc[kv, 0:HEAD_DIM, :] / o_acc[kv, HEAD_DIM:HEAD_DIM + 1, :]
        for g in range(KV_GROUP):
            hd = kv * KV_GROUP + g
            o_ref[:, hd * HEAD_DIM:(hd + 1) * HEAD_DIM] = out_t[:, g * tq:(g + 1) * tq].T.astype(o_ref.dtype)


def _dsa(qi, kw, q, k, vt, batch):
    m = q.shape[0]
    nq = T_PAD // TQ
    pairs = qi.shape[0]
    cols = KV_GROUP * TQ
    return pl.pallas_call(
        _dsa_kernel,
        grid=(batch, nq),
        in_specs=[pl.BlockSpec((pairs, TQ, LANES), lambda b, i: (0, b * nq + i, 0)),
                  pl.BlockSpec((TQ, LANES), lambda b, i: (b * nq + i, 0)),
                  pl.BlockSpec((TQ, Q_W), lambda b, i: (b * nq + i, 0)),
                  pl.BlockSpec((T_PAD, LANES), lambda b, i: (b, 0)),
                  pl.BlockSpec((T_PAD, KV_W), lambda b, i: (b, 0)),
                  pl.BlockSpec((KV_W, T_PAD), lambda b, i: (0, b))],
        out_specs=pl.BlockSpec((TQ, Q_W), lambda b, i: (b * nq + i, 0)),
        out_shape=jax.ShapeDtypeStruct((m, Q_W), BF16),
        scratch_shapes=[pltpu.VMEM((N_KB, KB, LANES), BF16), pltpu.VMEM((N_KB, KB, LANES), BF16),
                        pltpu.VMEM((N_KB, KB, KV_W), BF16), pltpu.VMEM((N_KB, N_KV_HEADS * VT_ROWS, KB), BF16),
                        pltpu.VMEM((N_KB, KB, TQ), jnp.int32),
                        pltpu.VMEM((N_KV_HEADS, N_KB, KB, cols), F32),
                        pltpu.VMEM((N_KV_HEADS, SUBLANES, cols), F32),
                        pltpu.VMEM((N_KV_HEADS, VT_ROWS, cols), F32)],
        compiler_params=_params(("parallel", "arbitrary")),
        name="dsa",
    )(qi, kw, q, kw, k, vt)


def kernel(x, meta, norm_g, ab_w_in, pool_w, pool_scale, s5_lambda_re, s5_lambda_im, s5_log_dt,
           s5_b_re, s5_b_im, s5_c_re, s5_c_im, s5_d, s5_w_glu, ab_w_out, c_w_in, c_w_out, mlp_w1, mlp_w2):
    batch, seq, d = x.shape
    assert (seq + N_META, d) == (T_REAL, D_MODEL)
    depth = norm_g.shape[0]
    meta_b = jnp.broadcast_to(meta[None].astype(x.dtype), (batch, N_META, d))
    pad = jnp.zeros((batch, T_PAD - T_REAL, d), x.dtype)
    h = jnp.concatenate([meta_b, x, pad], axis=1).reshape(batch * T_PAD, d)

    rope128 = _rope_tables(HEAD_DIM, T_PAD, batch)
    rope64 = _rope_tables(IDX_DIM, T_PAD, batch)

    for layer in range(depth):
        g = norm_g[layer]
        if layer % 2 == 0:
            e = layer // 2
            u = _norm_proj(h, g[0], ab_w_in[e].astype(BF16))
            bbr, bbi, pwr, pwi = _ssm_prep(s5_lambda_re[e], s5_lambda_im[e], s5_log_dt[e], s5_b_re[e], s5_b_im[e])
            nj = SSM_GROUPS // GROUPS_PER_TILE
            tile_major = lambda p: p.reshape(2, nj, GROUPS_PER_TILE * SSM_STATE).swapaxes(0, 1)
            wb = jnp.concatenate([_block_diag_in(bbr), _block_diag_in(bbi)], axis=-1).astype(BF16)
            wc = jnp.concatenate([_block_diag_out(s5_c_re[e]), -_block_diag_out(s5_c_im[e])], axis=1).astype(BF16)
            mix = _mixer0(u, batch, pool_w[e].astype(BF16), pool_scale[e].reshape(1, POOL_WIDTH), wb, wc,
                          tile_major(pwr), tile_major(pwi),
                          s5_d[e].reshape(1, SSM_WIDTH), s5_w_glu[e].astype(BF16))
            w_out = ab_w_out[e]
        else:
            o = layer // 2
            n_in = c_w_in.shape[-1]
            w_in = jnp.pad(c_w_in[o].astype(BF16), ((0, 0), (0, -n_in % LANES)))
            w_vt = w_in[:, Q_W + KV_W:Q_W + 2 * KV_W].T
            q, k, vt, qi, kw = _dsa_proj(h, g[0], w_in, w_vt, rope128, rope64)
            mix = _dsa(qi, kw, q, k, vt, batch)
            w_out = c_w_out[o]
        h = _matmul_norm_res(mix, w_out.astype(BF16), g[1], h)
        h = _mlp(h, g[2], g[3], mlp_w1, mlp_w2, layer)
    return h.reshape(batch, T_PAD, d)[:, N_META:T_REAL]
```

```python
import functools

import jax
import jax.numpy as jnp
from jax import lax
from jax.experimental import pallas as pl
from jax.experimental.pallas import tpu as pltpu

F32 = jnp.float32
BF16 = jnp.bfloat16

D_MODEL = 2048
CHUNK = 64
N_META = 16
POOL_WIDTH = 1024
POOL_WINDOWS = (2, 4, 8, 16)
POOL_GROUP = 256
POOL_HALO = 16
SSM_WIDTH = 1024
SSM_GROUP_CH = 16
SSM_GROUPS = 64
SSM_STATE = 64
GROUPS_PER_TILE = 8
HEAD_DIM = 128
N_HEADS = 16
N_KV_HEADS = 4
KV_GROUP = 4
IDX_HEADS = 16
IDX_DIM = 64
ROPE_THETA = 500000.0
ROPE_FRAC = 4
TOPK = 256
EPS = 1e-6
Q_W = N_HEADS * HEAD_DIM
KV_W = N_KV_HEADS * HEAD_DIM
IQ_W = IDX_HEADS * IDX_DIM

LANES = 128
SUBLANES = 8
VMEM_LIMIT = 56 * 1024 * 1024

T_REAL = N_META + 2048
T_PAD = 2176
TM = 512
TQ = 128
L_SSM = 544
SCAN_ROWS = L_SSM // 8
assert SCAN_ROWS * 8 == L_SSM
TF = 1024

LOG2_E = 1.4426950408889634
INT_MIN = -2 ** 31
NEG_INF_KEY = -2139095041


def _params(sem):
    return pltpu.CompilerParams(dimension_semantics=sem, vmem_limit_bytes=VMEM_LIMIT)


def _rms(x, g):
    return x * lax.rsqrt(jnp.mean(x * x, axis=-1, keepdims=True) + EPS) * g


def _const_spec(shape):
    zeros = (0,) * len(shape)
    return pl.BlockSpec(shape, lambda *_: zeros, pipeline_mode=pl.Buffered(1))


def _norm_proj_kernel(h_ref, g_ref, w_ref, o_ref):
    a = _rms(h_ref[...], g_ref[...]).astype(BF16)
    o_ref[...] = jnp.dot(a, w_ref[...], preferred_element_type=F32)


def _norm_proj(h, g, w):
    m, d = h.shape
    n = w.shape[1]
    return pl.pallas_call(
        _norm_proj_kernel,
        grid=(m // TM,),
        in_specs=[pl.BlockSpec((TM, d), lambda i: (i, 0)), _const_spec((1, d)), _const_spec((d, n))],
        out_specs=pl.BlockSpec((TM, n), lambda i: (i, 0)),
        out_shape=jax.ShapeDtypeStruct((m, n), F32),
        compiler_params=_params(("parallel",)),
        name="norm_proj",
    )(h, g.reshape(1, d), w)


def _rope_tables(head_dim, rows_per_batch, batch):
    r = head_dim // ROPE_FRAC
    half = r // 2
    inv = ROPE_THETA ** (-jnp.arange(half, dtype=F32) / half)
    ang = jnp.arange(rows_per_batch, dtype=F32)[:, None] * inv[None, :]
    cos, sin = jnp.cos(ang), jnp.sin(ang)
    ones = jnp.ones((rows_per_batch, head_dim - r), F32)
    zeros_h = jnp.zeros((rows_per_batch, half), F32)
    zeros_t = jnp.zeros((rows_per_batch, head_dim - r), F32)
    c = jnp.concatenate([cos, cos, ones], axis=1)
    s_up = jnp.concatenate([zeros_h, sin, zeros_t], axis=1)
    s_dn = jnp.concatenate([-sin, zeros_h, zeros_t], axis=1)
    reps = (batch, LANES // head_dim)
    return half, jnp.tile(c, reps), jnp.tile(s_up, reps), jnp.tile(s_dn, reps)


def _rope(ys, half, c, s_up, s_dn):
    return ys * c + pltpu.roll(ys, half, 1) * s_up + pltpu.roll(ys, LANES - half, 1) * s_dn


def _dsa_proj_kernel(h_ref, g_ref, w_ref, wvt_ref, c128_ref, u128_ref, d128_ref, c64_ref, u64_ref, d64_ref,
                     q_ref, k_ref, vt_ref, qi_ref, kw_ref, *, half128, half64):
    a = _rms(h_ref[...], g_ref[...]).astype(BF16)
    t128 = (half128, c128_ref[...], u128_ref[...], d128_ref[...])
    t64 = (half64, c64_ref[...], u64_ref[...], d64_ref[...])
    c0, c1, c2, c3 = Q_W, Q_W + KV_W, Q_W + 2 * KV_W, Q_W + 2 * KV_W + IQ_W

    def slabs(lo, hi):
        y = jnp.dot(a, w_ref[:, lo:hi], preferred_element_type=F32)
        return [y[:, s * LANES:(s + 1) * LANES] for s in range((hi - lo) // LANES)]

    for s, ys in enumerate(slabs(0, c0)):
        q_ref[:, s * LANES:(s + 1) * LANES] = (_rope(ys, *t128) * (HEAD_DIM ** -0.5 * LOG2_E)).astype(q_ref.dtype)
    for s, ys in enumerate(slabs(c0, c1)):
        k_ref[:, s * LANES:(s + 1) * LANES] = _rope(ys, *t128).astype(k_ref.dtype)
    vt_ref[...] = lax.dot_general(wvt_ref[...], a, (((1,), (1,)), ((), ())),
                                  preferred_element_type=F32).astype(vt_ref.dtype)
    for s, ys in enumerate(slabs(c2, c3)):
        qi_ref[s] = _rope(ys, *t64).astype(qi_ref.dtype)
    (ys,) = slabs(c3, c3 + LANES)
    lane = lax.broadcasted_iota(jnp.int32, ys.shape, 1)
    kw_ref[...] = jnp.where(lane < IDX_DIM, _rope(ys, *t64), ys)


def _dsa_proj(h, g, w, w_vt, rope128, rope64):
    m, d = h.shape
    n = w.shape[1]
    pairs = IQ_W // LANES
    row = lambda width: pl.BlockSpec((TM, width), lambda i: (i, 0))
    return pl.pallas_call(
        functools.partial(_dsa_proj_kernel, half128=rope128[0], half64=rope64[0]),
        grid=(m // TM,),
        in_specs=[row(d), _const_spec((1, d)), _const_spec((d, n)), _const_spec((KV_W, d))] + [row(LANES)] * 6,
        out_specs=[row(Q_W), row(KV_W), pl.BlockSpec((KV_W, TM), lambda i: (0, i)),
                   pl.BlockSpec((pairs, TM, LANES), lambda i: (0, i, 0)), row(LANES)],
        out_shape=[jax.ShapeDtypeStruct((m, Q_W), BF16), jax.ShapeDtypeStruct((m, KV_W), BF16),
                   jax.ShapeDtypeStruct((KV_W, m), BF16), jax.ShapeDtypeStruct((pairs, m, LANES), BF16),
                   jax.ShapeDtypeStruct((m, LANES), F32)],
        compiler_params=_params(("parallel",)),
        name="dsa_proj",
    )(h, g.reshape(1, d), w, w_vt, *rope128[1:], *rope64[1:])


def _matmul_norm_res_kernel(a_ref, w_ref, g_ref, r_ref, o_ref):
    y = jnp.dot(a_ref[...], w_ref[...], preferred_element_type=F32)
    o_ref[...] = r_ref[...] + _rms(y, g_ref[...])


def _matmul_norm_res(a, w, g, res):
    m, k = a.shape
    n = w.shape[1]
    return pl.pallas_call(
        _matmul_norm_res_kernel,
        grid=(m // TM,),
        in_specs=[pl.BlockSpec((TM, k), lambda i: (i, 0)), _const_spec((k, n)),
                  _const_spec((1, n)), pl.BlockSpec((TM, n), lambda i: (i, 0))],
        out_specs=pl.BlockSpec((TM, n), lambda i: (i, 0)),
        out_shape=jax.ShapeDtypeStruct((m, n), F32),
        compiler_params=_params(("parallel",)),
        name="matmul_norm_res",
    )(a, w, g.reshape(1, n), res)


def _mlp_kernel(h_ref, g_in_ref, g_out_ref, w1_ref, w2_ref, o_ref, a_ref, acc_ref):
    f = pl.program_id(1)

    @pl.when(f == 0)
    def _():
        a_ref[...] = _rms(h_ref[...], g_in_ref[...]).astype(BF16)
        acc_ref[...] = jnp.zeros_like(acc_ref)

    z = jnp.dot(a_ref[...], w1_ref[...], preferred_element_type=F32)
    z = jnp.square(jnp.maximum(z, 0.0)).astype(BF16)
    acc_ref[...] += jnp.dot(z, w2_ref[...], preferred_element_type=F32)

    @pl.when(f == pl.num_programs(1) - 1)
    def _():
        o_ref[...] = h_ref[...] + _rms(acc_ref[...], g_out_ref[...])


def _mlp(h, g_in, g_out, w1, w2, layer):
    m, d = h.shape
    ff = w1.shape[2]
    return pl.pallas_call(
        _mlp_kernel,
        grid=(m // TM, ff // TF),
        in_specs=[pl.BlockSpec((TM, d), lambda i, f: (i, 0)),
                  _const_spec((1, d)), _const_spec((1, d)),
                  pl.BlockSpec((None, d, TF), lambda i, f: (layer, 0, f)),
                  pl.BlockSpec((None, TF, d), lambda i, f: (layer, f, 0))],
        out_specs=pl.BlockSpec((TM, d), lambda i, f: (i, 0)),
        out_shape=jax.ShapeDtypeStruct((m, d), F32),
        scratch_shapes=[pltpu.VMEM((TM, d), BF16), pltpu.VMEM((TM, d), F32)],
        compiler_params=_params(("parallel", "arbitrary")),
        name="mlp",
    )(h, g_in.reshape(1, d), g_out.reshape(1, d), w1, w2)


def _cmul(ar, ai, br, bi):
    return ar * br - ai * bi, ar * bi + ai * br


def _discretise(ar, ai, log_dt):
    dt = jnp.exp(log_dt)
    mag = jnp.exp(dt * ar)
    return mag * jnp.cos(dt * ai), mag * jnp.sin(dt * ai)


def _ssm_prep_kernel(lre_ref, lim_ref, ldt_ref, lre2_ref, lim2_ref, ldt2_ref, bre_ref, bim_ref,
                     bbr_ref, bbi_ref, pwr_ref, pwi_ref):
    ar, ai = lre_ref[...], lim_ref[...]
    abr, abi = _discretise(ar, ai, ldt_ref[...])
    den = ar * ar + ai * ai
    zr, zi = abr - 1.0, abi
    fr = (zr * ar + zi * ai) / den
    fi = (zi * ar - zr * ai) / den
    br, bi = bre_ref[...], bim_ref[...]
    bbr_ref[...] = fr * br - fi * bi
    bbi_ref[...] = fr * bi + fi * br
    a1r, a1i = _discretise(lre2_ref[...], lim2_ref[...], ldt2_ref[...])
    pr, pi = a1r, a1i
    for _ in range(SCAN_ROWS - 1):
        pr, pi = _cmul(pr, pi, a1r, a1i)
    pwr_ref[0], pwi_ref[0] = a1r, a1i
    pwr_ref[1], pwi_ref[1] = pr, pi


def _ssm_prep(lam_re, lam_im, log_dt, b_re, b_im):
    g, n = lam_re.shape
    c = b_re.shape[-1]
    bt = lambda b: jnp.swapaxes(b, 1, 2)
    full = lambda *s: pl.BlockSpec(s, lambda: (0,) * len(s))
    return pl.pallas_call(
        _ssm_prep_kernel,
        in_specs=[full(g, 1, n), full(g, 1, n), full(g, 1, 1), full(g, n), full(g, n), full(g, 1),
                  full(g, c, n), full(g, c, n)],
        out_specs=[full(g, c, n), full(g, c, n), full(2, g, n), full(2, g, n)],
        out_shape=[jax.ShapeDtypeStruct((g, c, n), F32)] * 2 + [jax.ShapeDtypeStruct((2, g, n), F32)] * 2,
        name="ssm_prep",
    )(lam_re.reshape(g, 1, n), lam_im.reshape(g, 1, n), log_dt.reshape(g, 1, 1), lam_re, lam_im,
      log_dt.reshape(g, 1), bt(b_re), bt(b_im))


def _block_diag_in(bb):
    g, c, n = bb.shape
    r = bb.reshape(g // GROUPS_PER_TILE, GROUPS_PER_TILE, c, n)
    eye = jnp.eye(GROUPS_PER_TILE, dtype=bb.dtype)
    return jnp.einsum('jgcn,gh->jgchn', r, eye).reshape(g // GROUPS_PER_TILE, GROUPS_PER_TILE * c, GROUPS_PER_TILE * n)


def _block_diag_out(cc):
    g, c, n = cc.shape
    r = cc.reshape(g // GROUPS_PER_TILE, GROUPS_PER_TILE, c, n)
    eye = jnp.eye(GROUPS_PER_TILE, dtype=cc.dtype)
    return jnp.einsum('jgcn,gh->jgnhc', r, eye).reshape(g // GROUPS_PER_TILE, GROUPS_PER_TILE * n, GROUPS_PER_TILE * c)


def _row_on_sublanes(ref, j, k):
    return jnp.broadcast_to(ref[j, pl.ds(k, 1), :], (SUBLANES, ref.shape[2]))


def _mixer0_kernel(u_ref, pw_ref, ps_ref, wb_ref, wc_ref, pwr_ref, pwi_ref, d_ref, wg_ref, o_ref,
                   pbuf, uperm, ynat, sbuf, carry_ref):
    c = pl.program_id(1)
    L = u_ref.shape[0]
    sw = GROUPS_PER_TILE * SSM_STATE

    @pl.when(c == 0)
    def _():
        pbuf[0:POOL_HALO, :] = jnp.zeros((POOL_HALO, POOL_WIDTH), F32)
        carry_ref[...] = jnp.zeros_like(carry_ref)

    pbuf[POOL_HALO:POOL_HALO + L, :] = u_ref[:, 0:POOL_WIDTH]
    tpos = c * L + lax.broadcasted_iota(jnp.int32, (L, 1), 0)
    for g, win in enumerate(POOL_WINDOWS):
        sl = slice(g * POOL_GROUP, (g + 1) * POOL_GROUP)
        x = u_ref[:, sl]
        s = x
        for k in range(1, win):
            s = s + pbuf[POOL_HALO - k:POOL_HALO - k + L, sl]
        cnt = jnp.minimum(tpos + 1, win).astype(F32)
        diff = (s / cnt - x).astype(BF16)
        y = jnp.dot(diff, pw_ref[g], preferred_element_type=F32) * ps_ref[:, sl]
        o_ref[:, sl] = y.astype(o_ref.dtype)
    pbuf[0:POOL_HALO, :] = pbuf[L:L + POOL_HALO, :]

    n_tiles = SSM_WIDTH // LANES
    for j in range(n_tiles):
        ynat[j] = u_ref[:, POOL_WIDTH + j * LANES:POOL_WIDTH + (j + 1) * LANES]

    def permute(k, carry):
        for j in range(n_tiles):
            uperm[j, pl.ds(pl.multiple_of(k * SUBLANES, SUBLANES), SUBLANES), :] = \
                ynat[j, pl.ds(k, SUBLANES, stride=SCAN_ROWS), :]
        return carry

    lax.fori_loop(0, SCAN_ROWS, permute, 0)
    row = lax.broadcasted_iota(jnp.int32, (SUBLANES, sw), 0)
    for j in range(n_tiles):
        uj = uperm[j]
        sbuf[...] = jnp.dot(uj.astype(BF16), wb_ref[j], preferred_element_type=F32)
        a1r = _row_on_sublanes(pwr_ref, j, 0)
        a1i = _row_on_sublanes(pwi_ref, j, 0)

        def scan_step(k, x, store):
            xr, xi = x
            rows = pl.ds(pl.multiple_of(k * SUBLANES, SUBLANES), SUBLANES)
            xr, xi = a1r * xr - a1i * xi + sbuf[rows, 0:sw], a1r * xi + a1i * xr + sbuf[rows, sw:2 * sw]
            if store:
                sbuf[rows, 0:sw] = xr
                sbuf[rows, sw:2 * sw] = xi
            return xr, xi

        zero = jnp.zeros((SUBLANES, sw), F32)
        er, ei = lax.fori_loop(0, SCAN_ROWS, functools.partial(scan_step, store=False), (zero, zero))

        cr = _row_on_sublanes(carry_ref, j, 0)
        ci = _row_on_sublanes(carry_ref, j, 1)
        sr = jnp.where(row == 0, cr, pltpu.roll(er, 1, 0))
        si = jnp.where(row == 0, ci, pltpu.roll(ei, 1, 0))
        qr = _row_on_sublanes(pwr_ref, j, 1)
        qi = _row_on_sublanes(pwi_ref, j, 1)
        for k in (1, 2, 4):
            mr, mi = jnp.where(row >= k, qr, 0.0), jnp.where(row >= k, qi, 0.0)
            tr, ti = pltpu.roll(sr, k, 0), pltpu.roll(si, k, 0)
            sr, si = sr + mr * tr - mi * ti, si + mr * ti + mi * tr
            qr, qi = _cmul(qr, qi, qr, qi)

        lax.fori_loop(0, SCAN_ROWS, functools.partial(scan_step, store=True), (sr, si))
        carry_ref[j, 0:1, :] = sbuf[L - 1:L, 0:sw]
        carry_ref[j, 1:2, :] = sbuf[L - 1:L, sw:2 * sw]
        yj = jnp.dot(sbuf[...].astype(BF16), wc_ref[j], preferred_element_type=F32)
        uperm[j] = yj + d_ref[:, j * LANES:(j + 1) * LANES] * uj

    def unpermute(k, carry):
        for j in range(n_tiles):
            ynat[j, pl.ds(k, SUBLANES, stride=SCAN_ROWS), :] = \
                uperm[j, pl.ds(pl.multiple_of(k * SUBLANES, SUBLANES), SUBLANES), :]
        return carry

    lax.fori_loop(0, SCAN_ROWS, unpermute, 0)
    y = jax.nn.gelu(jnp.concatenate([ynat[j] for j in range(n_tiles)], axis=1))
    gate = jax.nn.sigmoid(jnp.dot(y.astype(BF16), wg_ref[...], preferred_element_type=F32))
    o_ref[:, POOL_WIDTH:] = (y * gate).astype(o_ref.dtype)


def _mixer0(u, batch, pool_w, pool_scale, wb, wc, pwr, pwi, d, w_glu):
    m, width = u.shape
    nj = SSM_WIDTH // LANES
    sw = GROUPS_PER_TILE * SSM_STATE
    chunks = T_PAD // L_SSM
    return pl.pallas_call(
        _mixer0_kernel,
        grid=(batch, chunks),
        in_specs=[pl.BlockSpec((L_SSM, width), lambda b, c: (b * chunks + c, 0)),
                  _const_spec((len(POOL_WINDOWS), POOL_GROUP, POOL_GROUP)), _const_spec((1, POOL_WIDTH)),
                  _const_spec((nj, LANES, 2 * sw)), _const_spec((nj, 2 * sw, LANES)),
                  _const_spec((nj, 2, sw)), _const_spec((nj, 2, sw)),
                  _const_spec((1, SSM_WIDTH)), _const_spec((SSM_WIDTH, SSM_WIDTH))],
        out_specs=pl.BlockSpec((L_SSM, width), lambda b, c: (b * chunks + c, 0)),
        out_shape=jax.ShapeDtypeStruct((m, width), BF16),
        scratch_shapes=[pltpu.VMEM((POOL_HALO + L_SSM, POOL_WIDTH), F32),
                        pltpu.VMEM((nj, L_SSM, LANES), F32),
                        pltpu.VMEM((nj, L_SSM, LANES), F32),
                        pltpu.VMEM((L_SSM, 2 * sw), F32),
                        pltpu.VMEM((nj, 2, sw), F32)],
        compiler_params=_params(("parallel", "arbitrary")),
        name="mixer0",
    )(u, pool_w, pool_scale, wb, wc, pwr, pwi, d, w_glu)


CHUNK_SHIFT = CHUNK.bit_length() - 1
assert 1 << CHUNK_SHIFT == CHUNK
KB = 512
KB_SHIFT = KB.bit_length() - 1
assert 1 << KB_SHIFT == KB
N_KB = -(-T_PAD // KB)
COUNT_ROWS = 64
PAIRS_PER_DOT = 2
ONES_ROWS = 16
VT_ROWS = HEAD_DIM + ONES_ROWS


def _chunk_id(t):
    return jnp.where(t < N_META, 0, ((t - N_META) >> CHUNK_SHIFT) + 1)


def _fold_rows(x, op):
    return op(x.reshape(x.shape[0] // SUBLANES, SUBLANES, x.shape[1]), axis=0)


def _stage_rows(src, dst):
    for kb in range(N_KB):
        n = min(KB, T_PAD - kb * KB)
        dst[kb, 0:n, :] = src[kb * KB:kb * KB + n, :].astype(dst.dtype)
        if n < KB:
            dst[kb, n:KB, :] = jnp.zeros((KB - n, dst.shape[2]), dst.dtype)


def _stage_values(src, dst):
    for kb in range(N_KB):
        n = min(KB, T_PAD - kb * KB)
        for kv in range(N_KV_HEADS):
            r0 = kv * VT_ROWS
            dst[kb, r0:r0 + HEAD_DIM, 0:n] = src[kv * HEAD_DIM:(kv + 1) * HEAD_DIM, kb * KB:kb * KB + n]
            if n < KB:
                dst[kb, r0:r0 + HEAD_DIM, n:KB] = jnp.zeros((HEAD_DIM, KB - n), dst.dtype)
            dst[kb, r0 + HEAD_DIM:r0 + VT_ROWS, :] = jnp.ones((ONES_ROWS, KB), dst.dtype)


def _dsa_kernel(qi_ref, wq_ref, q_ref, kw_ref, k_ref, vt_ref, o_ref,
                kie, kio, kbuf, vtbuf, key_buf, lg_buf, m_acc, o_acc):
    i = pl.program_id(1)
    tq = q_ref.shape[0]
    nt = (((1,), (1,)), ((), ()))

    @pl.when(i == 0)
    def _():
        kw = kw_ref[...]
        lane = lax.broadcasted_iota(jnp.int32, kw.shape, 1)
        _stage_rows(jnp.where(lane < IDX_DIM, kw, 0.0), kie)
        _stage_rows(jnp.where(lane >= IDX_DIM, pltpu.roll(kw, IDX_DIM, 1), 0.0), kio)
        _stage_rows(k_ref, kbuf)
        _stage_values(vt_ref, vtbuf)

    last_q = (i + 1) * tq - 1
    key_end = N_META + CHUNK * (((last_q - N_META) >> CHUNK_SHIFT) + 1)
    n_kb = jnp.minimum((key_end + KB - 1) >> KB_SHIFT, N_KB)

    w_t = wq_ref[...].T * ((IDX_HEADS ** -0.5) * (IDX_DIM ** -0.5))
    pairs = IDX_HEADS // 2
    qpos = i * tq + lax.broadcasted_iota(jnp.int32, (1, tq), 1)

    def index_block(kb, carry):
        score = jnp.zeros((KB, tq), F32)
        for p0 in range(0, pairs, PAIRS_PER_DOT):
            qg = qi_ref[p0:p0 + PAIRS_PER_DOT].reshape(PAIRS_PER_DOT * tq, LANES)
            se = lax.dot_general(kie[kb], qg, nt, preferred_element_type=F32)
            so = lax.dot_general(kio[kb], qg, nt, preferred_element_type=F32)
            for c in range(PAIRS_PER_DOT):
                p = p0 + c
                we = w_t[IDX_DIM + 2 * p:IDX_DIM + 2 * p + 1, :]
                wo = w_t[IDX_DIM + 2 * p + 1:IDX_DIM + 2 * p + 2, :]
                score = score + we * jnp.maximum(se[:, c * tq:(c + 1) * tq], 0.0)
                score = score + wo * jnp.maximum(so[:, c * tq:(c + 1) * tq], 0.0)
        kpos = kb * KB + lax.broadcasted_iota(jnp.int32, (KB, 1), 0)
        allowed = (_chunk_id(kpos) <= _chunk_id(qpos)) & (kpos < T_REAL)
        bits = pltpu.bitcast(jnp.where(allowed, score, -jnp.inf), jnp.int32)
        key_buf[kb] = jnp.where(bits < 0, bits ^ 0x7FFFFFFF, bits)
        return carry

    lax.fori_loop(0, n_kb, index_block, 0)

    def count_ge(cand):
        def block(kb, acc):
            ind = jnp.where(key_buf[kb] >= cand, 1.0, 0.0)
            return acc + jnp.sum(ind.reshape(KB // COUNT_ROWS, COUNT_ROWS, tq), axis=0)
        acc = lax.fori_loop(0, n_kb, block, jnp.zeros((COUNT_ROWS, tq), F32))
        return jnp.sum(acc, axis=0, keepdims=True)

    c0 = count_ge(0)
    th = jnp.where(c0 >= TOPK, 0, INT_MIN).astype(jnp.int32)
    n_ge = jnp.where(c0 >= TOPK, c0, float(N_KB * KB))

    def search(j, carry):
        th, n_ge = carry
        cand = th | jnp.left_shift(jnp.int32(1), 30 - j)
        c = count_ge(cand)
        return jnp.where(c >= TOPK, cand, th), jnp.where(c >= TOPK, c, n_ge)

    th, n_ge = lax.fori_loop(0, 31, search, (th, n_ge))
    tied = (n_ge > TOPK) & (th > NEG_INF_KEY)
    th = jnp.maximum(th, NEG_INF_KEY + 1)

    @pl.when(jnp.max(tied.astype(jnp.int32)) > 0)
    def _():
        def count_eq(kb, acc):
            ind = jnp.where(key_buf[kb] == th, 1.0, 0.0)
            return acc + jnp.sum(ind.reshape(KB // COUNT_ROWS, COUNT_ROWS, tq), axis=0)
        n_eq = jnp.sum(lax.fori_loop(0, n_kb, count_eq, jnp.zeros((COUNT_ROWS, tq), F32)), axis=0, keepdims=True)
        keep = TOPK - (n_ge - n_eq)
        lower = (lax.broadcasted_iota(jnp.int32, (KB, KB), 0) >= lax.broadcasted_iota(jnp.int32, (KB, KB), 1))
        lower = jnp.where(lower, 1.0, 0.0).astype(BF16)

        def drop_late(kb, seen):
            key = key_buf[kb]
            eq = (key == th) & tied
            rank = seen + jnp.dot(lower, jnp.where(eq, 1.0, 0.0).astype(BF16), preferred_element_type=F32)
            key_buf[kb] = jnp.where(eq & (rank > keep), INT_MIN, key)
            return rank[KB - 1:KB, :]

        lax.fori_loop(0, n_kb, drop_late, jnp.zeros((1, tq), F32))

    qs = []
    for kv in range(N_KV_HEADS):
        h0 = kv * KV_GROUP
        qs.append(jnp.concatenate([q_ref[:, (h0 + g) * HEAD_DIM:(h0 + g + 1) * HEAD_DIM]
                                   for g in range(KV_GROUP)], axis=0))
    m_acc[...] = jnp.full(m_acc.shape, -jnp.inf, F32)
    o_acc[...] = jnp.zeros(o_acc.shape, F32)

    def logits_block(kb, carry):
        sel = key_buf[kb] >= th
        for kv in range(N_KV_HEADS):
            kk = kbuf[kb, :, kv * HEAD_DIM:(kv + 1) * HEAD_DIM]
            lg = lax.dot_general(kk, qs[kv], nt, preferred_element_type=F32)
            lg = jnp.concatenate([jnp.where(sel, lg[:, g * tq:(g + 1) * tq], -jnp.inf)
                                  for g in range(KV_GROUP)], axis=1)
            lg_buf[kv, kb] = lg
            m_acc[kv] = jnp.maximum(m_acc[kv], _fold_rows(lg, jnp.max))
        return carry

    lax.fori_loop(0, n_kb, logits_block, 0)
    mx = [jnp.max(m_acc[kv], axis=0, keepdims=True) for kv in range(N_KV_HEADS)]

    def value_block(kb, carry):
        for kv in range(N_KV_HEADS):
            e = jnp.exp2((lg_buf[kv, kb] - mx[kv]).astype(BF16))
            vt = vtbuf[kb, kv * VT_ROWS:(kv + 1) * VT_ROWS, :]
            o_acc[kv] += jnp.dot(vt, e, preferred_element_type=F32)
        return carry

    lax.fori_loop(0, n_kb, value_block, 0)
    for kv in range(N_KV_HEADS):
        out_t = o_acc[kv, 0:HEAD_DIM, :] / o_acc[kv, HEAD_DIM:HEAD_DIM + 1, :]
        for g in range(KV_GROUP):
            hd = kv * KV_GROUP + g
            o_ref[:, hd * HEAD_DIM:(hd + 1) * HEAD_DIM] = out_t[:, g * tq:(g + 1) * tq].T.astype(o_ref.dtype)


def _dsa(qi, kw, q, k, vt, batch):
    m = q.shape[0]
    nq = T_PAD // TQ
    pairs = qi.shape[0]
    cols = KV_GROUP * TQ
    return pl.pallas_call(
        _dsa_kernel,
        grid=(batch, nq),
        in_specs=[pl.BlockSpec((pairs, TQ, LANES), lambda b, i: (0, b * nq + i, 0)),
                  pl.BlockSpec((TQ, LANES), lambda b, i: (b * nq + i, 0)),
                  pl.BlockSpec((TQ, Q_W), lambda b, i: (b * nq + i, 0)),
                  pl.BlockSpec((T_PAD, LANES), lambda b, i: (b, 0)),
                  pl.BlockSpec((T_PAD, KV_W), lambda b, i: (b, 0)),
                  pl.BlockSpec((KV_W, T_PAD), lambda b, i: (0, b))],
        out_specs=pl.BlockSpec((TQ, Q_W), lambda b, i: (b * nq + i, 0)),
        out_shape=jax.ShapeDtypeStruct((m, Q_W), BF16),
        scratch_shapes=[pltpu.VMEM((N_KB, KB, LANES), BF16), pltpu.VMEM((N_KB, KB, LANES), BF16),
                        pltpu.VMEM((N_KB, KB, KV_W), BF16), pltpu.VMEM((N_KB, N_KV_HEADS * VT_ROWS, KB), BF16),
                        pltpu.VMEM((N_KB, KB, TQ), jnp.int32),
                        pltpu.VMEM((N_KV_HEADS, N_KB, KB, cols), F32),
                        pltpu.VMEM((N_KV_HEADS, SUBLANES, cols), F32),
                        pltpu.VMEM((N_KV_HEADS, VT_ROWS, cols), F32)],
        compiler_params=_params(("parallel", "arbitrary")),
        name="dsa",
    )(qi, kw, q, kw, k, vt)


def kernel(x, meta, norm_g, ab_w_in, pool_w, pool_scale, s5_lambda_re, s5_lambda_im, s5_log_dt,
           s5_b_re, s5_b_im, s5_c_re, s5_c_im, s5_d, s5_w_glu, ab_w_out, c_w_in, c_w_out, mlp_w1, mlp_w2):
    batch, seq, d = x.shape
    assert (seq + N_META, d) == (T_REAL, D_MODEL)
    depth = norm_g.shape[0]
    meta_b = jnp.broadcast_to(meta[None].astype(x.dtype), (batch, N_META, d))
    pad = jnp.zeros((batch, T_PAD - T_REAL, d), x.dtype)
    h = jnp.concatenate([meta_b, x, pad], axis=1).reshape(batch * T_PAD, d)

    rope128 = _rope_tables(HEAD_DIM, T_PAD, batch)
    rope64 = _rope_tables(IDX_DIM, T_PAD, batch)
    w1, w2 = mlp_w1.astype(BF16), mlp_w2.astype(BF16)

    for layer in range(depth):
        g = norm_g[layer]
        if layer % 2 == 0:
            e = layer // 2
            u = _norm_proj(h, g[0], ab_w_in[e].astype(BF16))
            bbr, bbi, pwr, pwi = _ssm_prep(s5_lambda_re[e], s5_lambda_im[e], s5_log_dt[e], s5_b_re[e], s5_b_im[e])
            nj = SSM_GROUPS // GROUPS_PER_TILE
            tile_major = lambda p: p.reshape(2, nj, GROUPS_PER_TILE * SSM_STATE).swapaxes(0, 1)
            wb = jnp.concatenate([_block_diag_in(bbr), _block_diag_in(bbi)], axis=-1).astype(BF16)
            wc = jnp.concatenate([_block_diag_out(s5_c_re[e]), -_block_diag_out(s5_c_im[e])], axis=1).astype(BF16)
            mix = _mixer0(u, batch, pool_w[e].astype(BF16), pool_scale[e].reshape(1, POOL_WIDTH), wb, wc,
                          tile_major(pwr), tile_major(pwi),
                          s5_d[e].reshape(1, SSM_WIDTH), s5_w_glu[e].astype(BF16))
            w_out = ab_w_out[e]
        else:
            o = layer // 2
            n_in = c_w_in.shape[-1]
            w_in = jnp.pad(c_w_in[o].astype(BF16), ((0, 0), (0, -n_in % LANES)))
            w_vt = w_in[:, Q_W + KV_W:Q_W + 2 * KV_W].T
            q, k, vt, qi, kw = _dsa_proj(h, g[0], w_in, w_vt, rope128, rope64)
            mix = _dsa(qi, kw, q, k, vt, batch)
            w_out = c_w_out[o]
        h = _matmul_norm_res(mix, w_out.astype(BF16), g[1], h)
        h = _mlp(h, g[2], g[3], w1, w2, layer)
    return h.reshape(batch, T_PAD, d)[:, N_META:T_REAL]
```

```python
import functools

import jax
import jax.numpy as jnp
from jax import lax
from jax.experimental import pallas as pl
from jax.experimental.pallas import tpu as pltpu

F32 = jnp.float32
BF16 = jnp.bfloat16

D_MODEL = 2048
CHUNK = 64
N_META = 16
POOL_WIDTH = 1024
POOL_WINDOWS = (2, 4, 8, 16)
POOL_GROUP = 256
POOL_HALO = 16
SSM_WIDTH = 1024
SSM_GROUP_CH = 16
SSM_GROUPS = 64
SSM_STATE = 64
GROUPS_PER_TILE = 8
HEAD_DIM = 128
N_HEADS = 16
N_KV_HEADS = 4
KV_GROUP = 4
IDX_HEADS = 16
IDX_DIM = 64
ROPE_THETA = 500000.0
ROPE_FRAC = 4
TOPK = 256
EPS = 1e-6
Q_W = N_HEADS * HEAD_DIM
KV_W = N_KV_HEADS * HEAD_DIM
IQ_W = IDX_HEADS * IDX_DIM

LANES = 128
SUBLANES = 8
VMEM_LIMIT = 56 * 1024 * 1024

T_REAL = N_META + 2048
T_PAD = 2176
TM = 512
TQ = 128
L_SSM = 544
SCAN_ROWS = L_SSM // 8
assert SCAN_ROWS * 8 == L_SSM
TF = 1024

LOG2_E = 1.4426950408889634
INT_MIN = -2 ** 31
NEG_INF_KEY = -2139095041


def _params(sem):
    return pltpu.CompilerParams(dimension_semantics=sem, vmem_limit_bytes=VMEM_LIMIT)


def _rms(x, g):
    return x * lax.rsqrt(jnp.mean(x * x, axis=-1, keepdims=True) + EPS) * g


def _const_spec(shape):
    zeros = (0,) * len(shape)
    return pl.BlockSpec(shape, lambda *_: zeros, pipeline_mode=pl.Buffered(1))


def _norm_proj_kernel(h_ref, g_ref, w_ref, o_ref):
    a = _rms(h_ref[...], g_ref[...]).astype(BF16)
    o_ref[...] = jnp.dot(a, w_ref[...], preferred_element_type=F32)


def _norm_proj(h, g, w):
    m, d = h.shape
    n = w.shape[1]
    return pl.pallas_call(
        _norm_proj_kernel,
        grid=(m // TM,),
        in_specs=[pl.BlockSpec((TM, d), lambda i: (i, 0)), _const_spec((1, d)), _const_spec((d, n))],
        out_specs=pl.BlockSpec((TM, n), lambda i: (i, 0)),
        out_shape=jax.ShapeDtypeStruct((m, n), F32),
        compiler_params=_params(("parallel",)),
        name="norm_proj",
    )(h, g.reshape(1, d), w)


def _rope_tables(head_dim, rows_per_batch, batch):
    r = head_dim // ROPE_FRAC
    half = r // 2
    inv = ROPE_THETA ** (-jnp.arange(half, dtype=F32) / half)
    ang = jnp.arange(rows_per_batch, dtype=F32)[:, None] * inv[None, :]
    cos, sin = jnp.cos(ang), jnp.sin(ang)
    ones = jnp.ones((rows_per_batch, head_dim - r), F32)
    zeros_h = jnp.zeros((rows_per_batch, half), F32)
    zeros_t = jnp.zeros((rows_per_batch, head_dim - r), F32)
    c = jnp.concatenate([cos, cos, ones], axis=1)
    s_up = jnp.concatenate([zeros_h, sin, zeros_t], axis=1)
    s_dn = jnp.concatenate([-sin, zeros_h, zeros_t], axis=1)
    reps = (batch, LANES // head_dim)
    return half, jnp.tile(c, reps), jnp.tile(s_up, reps), jnp.tile(s_dn, reps)


def _rope(ys, half, c, s_up, s_dn):
    return ys * c + pltpu.roll(ys, half, 1) * s_up + pltpu.roll(ys, LANES - half, 1) * s_dn


def _dsa_proj_kernel(h_ref, g_ref, w_ref, wvt_ref, c128_ref, u128_ref, d128_ref, c64_ref, u64_ref, d64_ref,
                     q_ref, k_ref, vt_ref, qi_ref, kw_ref, *, half128, half64):
    a = _rms(h_ref[...], g_ref[...]).astype(BF16)
    t128 = (half128, c128_ref[...], u128_ref[...], d128_ref[...])
    t64 = (half64, c64_ref[...], u64_ref[...], d64_ref[...])
    c0, c1, c2, c3 = Q_W, Q_W + KV_W, Q_W + 2 * KV_W, Q_W + 2 * KV_W + IQ_W

    def slabs(lo, hi):
        y = jnp.dot(a, w_ref[:, lo:hi], preferred_element_type=F32)
        return [y[:, s * LANES:(s + 1) * LANES] for s in range((hi - lo) // LANES)]

    for s, ys in enumerate(slabs(0, c0)):
        q_ref[:, s * LANES:(s + 1) * LANES] = (_rope(ys, *t128) * (HEAD_DIM ** -0.5 * LOG2_E)).astype(q_ref.dtype)
    for s, ys in enumerate(slabs(c0, c1)):
        k_ref[:, s * LANES:(s + 1) * LANES] = _rope(ys, *t128).astype(k_ref.dtype)
    vt_ref[...] = lax.dot_general(wvt_ref[...], a, (((1,), (1,)), ((), ())),
                                  preferred_element_type=F32).astype(vt_ref.dtype)
    for s, ys in enumerate(slabs(c2, c3)):
        qi_ref[s] = _rope(ys, *t64).astype(qi_ref.dtype)
    (ys,) = slabs(c3, c3 + LANES)
    lane = lax.broadcasted_iota(jnp.int32, ys.shape, 1)
    kw_ref[...] = jnp.where(lane < IDX_DIM, _rope(ys, *t64), ys)


def _dsa_proj(h, g, w, w_vt, rope128, rope64):
    m, d = h.shape
    n = w.shape[1]
    pairs = IQ_W // LANES
    row = lambda width: pl.BlockSpec((TM, width), lambda i: (i, 0))
    return pl.pallas_call(
        functools.partial(_dsa_proj_kernel, half128=rope128[0], half64=rope64[0]),
        grid=(m // TM,),
        in_specs=[row(d), _const_spec((1, d)), _const_spec((d, n)), _const_spec((KV_W, d))] + [row(LANES)] * 6,
        out_specs=[row(Q_W), row(KV_W), pl.BlockSpec((KV_W, TM), lambda i: (0, i)),
                   pl.BlockSpec((pairs, TM, LANES), lambda i: (0, i, 0)), row(LANES)],
        out_shape=[jax.ShapeDtypeStruct((m, Q_W), BF16), jax.ShapeDtypeStruct((m, KV_W), BF16),
                   jax.ShapeDtypeStruct((KV_W, m), BF16), jax.ShapeDtypeStruct((pairs, m, LANES), BF16),
                   jax.ShapeDtypeStruct((m, LANES), F32)],
        compiler_params=_params(("parallel",)),
        name="dsa_proj",
    )(h, g.reshape(1, d), w, w_vt, *rope128[1:], *rope64[1:])


def _matmul_norm_res_kernel(a_ref, w_ref, g_ref, r_ref, o_ref):
    y = jnp.dot(a_ref[...], w_ref[...], preferred_element_type=F32)
    o_ref[...] = r_ref[...] + _rms(y, g_ref[...])


def _matmul_norm_res(a, w, g, res):
    m, k = a.shape
    n = w.shape[1]
    return pl.pallas_call(
        _matmul_norm_res_kernel,
        grid=(m // TM,),
        in_specs=[pl.BlockSpec((TM, k), lambda i: (i, 0)), _const_spec((k, n)),
                  _const_spec((1, n)), pl.BlockSpec((TM, n), lambda i: (i, 0))],
        out_specs=pl.BlockSpec((TM, n), lambda i: (i, 0)),
        out_shape=jax.ShapeDtypeStruct((m, n), F32),
        compiler_params=_params(("parallel",)),
        name="matmul_norm_res",
    )(a, w, g.reshape(1, n), res)


def _mlp_kernel(h_ref, g_in_ref, g_out_ref, w1_ref, w2_ref, o_ref, a_ref, acc_ref):
    f = pl.program_id(1)

    @pl.when(f == 0)
    def _():
        a_ref[...] = _rms(h_ref[...], g_in_ref[...]).astype(BF16)
        acc_ref[...] = jnp.zeros_like(acc_ref)

    z = jnp.dot(a_ref[...], w1_ref[...], preferred_element_type=F32)
    z = jnp.square(jnp.maximum(z, 0.0)).astype(BF16)
    acc_ref[...] += jnp.dot(z, w2_ref[...], preferred_element_type=F32)

    @pl.when(f == pl.num_programs(1) - 1)
    def _():
        o_ref[...] = h_ref[...] + _rms(acc_ref[...], g_out_ref[...])


def _mlp(h, g_in, g_out, w1, w2, layer):
    m, d = h.shape
    ff = w1.shape[2]
    return pl.pallas_call(
        _mlp_kernel,
        grid=(m // TM, ff // TF),
        in_specs=[pl.BlockSpec((TM, d), lambda i, f: (i, 0)),
                  _const_spec((1, d)), _const_spec((1, d)),
                  pl.BlockSpec((None, d, TF), lambda i, f: (layer, 0, f)),
                  pl.BlockSpec((None, TF, d), lambda i, f: (layer, f, 0))],
        out_specs=pl.BlockSpec((TM, d), lambda i, f: (i, 0)),
        out_shape=jax.ShapeDtypeStruct((m, d), F32),
        scratch_shapes=[pltpu.VMEM((TM, d), BF16), pltpu.VMEM((TM, d), F32)],
        compiler_params=_params(("parallel", "arbitrary")),
        name="mlp",
    )(h, g_in.reshape(1, d), g_out.reshape(1, d), w1, w2)


def _cmul(ar, ai, br, bi):
    return ar * br - ai * bi, ar * bi + ai * br


def _discretise(ar, ai, log_dt):
    dt = jnp.exp(log_dt)
    mag = jnp.exp(dt * ar)
    return mag * jnp.cos(dt * ai), mag * jnp.sin(dt * ai)


def _ssm_prep_kernel(lre_ref, lim_ref, ldt_ref, lre2_ref, lim2_ref, ldt2_ref, bre_ref, bim_ref,
                     bbr_ref, bbi_ref, pwr_ref, pwi_ref):
    ar, ai = lre_ref[...], lim_ref[...]
    abr, abi = _discretise(ar, ai, ldt_ref[...])
    den = ar * ar + ai * ai
    zr, zi = abr - 1.0, abi
    fr = (zr * ar + zi * ai) / den
    fi = (zi * ar - zr * ai) / den
    br, bi = bre_ref[...], bim_ref[...]
    bbr_ref[...] = fr * br - fi * bi
    bbi_ref[...] = fr * bi + fi * br
    a1r, a1i = _discretise(lre2_ref[...], lim2_ref[...], ldt2_ref[...])
    pr, pi = a1r, a1i
    for _ in range(SCAN_ROWS - 1):
        pr, pi = _cmul(pr, pi, a1r, a1i)
    pwr_ref[0], pwi_ref[0] = a1r, a1i
    pwr_ref[1], pwi_ref[1] = pr, pi


def _ssm_prep(lam_re, lam_im, log_dt, b_re, b_im):
    g, n = lam_re.shape
    c = b_re.shape[-1]
    bt = lambda b: jnp.swapaxes(b, 1, 2)
    full = lambda *s: pl.BlockSpec(s, lambda: (0,) * len(s))
    return pl.pallas_call(
        _ssm_prep_kernel,
        in_specs=[full(g, 1, n), full(g, 1, n), full(g, 1, 1), full(g, n), full(g, n), full(g, 1),
                  full(g, c, n), full(g, c, n)],
        out_specs=[full(g, c, n), full(g, c, n), full(2, g, n), full(2, g, n)],
        out_shape=[jax.ShapeDtypeStruct((g, c, n), F32)] * 2 + [jax.ShapeDtypeStruct((2, g, n), F32)] * 2,
        name="ssm_prep",
    )(lam_re.reshape(g, 1, n), lam_im.reshape(g, 1, n), log_dt.reshape(g, 1, 1), lam_re, lam_im,
      log_dt.reshape(g, 1), bt(b_re), bt(b_im))


def _block_diag_in(bb):
    g, c, n = bb.shape
    r = bb.reshape(g // GROUPS_PER_TILE, GROUPS_PER_TILE, c, n)
    eye = jnp.eye(GROUPS_PER_TILE, dtype=bb.dtype)
    return jnp.einsum('jgcn,gh->jgchn', r, eye).reshape(g // GROUPS_PER_TILE, GROUPS_PER_TILE * c, GROUPS_PER_TILE * n)


def _block_diag_out(cc):
    g, c, n = cc.shape
    r = cc.reshape(g // GROUPS_PER_TILE, GROUPS_PER_TILE, c, n)
    eye = jnp.eye(GROUPS_PER_TILE, dtype=cc.dtype)
    return jnp.einsum('jgcn,gh->jgnhc', r, eye).reshape(g // GROUPS_PER_TILE, GROUPS_PER_TILE * n, GROUPS_PER_TILE * c)


def _row_on_sublanes(ref, j, k):
    return jnp.broadcast_to(ref[j, pl.ds(k, 1), :], (SUBLANES, ref.shape[2]))


def _mixer0_kernel(u_ref, pw_ref, ps_ref, wb_ref, wc_ref, pwr_ref, pwi_ref, d_ref, wg_ref, o_ref,
                   pbuf, uperm, ynat, sbuf, carry_ref):
    c = pl.program_id(1)
    L = u_ref.shape[0]
    sw = GROUPS_PER_TILE * SSM_STATE

    @pl.when(c == 0)
    def _():
        pbuf[0:POOL_HALO, :] = jnp.zeros((POOL_HALO, POOL_WIDTH), F32)
        carry_ref[...] = jnp.zeros_like(carry_ref)

    pbuf[POOL_HALO:POOL_HALO + L, :] = u_ref[:, 0:POOL_WIDTH]
    tpos = c * L + lax.broadcasted_iota(jnp.int32, (L, 1), 0)
    for g, win in enumerate(POOL_WINDOWS):
        sl = slice(g * POOL_GROUP, (g + 1) * POOL_GROUP)
        x = u_ref[:, sl]
        s = x
        for k in range(1, win):
            s = s + pbuf[POOL_HALO - k:POOL_HALO - k + L, sl]
        cnt = jnp.minimum(tpos + 1, win).astype(F32)
        diff = (s / cnt - x).astype(BF16)
        y = jnp.dot(diff, pw_ref[g], preferred_element_type=F32) * ps_ref[:, sl]
        o_ref[:, sl] = y.astype(o_ref.dtype)
    pbuf[0:POOL_HALO, :] = pbuf[L:L + POOL_HALO, :]

    n_tiles = SSM_WIDTH // LANES
    for j in range(n_tiles):
        ynat[j] = u_ref[:, POOL_WIDTH + j * LANES:POOL_WIDTH + (j + 1) * LANES]

    def permute(k, carry):
        for j in range(n_tiles):
            uperm[j, pl.ds(pl.multiple_of(k * SUBLANES, SUBLANES), SUBLANES), :] = \
                ynat[j, pl.ds(k, SUBLANES, stride=SCAN_ROWS), :]
        return carry

    lax.fori_loop(0, SCAN_ROWS, permute, 0)
    row = lax.broadcasted_iota(jnp.int32, (SUBLANES, sw), 0)
    for j in range(n_tiles):
        uj = uperm[j]
        sbuf[...] = jnp.dot(uj.astype(BF16), wb_ref[j], preferred_element_type=F32)
        a1r = _row_on_sublanes(pwr_ref, j, 0)
        a1i = _row_on_sublanes(pwi_ref, j, 0)

        def scan_step(k, x, store):
            xr, xi = x
            rows = pl.ds(pl.multiple_of(k * SUBLANES, SUBLANES), SUBLANES)
            xr, xi = a1r * xr - a1i * xi + sbuf[rows, 0:sw], a1r * xi + a1i * xr + sbuf[rows, sw:2 * sw]
            if store:
                sbuf[rows, 0:sw] = xr
                sbuf[rows, sw:2 * sw] = xi
            return xr, xi

        zero = jnp.zeros((SUBLANES, sw), F32)
        er, ei = lax.fori_loop(0, SCAN_ROWS, functools.partial(scan_step, store=False), (zero, zero))

        cr = _row_on_sublanes(carry_ref, j, 0)
        ci = _row_on_sublanes(carry_ref, j, 1)
        sr = jnp.where(row == 0, cr, pltpu.roll(er, 1, 0))
        si = jnp.where(row == 0, ci, pltpu.roll(ei, 1, 0))
        qr = _row_on_sublanes(pwr_ref, j, 1)
        qi = _row_on_sublanes(pwi_ref, j, 1)
        for k in (1, 2, 4):
            mr, mi = jnp.where(row >= k, qr, 0.0), jnp.where(row >= k, qi, 0.0)
            tr, ti = pltpu.roll(sr, k, 0), pltpu.roll(si, k, 0)
            sr, si = sr + mr * tr - mi * ti, si + mr * ti + mi * tr
            qr, qi = _cmul(qr, qi, qr, qi)

        lax.fori_loop(0, SCAN_ROWS, functools.partial(scan_step, store=True), (sr, si))
        carry_ref[j, 0:1, :] = sbuf[L - 1:L, 0:sw]
        carry_ref[j, 1:2, :] = sbuf[L - 1:L, sw:2 * sw]
        yj = jnp.dot(sbuf[...].astype(BF16), wc_ref[j], preferred_element_type=F32)
        uperm[j] = yj + d_ref[:, j * LANES:(j + 1) * LANES] * uj

    def unpermute(k, carry):
        for j in range(n_tiles):
            ynat[j, pl.ds(k, SUBLANES, stride=SCAN_ROWS), :] = \
                uperm[j, pl.ds(pl.multiple_of(k * SUBLANES, SUBLANES), SUBLANES), :]
        return carry

    lax.fori_loop(0, SCAN_ROWS, unpermute, 0)
    y = jax.nn.gelu(jnp.concatenate([ynat[j] for j in range(n_tiles)], axis=1))
    gate = jax.nn.sigmoid(jnp.dot(y.astype(BF16), wg_ref[...], preferred_element_type=F32))
    o_ref[:, POOL_WIDTH:] = (y * gate).astype(o_ref.dtype)


def _mixer0(u, batch, pool_w, pool_scale, wb, wc, pwr, pwi, d, w_glu):
    m, width = u.shape
    nj = SSM_WIDTH // LANES
    sw = GROUPS_PER_TILE * SSM_STATE
    chunks = T_PAD // L_SSM
    return pl.pallas_call(
        _mixer0_kernel,
        grid=(batch, chunks),
        in_specs=[pl.BlockSpec((L_SSM, width), lambda b, c: (b * chunks + c, 0)),
                  _const_spec((len(POOL_WINDOWS), POOL_GROUP, POOL_GROUP)), _const_spec((1, POOL_WIDTH)),
                  _const_spec((nj, LANES, 2 * sw)), _const_spec((nj, 2 * sw, LANES)),
                  _const_spec((nj, 2, sw)), _const_spec((nj, 2, sw)),
                  _const_spec((1, SSM_WIDTH)), _const_spec((SSM_WIDTH, SSM_WIDTH))],
        out_specs=pl.BlockSpec((L_SSM, width), lambda b, c: (b * chunks + c, 0)),
        out_shape=jax.ShapeDtypeStruct((m, width), BF16),
        scratch_shapes=[pltpu.VMEM((POOL_HALO + L_SSM, POOL_WIDTH), F32),
                        pltpu.VMEM((nj, L_SSM, LANES), F32),
                        pltpu.VMEM((nj, L_SSM, LANES), F32),
                        pltpu.VMEM((L_SSM, 2 * sw), F32),
                        pltpu.VMEM((nj, 2, sw), F32)],
        compiler_params=_params(("parallel", "arbitrary")),
        name="mixer0",
    )(u, pool_w, pool_scale, wb, wc, pwr, pwi, d, w_glu)


CHUNK_SHIFT = CHUNK.bit_length() - 1
assert 1 << CHUNK_SHIFT == CHUNK
KB = 512
KB_SHIFT = KB.bit_length() - 1
assert 1 << KB_SHIFT == KB
N_KB = -(-T_PAD // KB)
COUNT_ROWS = 64
PAIRS_PER_DOT = 2
ONES_ROWS = 16
VT_ROWS = HEAD_DIM + ONES_ROWS


def _chunk_id(t):
    return jnp.where(t < N_META, 0, ((t - N_META) >> CHUNK_SHIFT) + 1)


def _fold_rows(x, op):
    return op(x.reshape(x.shape[0] // SUBLANES, SUBLANES, x.shape[1]), axis=0)


def _stage_rows(src, dst):
    for kb in range(N_KB):
        n = min(KB, T_PAD - kb * KB)
        dst[kb, 0:n, :] = src[kb * KB:kb * KB + n, :].astype(dst.dtype)
        if n < KB:
            dst[kb, n:KB, :] = jnp.zeros((KB - n, dst.shape[2]), dst.dtype)


def _stage_values(src, dst):
    for kb in range(N_KB):
        n = min(KB, T_PAD - kb * KB)
        for kv in range(N_KV_HEADS):
            r0 = kv * VT_ROWS
            dst[kb, r0:r0 + HEAD_DIM, 0:n] = src[kv * HEAD_DIM:(kv + 1) * HEAD_DIM, kb * KB:kb * KB + n]
            if n < KB:
                dst[kb, r0:r0 + HEAD_DIM, n:KB] = jnp.zeros((HEAD_DIM, KB - n), dst.dtype)
            dst[kb, r0 + HEAD_DIM:r0 + VT_ROWS, :] = jnp.ones((ONES_ROWS, KB), dst.dtype)


def _for_key_blocks(n_kb, body):
    def pair(p, carry):
        body(2 * p)
        body(2 * p + 1)
        return carry

    lax.fori_loop(0, n_kb >> 1, pair, 0)

    @pl.when((n_kb & 1) == 1)
    def _():
        body(n_kb - 1)


def _dsa_kernel(qi_ref, wq_ref, q_ref, kw_ref, k_ref, vt_ref, o_ref,
                kie, kio, kbuf, vtbuf, key_buf, lg_buf, m_acc, o_acc):
    i = pl.program_id(1)
    tq = q_ref.shape[0]
    nt = (((1,), (1,)), ((), ()))

    @pl.when(i == 0)
    def _():
        kw = kw_ref[...]
        lane = lax.broadcasted_iota(jnp.int32, kw.shape, 1)
        _stage_rows(jnp.where(lane < IDX_DIM, kw, 0.0), kie)
        _stage_rows(jnp.where(lane >= IDX_DIM, pltpu.roll(kw, IDX_DIM, 1), 0.0), kio)
        _stage_rows(k_ref, kbuf)
        _stage_values(vt_ref, vtbuf)

    last_q = (i + 1) * tq - 1
    key_end = N_META + CHUNK * (((last_q - N_META) >> CHUNK_SHIFT) + 1)
    n_kb = jnp.minimum((key_end + KB - 1) >> KB_SHIFT, N_KB)

    w_t = wq_ref[...].T * ((IDX_HEADS ** -0.5) * (IDX_DIM ** -0.5))
    pairs = IDX_HEADS // 2
    qpos = i * tq + lax.broadcasted_iota(jnp.int32, (1, tq), 1)

    def index_block(kb):
        score = jnp.zeros((KB, tq), F32)
        for p0 in range(0, pairs, PAIRS_PER_DOT):
            qg = qi_ref[p0:p0 + PAIRS_PER_DOT].reshape(PAIRS_PER_DOT * tq, LANES)
            se = lax.dot_general(kie[kb], qg, nt, preferred_element_type=F32)
            so = lax.dot_general(kio[kb], qg, nt, preferred_element_type=F32)
            for c in range(PAIRS_PER_DOT):
                p = p0 + c
                we = w_t[IDX_DIM + 2 * p:IDX_DIM + 2 * p + 1, :]
                wo = w_t[IDX_DIM + 2 * p + 1:IDX_DIM + 2 * p + 2, :]
                score = score + we * jnp.maximum(se[:, c * tq:(c + 1) * tq], 0.0)
                score = score + wo * jnp.maximum(so[:, c * tq:(c + 1) * tq], 0.0)
        kpos = kb * KB + lax.broadcasted_iota(jnp.int32, (KB, 1), 0)
        allowed = (_chunk_id(kpos) <= _chunk_id(qpos)) & (kpos < T_REAL)
        bits = pltpu.bitcast(jnp.where(allowed, score, -jnp.inf), jnp.int32)
        key_buf[kb] = jnp.where(bits < 0, bits ^ 0x7FFFFFFF, bits)

    _for_key_blocks(n_kb, index_block)

    def count_ge(cand):
        def block(kb, acc):
            ind = jnp.where(key_buf[kb] >= cand, 1.0, 0.0)
            return acc + jnp.sum(ind.reshape(KB // COUNT_ROWS, COUNT_ROWS, tq), axis=0)
        acc = lax.fori_loop(0, n_kb, block, jnp.zeros((COUNT_ROWS, tq), F32))
        return jnp.sum(acc, axis=0, keepdims=True)

    c0 = count_ge(0)
    th = jnp.where(c0 >= TOPK, 0, INT_MIN).astype(jnp.int32)
    n_ge = jnp.where(c0 >= TOPK, c0, float(N_KB * KB))

    def search(j, carry):
        th, n_ge = carry
        cand = th | jnp.left_shift(jnp.int32(1), 30 - j)
        c = count_ge(cand)
        return jnp.where(c >= TOPK, cand, th), jnp.where(c >= TOPK, c, n_ge)

    th, n_ge = lax.fori_loop(0, 31, search, (th, n_ge))
    tied = (n_ge > TOPK) & (th > NEG_INF_KEY) & (qpos < T_REAL)
    th = jnp.maximum(th, NEG_INF_KEY + 1)

    @pl.when(jnp.max(tied.astype(jnp.int32)) > 0)
    def _():
        def count_eq(kb, acc):
            ind = jnp.where(key_buf[kb] == th, 1.0, 0.0)
            return acc + jnp.sum(ind.reshape(KB // COUNT_ROWS, COUNT_ROWS, tq), axis=0)
        n_eq = jnp.sum(lax.fori_loop(0, n_kb, count_eq, jnp.zeros((COUNT_ROWS, tq), F32)), axis=0, keepdims=True)
        keep = TOPK - (n_ge - n_eq)
        lower = (lax.broadcasted_iota(jnp.int32, (KB, KB), 0) >= lax.broadcasted_iota(jnp.int32, (KB, KB), 1))
        lower = jnp.where(lower, 1.0, 0.0).astype(BF16)

        def drop_late(kb, seen):
            key = key_buf[kb]
            eq = (key == th) & tied
            rank = seen + jnp.dot(lower, jnp.where(eq, 1.0, 0.0).astype(BF16), preferred_element_type=F32)
            key_buf[kb] = jnp.where(eq & (rank > keep), INT_MIN, key)
            return rank[KB - 1:KB, :]

        lax.fori_loop(0, n_kb, drop_late, jnp.zeros((1, tq), F32))

    qs = []
    for kv in range(N_KV_HEADS):
        h0 = kv * KV_GROUP
        qs.append(jnp.concatenate([q_ref[:, (h0 + g) * HEAD_DIM:(h0 + g + 1) * HEAD_DIM]
                                   for g in range(KV_GROUP)], axis=0))
    m_acc[...] = jnp.full(m_acc.shape, -jnp.inf, F32)
    o_acc[...] = jnp.zeros(o_acc.shape, F32)

    def logits_block(kb):
        sel = key_buf[kb] >= th
        for kv in range(N_KV_HEADS):
            kk = kbuf[kb, :, kv * HEAD_DIM:(kv + 1) * HEAD_DIM]
            lg = lax.dot_general(kk, qs[kv], nt, preferred_element_type=F32)
            lg = jnp.concatenate([jnp.where(sel, lg[:, g * tq:(g + 1) * tq], -jnp.inf)
                                  for g in range(KV_GROUP)], axis=1)
            lg_buf[kv, kb] = lg
            m_acc[kv] = jnp.maximum(m_acc[kv], _fold_rows(lg, jnp.max))

    _for_key_blocks(n_kb, logits_block)
    mx = [jnp.max(m_acc[kv], axis=0, keepdims=True) for kv in range(N_KV_HEADS)]

    def value_block(kb):
        for kv in range(N_KV_HEADS):
            e = jnp.exp2((lg_buf[kv, kb] - mx[kv]).astype(BF16))
            vt = vtbuf[kb, kv * VT_ROWS:(kv + 1) * VT_ROWS, :]
            o_acc[kv] += jnp.dot(vt, e, preferred_element_type=F32)

    _for_key_blocks(n_kb, value_block)
    for kv in range(N_KV_HEADS):
        out_t = o_acc[kv, 0:HEAD_DIM, :] / o_acc[kv, HEAD_DIM:HEAD_DIM + 1, :]
        for g in range(KV_GROUP):
            hd = kv * KV_GROUP + g
            o_ref[:, hd * HEAD_DIM:(hd + 1) * HEAD_DIM] = out_t[:, g * tq:(g + 1) * tq].T.astype(o_ref.dtype)


def _dsa(qi, kw, q, k, vt, batch):
    m = q.shape[0]
    nq = T_PAD // TQ
    pairs = qi.shape[0]
    cols = KV_GROUP * TQ
    return pl.pallas_call(
        _dsa_kernel,
        grid=(batch, nq),
        in_specs=[pl.BlockSpec((pairs, TQ, LANES), lambda b, i: (0, b * nq + i, 0)),
                  pl.BlockSpec((TQ, LANES), lambda b, i: (b * nq + i, 0)),
                  pl.BlockSpec((TQ, Q_W), lambda b, i: (b * nq + i, 0)),
                  pl.BlockSpec((T_PAD, LANES), lambda b, i: (b, 0)),
                  pl.BlockSpec((T_PAD, KV_W), lambda b, i: (b, 0)),
                  pl.BlockSpec((KV_W, T_PAD), lambda b, i: (0, b))],
        out_specs=pl.BlockSpec((TQ, Q_W), lambda b, i: (b * nq + i, 0)),
        out_shape=jax.ShapeDtypeStruct((m, Q_W), BF16),
        scratch_shapes=[pltpu.VMEM((N_KB, KB, LANES), BF16), pltpu.VMEM((N_KB, KB, LANES), BF16),
                        pltpu.VMEM((N_KB, KB, KV_W), BF16), pltpu.VMEM((N_KB, N_KV_HEADS * VT_ROWS, KB), BF16),
                        pltpu.VMEM((N_KB, KB, TQ), jnp.int32),
                        pltpu.VMEM((N_KV_HEADS, N_KB, KB, cols), F32),
                        pltpu.VMEM((N_KV_HEADS, SUBLANES, cols), F32),
                        pltpu.VMEM((N_KV_HEADS, VT_ROWS, cols), F32)],
        compiler_params=_params(("parallel", "arbitrary")),
        name="dsa",
    )(qi, kw, q, kw, k, vt)


def kernel(x, meta, norm_g, ab_w_in, pool_w, pool_scale, s5_lambda_re, s5_lambda_im, s5_log_dt,
           s5_b_re, s5_b_im, s5_c_re, s5_c_im, s5_d, s5_w_glu, ab_w_out, c_w_in, c_w_out, mlp_w1, mlp_w2):
    batch, seq, d = x.shape
    assert (seq + N_META, d) == (T_REAL, D_MODEL)
    depth = norm_g.shape[0]
    meta_b = jnp.broadcast_to(meta[None].astype(x.dtype), (batch, N_META, d))
    pad = jnp.zeros((batch, T_PAD - T_REAL, d), x.dtype)
    h = jnp.concatenate([meta_b, x, pad], axis=1).reshape(batch * T_PAD, d)

    rope128 = _rope_tables(HEAD_DIM, T_PAD, batch)
    rope64 = _rope_tables(IDX_DIM, T_PAD, batch)
    w1, w2 = mlp_w1.astype(BF16), mlp_w2.astype(BF16)

    for layer in range(depth):
        g = norm_g[layer]
        if layer % 2 == 0:
            e = layer // 2
            u = _norm_proj(h, g[0], ab_w_in[e].astype(BF16))
            bbr, bbi, pwr, pwi = _ssm_prep(s5_lambda_re[e], s5_lambda_im[e], s5_log_dt[e], s5_b_re[e], s5_b_im[e])
            nj = SSM_GROUPS // GROUPS_PER_TILE
            tile_major = lambda p: p.reshape(2, nj, GROUPS_PER_TILE * SSM_STATE).swapaxes(0, 1)
            wb = jnp.concatenate([_block_diag_in(bbr), _block_diag_in(bbi)], axis=-1).astype(BF16)
            wc = jnp.concatenate([_block_diag_out(s5_c_re[e]), -_block_diag_out(s5_c_im[e])], axis=1).astype(BF16)
            mix = _mixer0(u, batch, pool_w[e].astype(BF16), pool_scale[e].reshape(1, POOL_WIDTH), wb, wc,
                          tile_major(pwr), tile_major(pwi),
                          s5_d[e].reshape(1, SSM_WIDTH), s5_w_glu[e].astype(BF16))
            w_out = ab_w_out[e]
        else:
            o = layer // 2
            n_in = c_w_in.shape[-1]
            w_in = jnp.pad(c_w_in[o].astype(BF16), ((0, 0), (0, -n_in % LANES)))
            w_vt = w_in[:, Q_W + KV_W:Q_W + 2 * KV_W].T
            q, k, vt, qi, kw = _dsa_proj(h, g[0], w_in, w_vt, rope128, rope64)
            mix = _dsa(qi, kw, q, k, vt, batch)
            w_out = c_w_out[o]
        h = _matmul_norm_res(mix, w_out.astype(BF16), g[1], h)
        h = _mlp(h, g[2], g[3], w1, w2, layer)
    return h.reshape(batch, T_PAD, d)[:, N_META:T_REAL]
```

```python
import functools

import jax
import jax.numpy as jnp
from jax import lax
from jax.experimental import pallas as pl
from jax.experimental.pallas import tpu as pltpu

F32 = jnp.float32
BF16 = jnp.bfloat16

D_MODEL = 2048
CHUNK = 64
N_META = 16
POOL_WIDTH = 1024
POOL_WINDOWS = (2, 4, 8, 16)
POOL_GROUP = 256
POOL_HALO = 16
SSM_WIDTH = 1024
SSM_GROUP_CH = 16
SSM_GROUPS = 64
SSM_STATE = 64
GROUPS_PER_TILE = 8
HEAD_DIM = 128
N_HEADS = 16
N_KV_HEADS = 4
KV_GROUP = 4
IDX_HEADS = 16
IDX_DIM = 64
ROPE_THETA = 500000.0
ROPE_FRAC = 4
TOPK = 256
EPS = 1e-6
Q_W = N_HEADS * HEAD_DIM
KV_W = N_KV_HEADS * HEAD_DIM
IQ_W = IDX_HEADS * IDX_DIM

LANES = 128
SUBLANES = 8
VMEM_LIMIT = 56 * 1024 * 1024

T_REAL = N_META + 2048
T_PAD = 2176
TM = 512
TQ = 128
L_SSM = 544
SCAN_ROWS = L_SSM // 8
assert SCAN_ROWS * 8 == L_SSM
TILES_PER_SCAN = 2
TF = 1024

LOG2_E = 1.4426950408889634
INT_MIN = -2 ** 31
NEG_INF_KEY = -2139095041


def _params(sem):
    return pltpu.CompilerParams(dimension_semantics=sem, vmem_limit_bytes=VMEM_LIMIT)


def _rms(x, g):
    return x * lax.rsqrt(jnp.mean(x * x, axis=-1, keepdims=True) + EPS) * g


def _const_spec(shape):
    zeros = (0,) * len(shape)
    return pl.BlockSpec(shape, lambda *_: zeros, pipeline_mode=pl.Buffered(1))


def _norm_proj_kernel(h_ref, g_ref, w_ref, o_ref):
    a = _rms(h_ref[...], g_ref[...]).astype(BF16)
    o_ref[...] = jnp.dot(a, w_ref[...], preferred_element_type=F32)


def _norm_proj(h, g, w):
    m, d = h.shape
    n = w.shape[1]
    return pl.pallas_call(
        _norm_proj_kernel,
        grid=(m // TM,),
        in_specs=[pl.BlockSpec((TM, d), lambda i: (i, 0)), _const_spec((1, d)), _const_spec((d, n))],
        out_specs=pl.BlockSpec((TM, n), lambda i: (i, 0)),
        out_shape=jax.ShapeDtypeStruct((m, n), F32),
        compiler_params=_params(("parallel",)),
        name="norm_proj",
    )(h, g.reshape(1, d), w)


def _rope_tables(head_dim, rows_per_batch, batch):
    r = head_dim // ROPE_FRAC
    half = r // 2
    inv = ROPE_THETA ** (-jnp.arange(half, dtype=F32) / half)
    ang = jnp.arange(rows_per_batch, dtype=F32)[:, None] * inv[None, :]
    cos, sin = jnp.cos(ang), jnp.sin(ang)
    ones = jnp.ones((rows_per_batch, head_dim - r), F32)
    zeros_h = jnp.zeros((rows_per_batch, half), F32)
    zeros_t = jnp.zeros((rows_per_batch, head_dim - r), F32)
    c = jnp.concatenate([cos, cos, ones], axis=1)
    s_up = jnp.concatenate([zeros_h, sin, zeros_t], axis=1)
    s_dn = jnp.concatenate([-sin, zeros_h, zeros_t], axis=1)
    reps = (batch, LANES // head_dim)
    return half, jnp.tile(c, reps), jnp.tile(s_up, reps), jnp.tile(s_dn, reps)


def _rope(ys, half, c, s_up, s_dn):
    return ys * c + pltpu.roll(ys, half, 1) * s_up + pltpu.roll(ys, LANES - half, 1) * s_dn


def _dsa_proj_kernel(h_ref, g_ref, w_ref, wvt_ref, c128_ref, u128_ref, d128_ref, c64_ref, u64_ref, d64_ref,
                     q_ref, k_ref, vt_ref, qi_ref, kw_ref, *, half128, half64):
    a = _rms(h_ref[...], g_ref[...]).astype(BF16)
    t128 = (half128, c128_ref[...], u128_ref[...], d128_ref[...])
    t64 = (half64, c64_ref[...], u64_ref[...], d64_ref[...])
    c0, c1, c2, c3 = Q_W, Q_W + KV_W, Q_W + 2 * KV_W, Q_W + 2 * KV_W + IQ_W

    def slabs(lo, hi):
        y = jnp.dot(a, w_ref[:, lo:hi], preferred_element_type=F32)
        return [y[:, s * LANES:(s + 1) * LANES] for s in range((hi - lo) // LANES)]

    for s, ys in enumerate(slabs(0, c0)):
        q_ref[:, s * LANES:(s + 1) * LANES] = (_rope(ys, *t128) * (HEAD_DIM ** -0.5 * LOG2_E)).astype(q_ref.dtype)
    for s, ys in enumerate(slabs(c0, c1)):
        k_ref[:, s * LANES:(s + 1) * LANES] = _rope(ys, *t128).astype(k_ref.dtype)
    vt_ref[...] = lax.dot_general(wvt_ref[...], a, (((1,), (1,)), ((), ())),
                                  preferred_element_type=F32).astype(vt_ref.dtype)
    for s, ys in enumerate(slabs(c2, c3)):
        qi_ref[s] = _rope(ys, *t64).astype(qi_ref.dtype)
    (ys,) = slabs(c3, c3 + LANES)
    lane = lax.broadcasted_iota(jnp.int32, ys.shape, 1)
    kw_ref[...] = jnp.where(lane < IDX_DIM, _rope(ys, *t64), ys)


def _dsa_proj(h, g, w, w_vt, rope128, rope64):
    m, d = h.shape
    n = w.shape[1]
    pairs = IQ_W // LANES
    row = lambda width: pl.BlockSpec((TM, width), lambda i: (i, 0))
    return pl.pallas_call(
        functools.partial(_dsa_proj_kernel, half128=rope128[0], half64=rope64[0]),
        grid=(m // TM,),
        in_specs=[row(d), _const_spec((1, d)), _const_spec((d, n)), _const_spec((KV_W, d))] + [row(LANES)] * 6,
        out_specs=[row(Q_W), row(KV_W), pl.BlockSpec((KV_W, TM), lambda i: (0, i)),
                   pl.BlockSpec((pairs, TM, LANES), lambda i: (0, i, 0)), row(LANES)],
        out_shape=[jax.ShapeDtypeStruct((m, Q_W), BF16), jax.ShapeDtypeStruct((m, KV_W), BF16),
                   jax.ShapeDtypeStruct((KV_W, m), BF16), jax.ShapeDtypeStruct((pairs, m, LANES), BF16),
                   jax.ShapeDtypeStruct((m, LANES), F32)],
        compiler_params=_params(("parallel",)),
        name="dsa_proj",
    )(h, g.reshape(1, d), w, w_vt, *rope128[1:], *rope64[1:])


def _matmul_norm_res_kernel(a_ref, w_ref, g_ref, r_ref, o_ref):
    y = jnp.dot(a_ref[...], w_ref[...], preferred_element_type=F32)
    o_ref[...] = r_ref[...] + _rms(y, g_ref[...])


def _matmul_norm_res(a, w, g, res):
    m, k = a.shape
    n = w.shape[1]
    return pl.pallas_call(
        _matmul_norm_res_kernel,
        grid=(m // TM,),
        in_specs=[pl.BlockSpec((TM, k), lambda i: (i, 0)), _const_spec((k, n)),
                  _const_spec((1, n)), pl.BlockSpec((TM, n), lambda i: (i, 0))],
        out_specs=pl.BlockSpec((TM, n), lambda i: (i, 0)),
        out_shape=jax.ShapeDtypeStruct((m, n), F32),
        compiler_params=_params(("parallel",)),
        name="matmul_norm_res",
    )(a, w, g.reshape(1, n), res)


def _mlp_kernel(h_ref, g_in_ref, g_out_ref, w1_ref, w2_ref, o_ref, a_ref, acc_ref):
    f = pl.program_id(1)

    @pl.when(f == 0)
    def _():
        a_ref[...] = _rms(h_ref[...], g_in_ref[...]).astype(BF16)
        acc_ref[...] = jnp.zeros_like(acc_ref)

    z = jnp.dot(a_ref[...], w1_ref[...], preferred_element_type=F32)
    z = jnp.square(jnp.maximum(z, 0.0)).astype(BF16)
    acc_ref[...] += jnp.dot(z, w2_ref[...], preferred_element_type=F32)

    @pl.when(f == pl.num_programs(1) - 1)
    def _():
        o_ref[...] = h_ref[...] + _rms(acc_ref[...], g_out_ref[...])


def _mlp(h, g_in, g_out, w1, w2, layer):
    m, d = h.shape
    ff = w1.shape[2]
    return pl.pallas_call(
        _mlp_kernel,
        grid=(m // TM, ff // TF),
        in_specs=[pl.BlockSpec((TM, d), lambda i, f: (i, 0)),
                  _const_spec((1, d)), _const_spec((1, d)),
                  pl.BlockSpec((None, d, TF), lambda i, f: (layer, 0, f)),
                  pl.BlockSpec((None, TF, d), lambda i, f: (layer, f, 0))],
        out_specs=pl.BlockSpec((TM, d), lambda i, f: (i, 0)),
        out_shape=jax.ShapeDtypeStruct((m, d), F32),
        scratch_shapes=[pltpu.VMEM((TM, d), BF16), pltpu.VMEM((TM, d), F32)],
        compiler_params=_params(("parallel", "arbitrary")),
        name="mlp",
    )(h, g_in.reshape(1, d), g_out.reshape(1, d), w1, w2)


def _cmul(ar, ai, br, bi):
    return ar * br - ai * bi, ar * bi + ai * br


def _discretise(ar, ai, log_dt):
    dt = jnp.exp(log_dt)
    mag = jnp.exp(dt * ar)
    return mag * jnp.cos(dt * ai), mag * jnp.sin(dt * ai)


def _ssm_prep_kernel(lre_ref, lim_ref, ldt_ref, lre2_ref, lim2_ref, ldt2_ref, bre_ref, bim_ref,
                     bbr_ref, bbi_ref, pwr_ref, pwi_ref):
    ar, ai = lre_ref[...], lim_ref[...]
    abr, abi = _discretise(ar, ai, ldt_ref[...])
    den = ar * ar + ai * ai
    zr, zi = abr - 1.0, abi
    fr = (zr * ar + zi * ai) / den
    fi = (zi * ar - zr * ai) / den
    br, bi = bre_ref[...], bim_ref[...]
    bbr_ref[...] = fr * br - fi * bi
    bbi_ref[...] = fr * bi + fi * br
    a1r, a1i = _discretise(lre2_ref[...], lim2_ref[...], ldt2_ref[...])
    pr, pi = a1r, a1i
    for _ in range(SCAN_ROWS - 1):
        pr, pi = _cmul(pr, pi, a1r, a1i)
    pwr_ref[0], pwi_ref[0] = a1r, a1i
    pwr_ref[1], pwi_ref[1] = pr, pi


def _ssm_prep(lam_re, lam_im, log_dt, b_re, b_im):
    g, n = lam_re.shape
    c = b_re.shape[-1]
    bt = lambda b: jnp.swapaxes(b, 1, 2)
    full = lambda *s: pl.BlockSpec(s, lambda: (0,) * len(s))
    return pl.pallas_call(
        _ssm_prep_kernel,
        in_specs=[full(g, 1, n), full(g, 1, n), full(g, 1, 1), full(g, n), full(g, n), full(g, 1),
                  full(g, c, n), full(g, c, n)],
        out_specs=[full(g, c, n), full(g, c, n), full(2, g, n), full(2, g, n)],
        out_shape=[jax.ShapeDtypeStruct((g, c, n), F32)] * 2 + [jax.ShapeDtypeStruct((2, g, n), F32)] * 2,
        name="ssm_prep",
    )(lam_re.reshape(g, 1, n), lam_im.reshape(g, 1, n), log_dt.reshape(g, 1, 1), lam_re, lam_im,
      log_dt.reshape(g, 1), bt(b_re), bt(b_im))


def _block_diag_in(bb):
    g, c, n = bb.shape
    r = bb.reshape(g // GROUPS_PER_TILE, GROUPS_PER_TILE, c, n)
    eye = jnp.eye(GROUPS_PER_TILE, dtype=bb.dtype)
    return jnp.einsum('jgcn,gh->jgchn', r, eye).reshape(g // GROUPS_PER_TILE, GROUPS_PER_TILE * c, GROUPS_PER_TILE * n)


def _block_diag_out(cc):
    g, c, n = cc.shape
    r = cc.reshape(g // GROUPS_PER_TILE, GROUPS_PER_TILE, c, n)
    eye = jnp.eye(GROUPS_PER_TILE, dtype=cc.dtype)
    return jnp.einsum('jgcn,gh->jgnhc', r, eye).reshape(g // GROUPS_PER_TILE, GROUPS_PER_TILE * n, GROUPS_PER_TILE * c)


def _row_on_sublanes(ref, j, k):
    return jnp.broadcast_to(ref[j, pl.ds(k, 1), :], (SUBLANES, ref.shape[2]))


def _mixer0_kernel(u_ref, pw_ref, ps_ref, wb_ref, wc_ref, pwr_ref, pwi_ref, d_ref, wg_ref, o_ref,
                   pbuf, uperm, ynat, sbuf, carry_ref):
    c = pl.program_id(1)
    L = u_ref.shape[0]
    sw = GROUPS_PER_TILE * SSM_STATE

    @pl.when(c == 0)
    def _():
        pbuf[0:POOL_HALO, :] = jnp.zeros((POOL_HALO, POOL_WIDTH), F32)
        carry_ref[...] = jnp.zeros_like(carry_ref)

    pbuf[POOL_HALO:POOL_HALO + L, :] = u_ref[:, 0:POOL_WIDTH]
    tpos = c * L + lax.broadcasted_iota(jnp.int32, (L, 1), 0)
    for g, win in enumerate(POOL_WINDOWS):
        sl = slice(g * POOL_GROUP, (g + 1) * POOL_GROUP)
        x = u_ref[:, sl]
        s = x
        for k in range(1, win):
            s = s + pbuf[POOL_HALO - k:POOL_HALO - k + L, sl]
        cnt = jnp.minimum(tpos + 1, win).astype(F32)
        diff = (s / cnt - x).astype(BF16)
        y = jnp.dot(diff, pw_ref[g], preferred_element_type=F32) * ps_ref[:, sl]
        o_ref[:, sl] = y.astype(o_ref.dtype)
    pbuf[0:POOL_HALO, :] = pbuf[L:L + POOL_HALO, :]

    n_tiles = SSM_WIDTH // LANES
    for j in range(n_tiles):
        ynat[j] = u_ref[:, POOL_WIDTH + j * LANES:POOL_WIDTH + (j + 1) * LANES]

    def permute(k, carry):
        for j in range(n_tiles):
            uperm[j, pl.ds(pl.multiple_of(k * SUBLANES, SUBLANES), SUBLANES), :] = \
                ynat[j, pl.ds(k, SUBLANES, stride=SCAN_ROWS), :]
        return carry

    lax.fori_loop(0, SCAN_ROWS, permute, 0)
    row = lax.broadcasted_iota(jnp.int32, (SUBLANES, sw), 0)
    for j0 in range(0, n_tiles, TILES_PER_SCAN):
        tiles = range(j0, j0 + TILES_PER_SCAN)
        for j in tiles:
            sbuf[j - j0] = jnp.dot(uperm[j].astype(BF16), wb_ref[j], preferred_element_type=F32)
        a1 = [(_row_on_sublanes(pwr_ref, j, 0), _row_on_sublanes(pwi_ref, j, 0)) for j in tiles]

        def scan_step(k, xs, store):
            rows = pl.ds(pl.multiple_of(k * SUBLANES, SUBLANES), SUBLANES)
            out = []
            for c, ((a1r, a1i), (xr, xi)) in enumerate(zip(a1, xs)):
                xr, xi = (a1r * xr - a1i * xi + sbuf[c, rows, 0:sw], a1r * xi + a1i * xr + sbuf[c, rows, sw:2 * sw])
                if store:
                    sbuf[c, rows, 0:sw] = xr
                    sbuf[c, rows, sw:2 * sw] = xi
                out.append((xr, xi))
            return tuple(out)

        zero = jnp.zeros((SUBLANES, sw), F32)
        ends = lax.fori_loop(0, SCAN_ROWS, functools.partial(scan_step, store=False),
                             tuple((zero, zero) for _ in tiles))

        starts = []
        for j, (er, ei) in zip(tiles, ends):
            cr = _row_on_sublanes(carry_ref, j, 0)
            ci = _row_on_sublanes(carry_ref, j, 1)
            sr = jnp.where(row == 0, cr, pltpu.roll(er, 1, 0))
            si = jnp.where(row == 0, ci, pltpu.roll(ei, 1, 0))
            qr = _row_on_sublanes(pwr_ref, j, 1)
            qi = _row_on_sublanes(pwi_ref, j, 1)
            for k in (1, 2, 4):
                mr, mi = jnp.where(row >= k, qr, 0.0), jnp.where(row >= k, qi, 0.0)
                tr, ti = pltpu.roll(sr, k, 0), pltpu.roll(si, k, 0)
                sr, si = sr + mr * tr - mi * ti, si + mr * ti + mi * tr
                qr, qi = _cmul(qr, qi, qr, qi)
            starts.append((sr, si))

        lax.fori_loop(0, SCAN_ROWS, functools.partial(scan_step, store=True), tuple(starts))
        for j in tiles:
            c = j - j0
            carry_ref[j, 0:1, :] = sbuf[c, L - 1:L, 0:sw]
            carry_ref[j, 1:2, :] = sbuf[c, L - 1:L, sw:2 * sw]
            yj = jnp.dot(sbuf[c].astype(BF16), wc_ref[j], preferred_element_type=F32)
            uperm[j] = yj + d_ref[:, j * LANES:(j + 1) * LANES] * uperm[j]

    def unpermute(k, carry):
        for j in range(n_tiles):
            ynat[j, pl.ds(k, SUBLANES, stride=SCAN_ROWS), :] = \
                uperm[j, pl.ds(pl.multiple_of(k * SUBLANES, SUBLANES), SUBLANES), :]
        return carry

    lax.fori_loop(0, SCAN_ROWS, unpermute, 0)
    y = jax.nn.gelu(jnp.concatenate([ynat[j] for j in range(n_tiles)], axis=1))
    gate = jax.nn.sigmoid(jnp.dot(y.astype(BF16), wg_ref[...], preferred_element_type=F32))
    o_ref[:, POOL_WIDTH:] = (y * gate).astype(o_ref.dtype)


def _mixer0(u, batch, pool_w, pool_scale, wb, wc, pwr, pwi, d, w_glu):
    m, width = u.shape
    nj = SSM_WIDTH // LANES
    sw = GROUPS_PER_TILE * SSM_STATE
    chunks = T_PAD // L_SSM
    return pl.pallas_call(
        _mixer0_kernel,
        grid=(batch, chunks),
        in_specs=[pl.BlockSpec((L_SSM, width), lambda b, c: (b * chunks + c, 0)),
                  _const_spec((len(POOL_WINDOWS), POOL_GROUP, POOL_GROUP)), _const_spec((1, POOL_WIDTH)),
                  _const_spec((nj, LANES, 2 * sw)), _const_spec((nj, 2 * sw, LANES)),
                  _const_spec((nj, 2, sw)), _const_spec((nj, 2, sw)),
                  _const_spec((1, SSM_WIDTH)), _const_spec((SSM_WIDTH, SSM_WIDTH))],
        out_specs=pl.BlockSpec((L_SSM, width), lambda b, c: (b * chunks + c, 0)),
        out_shape=jax.ShapeDtypeStruct((m, width), BF16),
        scratch_shapes=[pltpu.VMEM((POOL_HALO + L_SSM, POOL_WIDTH), F32),
                        pltpu.VMEM((nj, L_SSM, LANES), F32),
                        pltpu.VMEM((nj, L_SSM, LANES), F32),
                        pltpu.VMEM((TILES_PER_SCAN, L_SSM, 2 * sw), F32),
                        pltpu.VMEM((nj, 2, sw), F32)],
        compiler_params=_params(("parallel", "arbitrary")),
        name="mixer0",
    )(u, pool_w, pool_scale, wb, wc, pwr, pwi, d, w_glu)


CHUNK_SHIFT = CHUNK.bit_length() - 1
assert 1 << CHUNK_SHIFT == CHUNK
KB = 512
KB_SHIFT = KB.bit_length() - 1
assert 1 << KB_SHIFT == KB
N_KB = -(-T_PAD // KB)
COUNT_ROWS = 64
PAIRS_PER_DOT = 2
ONES_ROWS = 16
VT_ROWS = HEAD_DIM + ONES_ROWS


def _chunk_id(t):
    return jnp.where(t < N_META, 0, ((t - N_META) >> CHUNK_SHIFT) + 1)


def _fold_rows(x, op):
    return op(x.reshape(x.shape[0] // SUBLANES, SUBLANES, x.shape[1]), axis=0)


def _stage_rows(src, dst):
    for kb in range(N_KB):
        n = min(KB, T_PAD - kb * KB)
        dst[kb, 0:n, :] = src[kb * KB:kb * KB + n, :].astype(dst.dtype)
        if n < KB:
            dst[kb, n:KB, :] = jnp.zeros((KB - n, dst.shape[2]), dst.dtype)


def _stage_values(src, dst):
    for kb in range(N_KB):
        n = min(KB, T_PAD - kb * KB)
        for kv in range(N_KV_HEADS):
            r0 = kv * VT_ROWS
            dst[kb, r0:r0 + HEAD_DIM, 0:n] = src[kv * HEAD_DIM:(kv + 1) * HEAD_DIM, kb * KB:kb * KB + n]
            if n < KB:
                dst[kb, r0:r0 + HEAD_DIM, n:KB] = jnp.zeros((HEAD_DIM, KB - n), dst.dtype)
            dst[kb, r0 + HEAD_DIM:r0 + VT_ROWS, :] = jnp.ones((ONES_ROWS, KB), dst.dtype)


def _for_key_blocks(n_kb, body):
    def pair(p, carry):
        body(2 * p)
        body(2 * p + 1)
        return carry

    lax.fori_loop(0, n_kb >> 1, pair, 0)

    @pl.when((n_kb & 1) == 1)
    def _():
        body(n_kb - 1)


def _dsa_kernel(qi_ref, wq_ref, q_ref, kw_ref, k_ref, vt_ref, o_ref,
                kie, kio, kbuf, vtbuf, key_buf, lg_buf, m_acc, o_acc):
    i = pl.program_id(1)
    tq = q_ref.shape[0]
    nt = (((1,), (1,)), ((), ()))

    @pl.when(i == 0)
    def _():
        kw = kw_ref[...]
        lane = lax.broadcasted_iota(jnp.int32, kw.shape, 1)
        _stage_rows(jnp.where(lane < IDX_DIM, kw, 0.0), kie)
        _stage_rows(jnp.where(lane >= IDX_DIM, pltpu.roll(kw, IDX_DIM, 1), 0.0), kio)
        _stage_rows(k_ref, kbuf)
        _stage_values(vt_ref, vtbuf)

    last_q = (i + 1) * tq - 1
    key_end = N_META + CHUNK * (((last_q - N_META) >> CHUNK_SHIFT) + 1)
    n_kb = jnp.minimum((key_end + KB - 1) >> KB_SHIFT, N_KB)

    w_t = wq_ref[...].T * ((IDX_HEADS ** -0.5) * (IDX_DIM ** -0.5))
    pairs = IDX_HEADS // 2
    qpos = i * tq + lax.broadcasted_iota(jnp.int32, (1, tq), 1)

    def index_block(kb):
        score = jnp.zeros((KB, tq), F32)
        for p0 in range(0, pairs, PAIRS_PER_DOT):
            qg = qi_ref[p0:p0 + PAIRS_PER_DOT].reshape(PAIRS_PER_DOT * tq, LANES)
            se = lax.dot_general(kie[kb], qg, nt, preferred_element_type=F32)
            so = lax.dot_general(kio[kb], qg, nt, preferred_element_type=F32)
            for c in range(PAIRS_PER_DOT):
                p = p0 + c
                we = w_t[IDX_DIM + 2 * p:IDX_DIM + 2 * p + 1, :]
                wo = w_t[IDX_DIM + 2 * p + 1:IDX_DIM + 2 * p + 2, :]
                score = score + we * jnp.maximum(se[:, c * tq:(c + 1) * tq], 0.0)
                score = score + wo * jnp.maximum(so[:, c * tq:(c + 1) * tq], 0.0)
        kpos = kb * KB + lax.broadcasted_iota(jnp.int32, (KB, 1), 0)
        allowed = (_chunk_id(kpos) <= _chunk_id(qpos)) & (kpos < T_REAL)
        bits = pltpu.bitcast(jnp.where(allowed, score, -jnp.inf), jnp.int32)
        key_buf[kb] = jnp.where(bits < 0, bits ^ 0x7FFFFFFF, bits)

    _for_key_blocks(n_kb, index_block)

    def count_ge(cand):
        def block(kb, acc):
            ind = jnp.where(key_buf[kb] >= cand, 1.0, 0.0)
            return acc + jnp.sum(ind.reshape(KB // COUNT_ROWS, COUNT_ROWS, tq), axis=0)
        acc = lax.fori_loop(0, n_kb, block, jnp.zeros((COUNT_ROWS, tq), F32))
        return jnp.sum(acc, axis=0, keepdims=True)

    c0 = count_ge(0)
    th = jnp.where(c0 >= TOPK, 0, INT_MIN).astype(jnp.int32)
    n_ge = jnp.where(c0 >= TOPK, c0, float(N_KB * KB))

    def search(j, carry):
        th, n_ge = carry
        cand = th | jnp.left_shift(jnp.int32(1), 30 - j)
        c = count_ge(cand)
        return jnp.where(c >= TOPK, cand, th), jnp.where(c >= TOPK, c, n_ge)

    th, n_ge = lax.fori_loop(0, 31, search, (th, n_ge))
    tied = (n_ge > TOPK) & (th > NEG_INF_KEY) & (qpos < T_REAL)
    th = jnp.maximum(th, NEG_INF_KEY + 1)

    @pl.when(jnp.max(tied.astype(jnp.int32)) > 0)
    def _():
        def count_eq(kb, acc):
            ind = jnp.where(key_buf[kb] == th, 1.0, 0.0)
            return acc + jnp.sum(ind.reshape(KB // COUNT_ROWS, COUNT_ROWS, tq), axis=0)
        n_eq = jnp.sum(lax.fori_loop(0, n_kb, count_eq, jnp.zeros((COUNT_ROWS, tq), F32)), axis=0, keepdims=True)
        keep = TOPK - (n_ge - n_eq)
        lower = (lax.broadcasted_iota(jnp.int32, (KB, KB), 0) >= lax.broadcasted_iota(jnp.int32, (KB, KB), 1))
        lower = jnp.where(lower, 1.0, 0.0).astype(BF16)

        def drop_late(kb, seen):
            key = key_buf[kb]
            eq = (key == th) & tied
            rank = seen + jnp.dot(lower, jnp.where(eq, 1.0, 0.0).astype(BF16), preferred_element_type=F32)
            key_buf[kb] = jnp.where(eq & (rank > keep), INT_MIN, key)
            return rank[KB - 1:KB, :]

        lax.fori_loop(0, n_kb, drop_late, jnp.zeros((1, tq), F32))

    qs = []
    for kv in range(N_KV_HEADS):
        h0 = kv * KV_GROUP
        qs.append(jnp.concatenate([q_ref[:, (h0 + g) * HEAD_DIM:(h0 + g + 1) * HEAD_DIM]
                                   for g in range(KV_GROUP)], axis=0))
    m_acc[...] = jnp.full(m_acc.shape, -jnp.inf, F32)
    o_acc[...] = jnp.zeros(o_acc.shape, F32)

    def logits_block(kb):
        sel = key_buf[kb] >= th
        for kv in range(N_KV_HEADS):
            kk = kbuf[kb, :, kv * HEAD_DIM:(kv + 1) * HEAD_DIM]
            lg = lax.dot_general(kk, qs[kv], nt, preferred_element_type=F32)
            lg = jnp.concatenate([jnp.where(sel, lg[:, g * tq:(g + 1) * tq], -jnp.inf)
                                  for g in range(KV_GROUP)], axis=1)
            lg_buf[kv, kb] = lg
            m_acc[kv] = jnp.maximum(m_acc[kv], _fold_rows(lg, jnp.max))

    _for_key_blocks(n_kb, logits_block)
    mx = [jnp.max(m_acc[kv], axis=0, keepdims=True) for kv in range(N_KV_HEADS)]

    def value_block(kb):
        for kv in range(N_KV_HEADS):
            e = jnp.exp2((lg_buf[kv, kb] - mx[kv]).astype(BF16))
            vt = vtbuf[kb, kv * VT_ROWS:(kv + 1) * VT_ROWS, :]
            o_acc[kv] += jnp.dot(vt, e, preferred_element_type=F32)

    _for_key_blocks(n_kb, value_block)
    for kv in range(N_KV_HEADS):
        out_t = o_acc[kv, 0:HEAD_DIM, :] / o_acc[kv, HEAD_DIM:HEAD_DIM + 1, :]
        for g in range(KV_GROUP):
            hd = kv * KV_GROUP + g
            o_ref[:, hd * HEAD_DIM:(hd + 1) * HEAD_DIM] = out_t[:, g * tq:(g + 1) * tq].T.astype(o_ref.dtype)


def _dsa(qi, kw, q, k, vt, batch):
    m = q.shape[0]
    nq = T_PAD // TQ
    pairs = qi.shape[0]
    cols = KV_GROUP * TQ
    return pl.pallas_call(
        _dsa_kernel,
        grid=(batch, nq),
        in_specs=[pl.BlockSpec((pairs, TQ, LANES), lambda b, i: (0, b * nq + i, 0)),
                  pl.BlockSpec((TQ, LANES), lambda b, i: (b * nq + i, 0)),
                  pl.BlockSpec((TQ, Q_W), lambda b, i: (b * nq + i, 0)),
                  pl.BlockSpec((T_PAD, LANES), lambda b, i: (b, 0)),
                  pl.BlockSpec((T_PAD, KV_W), lambda b, i: (b, 0)),
                  pl.BlockSpec((KV_W, T_PAD), lambda b, i: (0, b))],
        out_specs=pl.BlockSpec((TQ, Q_W), lambda b, i: (b * nq + i, 0)),
        out_shape=jax.ShapeDtypeStruct((m, Q_W), BF16),
        scratch_shapes=[pltpu.VMEM((N_KB, KB, LANES), BF16), pltpu.VMEM((N_KB, KB, LANES), BF16),
                        pltpu.VMEM((N_KB, KB, KV_W), BF16), pltpu.VMEM((N_KB, N_KV_HEADS * VT_ROWS, KB), BF16),
                        pltpu.VMEM((N_KB, KB, TQ), jnp.int32),
                        pltpu.VMEM((N_KV_HEADS, N_KB, KB, cols), F32),
                        pltpu.VMEM((N_KV_HEADS, SUBLANES, cols), F32),
                        pltpu.VMEM((N_KV_HEADS, VT_ROWS, cols), F32)],
        compiler_params=_params(("parallel", "arbitrary")),
        name="dsa",
    )(qi, kw, q, kw, k, vt)


def kernel(x, meta, norm_g, ab_w_in, pool_w, pool_scale, s5_lambda_re, s5_lambda_im, s5_log_dt,
           s5_b_re, s5_b_im, s5_c_re, s5_c_im, s5_d, s5_w_glu, ab_w_out, c_w_in, c_w_out, mlp_w1, mlp_w2):
    batch, seq, d = x.shape
    assert (seq + N_META, d) == (T_REAL, D_MODEL)
    depth = norm_g.shape[0]
    meta_b = jnp.broadcast_to(meta[None].astype(x.dtype), (batch, N_META, d))
    pad = jnp.zeros((batch, T_PAD - T_REAL, d), x.dtype)
    h = jnp.concatenate([meta_b, x, pad], axis=1).reshape(batch * T_PAD, d)

    rope128 = _rope_tables(HEAD_DIM, T_PAD, batch)
    rope64 = _rope_tables(IDX_DIM, T_PAD, batch)
    w1, w2 = mlp_w1.astype(BF16), mlp_w2.astype(BF16)

    for layer in range(depth):
        g = norm_g[layer]
        if layer % 2 == 0:
            e = layer // 2
            u = _norm_proj(h, g[0], ab_w_in[e].astype(BF16))
            bbr, bbi, pwr, pwi = _ssm_prep(s5_lambda_re[e], s5_lambda_im[e], s5_log_dt[e], s5_b_re[e], s5_b_im[e])
            nj = SSM_GROUPS // GROUPS_PER_TILE
            tile_major = lambda p: p.reshape(2, nj, GROUPS_PER_TILE * SSM_STATE).swapaxes(0, 1)
            wb = jnp.concatenate([_block_diag_in(bbr), _block_diag_in(bbi)], axis=-1).astype(BF16)
            wc = jnp.concatenate([_block_diag_out(s5_c_re[e]), -_block_diag_out(s5_c_im[e])], axis=1).astype(BF16)
            mix = _mixer0(u, batch, pool_w[e].astype(BF16), pool_scale[e].reshape(1, POOL_WIDTH), wb, wc,
                          tile_major(pwr), tile_major(pwi),
                          s5_d[e].reshape(1, SSM_WIDTH), s5_w_glu[e].astype(BF16))
            w_out = ab_w_out[e]
        else:
            o = layer // 2
            n_in = c_w_in.shape[-1]
            w_in = jnp.pad(c_w_in[o], ((0, 0), (0, -n_in % LANES))).astype(BF16)
            w_vt = w_in[:, Q_W + KV_W:Q_W + 2 * KV_W].T
            q, k, vt, qi, kw = _dsa_proj(h, g[0], w_in, w_vt, rope128, rope64)
            mix = _dsa(qi, kw, q, k, vt, batch)
            w_out = c_w_out[o]
        h = _matmul_norm_res(mix, w_out.astype(BF16), g[1], h)
        h = _mlp(h, g[2], g[3], w1, w2, layer)
    return h.reshape(batch, T_PAD, d)[:, N_META:T_REAL]
```

```python
import functools

import jax
import jax.numpy as jnp
from jax import lax
from jax.experimental import pallas as pl
from jax.experimental.pallas import tpu as pltpu

F32 = jnp.float32
BF16 = jnp.bfloat16

D_MODEL = 2048
CHUNK = 64
N_META = 16
POOL_WIDTH = 1024
POOL_WINDOWS = (2, 4, 8, 16)
POOL_GROUP = 256
POOL_HALO = 16
SSM_WIDTH = 1024
SSM_GROUP_CH = 16
SSM_GROUPS = 64
SSM_STATE = 64
GROUPS_PER_TILE = 8
HEAD_DIM = 128
N_HEADS = 16
N_KV_HEADS = 4
KV_GROUP = 4
IDX_HEADS = 16
IDX_DIM = 64
ROPE_THETA = 500000.0
ROPE_FRAC = 4
TOPK = 256
EPS = 1e-6
Q_W = N_HEADS * HEAD_DIM
KV_W = N_KV_HEADS * HEAD_DIM
IQ_W = IDX_HEADS * IDX_DIM

LANES = 128
SUBLANES = 8
VMEM_LIMIT = 56 * 1024 * 1024

T_REAL = N_META + 2048
T_PAD = 2176
PAD_FRONT = T_PAD - T_REAL
TM = 512
TQ = 128
L_SSM = 544
SCAN_ROWS = L_SSM // 8
assert SCAN_ROWS * 8 == L_SSM
TILES_PER_SCAN = 2
TF = 1024

LOG2_E = 1.4426950408889634
INT_MIN = -2 ** 31
NEG_INF_KEY = -2139095041


def _params(sem):
    return pltpu.CompilerParams(dimension_semantics=sem, vmem_limit_bytes=VMEM_LIMIT)


def _rms(x, g):
    return x * lax.rsqrt(jnp.mean(x * x, axis=-1, keepdims=True) + EPS) * g


def _const_spec(shape):
    zeros = (0,) * len(shape)
    return pl.BlockSpec(shape, lambda *_: zeros, pipeline_mode=pl.Buffered(1))


def _norm_proj_kernel(h_ref, g_ref, w_ref, o_ref):
    a = _rms(h_ref[...], g_ref[...]).astype(BF16)
    o_ref[...] = jnp.dot(a, w_ref[...], preferred_element_type=F32)


def _norm_proj(h, g, w):
    m, d = h.shape
    n = w.shape[1]
    return pl.pallas_call(
        _norm_proj_kernel,
        grid=(m // TM,),
        in_specs=[pl.BlockSpec((TM, d), lambda i: (i, 0)), _const_spec((1, d)), _const_spec((d, n))],
        out_specs=pl.BlockSpec((TM, n), lambda i: (i, 0)),
        out_shape=jax.ShapeDtypeStruct((m, n), F32),
        compiler_params=_params(("parallel",)),
        name="norm_proj",
    )(h, g.reshape(1, d), w)


def _rope_tables(head_dim, rows_per_batch, first_pos, batch):
    r = head_dim // ROPE_FRAC
    half = r // 2
    inv = ROPE_THETA ** (-jnp.arange(half, dtype=F32) / half)
    ang = (jnp.arange(rows_per_batch) + first_pos).astype(F32)[:, None] * inv[None, :]
    cos, sin = jnp.cos(ang), jnp.sin(ang)
    ones = jnp.ones((rows_per_batch, head_dim - r), F32)
    zeros_h = jnp.zeros((rows_per_batch, half), F32)
    zeros_t = jnp.zeros((rows_per_batch, head_dim - r), F32)
    c = jnp.concatenate([cos, cos, ones], axis=1)
    s_up = jnp.concatenate([zeros_h, sin, zeros_t], axis=1)
    s_dn = jnp.concatenate([-sin, zeros_h, zeros_t], axis=1)
    reps = (batch, LANES // head_dim)
    return half, jnp.tile(c, reps), jnp.tile(s_up, reps), jnp.tile(s_dn, reps)


def _rope(ys, half, c, s_up, s_dn):
    return ys * c + pltpu.roll(ys, half, 1) * s_up + pltpu.roll(ys, LANES - half, 1) * s_dn


def _dsa_proj_kernel(h_ref, g_ref, w_ref, wvt_ref, c128_ref, u128_ref, d128_ref, c64_ref, u64_ref, d64_ref,
                     q_ref, k_ref, vt_ref, qi_ref, kw_ref, *, half128, half64):
    a = _rms(h_ref[...], g_ref[...]).astype(BF16)
    t128 = (half128, c128_ref[...], u128_ref[...], d128_ref[...])
    t64 = (half64, c64_ref[...], u64_ref[...], d64_ref[...])
    c0, c1, c2, c3 = Q_W, Q_W + KV_W, Q_W + 2 * KV_W, Q_W + 2 * KV_W + IQ_W

    def slabs(lo, hi):
        y = jnp.dot(a, w_ref[:, lo:hi], preferred_element_type=F32)
        return [y[:, s * LANES:(s + 1) * LANES] for s in range((hi - lo) // LANES)]

    for s, ys in enumerate(slabs(0, c0)):
        q_ref[:, s * LANES:(s + 1) * LANES] = (_rope(ys, *t128) * (HEAD_DIM ** -0.5 * LOG2_E)).astype(q_ref.dtype)
    for s, ys in enumerate(slabs(c0, c1)):
        k_ref[:, s * LANES:(s + 1) * LANES] = _rope(ys, *t128).astype(k_ref.dtype)
    vt_ref[...] = lax.dot_general(wvt_ref[...], a, (((1,), (1,)), ((), ())),
                                  preferred_element_type=F32).astype(vt_ref.dtype)
    for s, ys in enumerate(slabs(c2, c3)):
        qi_ref[s] = _rope(ys, *t64).astype(qi_ref.dtype)
    (ys,) = slabs(c3, c3 + LANES)
    lane = lax.broadcasted_iota(jnp.int32, ys.shape, 1)
    kw_ref[...] = jnp.where(lane < IDX_DIM, _rope(ys, *t64), ys)


def _dsa_proj(h, g, w, w_vt, rope128, rope64):
    m, d = h.shape
    n = w.shape[1]
    pairs = IQ_W // LANES
    row = lambda width: pl.BlockSpec((TM, width), lambda i: (i, 0))
    return pl.pallas_call(
        functools.partial(_dsa_proj_kernel, half128=rope128[0], half64=rope64[0]),
        grid=(m // TM,),
        in_specs=[row(d), _const_spec((1, d)), _const_spec((d, n)), _const_spec((KV_W, d))] + [row(LANES)] * 6,
        out_specs=[row(Q_W), row(KV_W), pl.BlockSpec((KV_W, TM), lambda i: (0, i)),
                   pl.BlockSpec((pairs, TM, LANES), lambda i: (0, i, 0)), row(LANES)],
        out_shape=[jax.ShapeDtypeStruct((m, Q_W), BF16), jax.ShapeDtypeStruct((m, KV_W), BF16),
                   jax.ShapeDtypeStruct((KV_W, m), BF16), jax.ShapeDtypeStruct((pairs, m, LANES), BF16),
                   jax.ShapeDtypeStruct((m, LANES), F32)],
        compiler_params=_params(("parallel",)),
        name="dsa_proj",
    )(h, g.reshape(1, d), w, w_vt, *rope128[1:], *rope64[1:])


def _matmul_norm_res_kernel(a_ref, w_ref, g_ref, r_ref, o_ref):
    y = jnp.dot(a_ref[...], w_ref[...], preferred_element_type=F32)
    o_ref[...] = r_ref[...] + _rms(y, g_ref[...])


def _matmul_norm_res(a, w, g, res):
    m, k = a.shape
    n = w.shape[1]
    return pl.pallas_call(
        _matmul_norm_res_kernel,
        grid=(m // TM,),
        in_specs=[pl.BlockSpec((TM, k), lambda i: (i, 0)), _const_spec((k, n)),
                  _const_spec((1, n)), pl.BlockSpec((TM, n), lambda i: (i, 0))],
        out_specs=pl.BlockSpec((TM, n), lambda i: (i, 0)),
        out_shape=jax.ShapeDtypeStruct((m, n), F32),
        compiler_params=_params(("parallel",)),
        name="matmul_norm_res",
    )(a, w, g.reshape(1, n), res)


def _mlp_kernel(h_ref, g_in_ref, g_out_ref, w1_ref, w2_ref, o_ref, a_ref, acc_ref):
    f = pl.program_id(1)

    @pl.when(f == 0)
    def _():
        a_ref[...] = _rms(h_ref[...], g_in_ref[...]).astype(BF16)
        acc_ref[...] = jnp.zeros_like(acc_ref)

    z = jnp.dot(a_ref[...], w1_ref[...], preferred_element_type=F32)
    z = jnp.square(jnp.maximum(z, 0.0)).astype(BF16)
    acc_ref[...] += jnp.dot(z, w2_ref[...], preferred_element_type=F32)

    @pl.when(f == pl.num_programs(1) - 1)
    def _():
        o_ref[...] = h_ref[...] + _rms(acc_ref[...], g_out_ref[...])


def _mlp(h, g_in, g_out, w1, w2, layer):
    m, d = h.shape
    ff = w1.shape[2]
    return pl.pallas_call(
        _mlp_kernel,
        grid=(m // TM, ff // TF),
        in_specs=[pl.BlockSpec((TM, d), lambda i, f: (i, 0)),
                  _const_spec((1, d)), _const_spec((1, d)),
                  pl.BlockSpec((None, d, TF), lambda i, f: (layer, 0, f)),
                  pl.BlockSpec((None, TF, d), lambda i, f: (layer, f, 0))],
        out_specs=pl.BlockSpec((TM, d), lambda i, f: (i, 0)),
        out_shape=jax.ShapeDtypeStruct((m, d), F32),
        scratch_shapes=[pltpu.VMEM((TM, d), BF16), pltpu.VMEM((TM, d), F32)],
        compiler_params=_params(("parallel", "arbitrary")),
        name="mlp",
    )(h, g_in.reshape(1, d), g_out.reshape(1, d), w1, w2)


def _cmul(ar, ai, br, bi):
    return ar * br - ai * bi, ar * bi + ai * br


def _discretise(ar, ai, log_dt):
    dt = jnp.exp(log_dt)
    mag = jnp.exp(dt * ar)
    return mag * jnp.cos(dt * ai), mag * jnp.sin(dt * ai)


def _ssm_prep_kernel(lre_ref, lim_ref, ldt_ref, lre2_ref, lim2_ref, ldt2_ref, bre_ref, bim_ref,
                     bbr_ref, bbi_ref, pwr_ref, pwi_ref):
    ar, ai = lre_ref[...], lim_ref[...]
    abr, abi = _discretise(ar, ai, ldt_ref[...])
    den = ar * ar + ai * ai
    zr, zi = abr - 1.0, abi
    fr = (zr * ar + zi * ai) / den
    fi = (zi * ar - zr * ai) / den
    br, bi = bre_ref[...], bim_ref[...]
    bbr_ref[...] = fr * br - fi * bi
    bbi_ref[...] = fr * bi + fi * br
    a1r, a1i = _discretise(lre2_ref[...], lim2_ref[...], ldt2_ref[...])
    pr, pi = a1r, a1i
    for _ in range(SCAN_ROWS - 1):
        pr, pi = _cmul(pr, pi, a1r, a1i)
    pwr_ref[0], pwi_ref[0] = a1r, a1i
    pwr_ref[1], pwi_ref[1] = pr, pi


def _ssm_prep(lam_re, lam_im, log_dt, b_re, b_im):
    g, n = lam_re.shape
    c = b_re.shape[-1]
    bt = lambda b: jnp.swapaxes(b, 1, 2)
    full = lambda *s: pl.BlockSpec(s, lambda: (0,) * len(s))
    return pl.pallas_call(
        _ssm_prep_kernel,
        in_specs=[full(g, 1, n), full(g, 1, n), full(g, 1, 1), full(g, n), full(g, n), full(g, 1),
                  full(g, c, n), full(g, c, n)],
        out_specs=[full(g, c, n), full(g, c, n), full(2, g, n), full(2, g, n)],
        out_shape=[jax.ShapeDtypeStruct((g, c, n), F32)] * 2 + [jax.ShapeDtypeStruct((2, g, n), F32)] * 2,
        name="ssm_prep",
    )(lam_re.reshape(g, 1, n), lam_im.reshape(g, 1, n), log_dt.reshape(g, 1, 1), lam_re, lam_im,
      log_dt.reshape(g, 1), bt(b_re), bt(b_im))


def _block_diag_in(bb):
    g, c, n = bb.shape
    r = bb.reshape(g // GROUPS_PER_TILE, GROUPS_PER_TILE, c, n)
    eye = jnp.eye(GROUPS_PER_TILE, dtype=bb.dtype)
    return jnp.einsum('jgcn,gh->jgchn', r, eye).reshape(g // GROUPS_PER_TILE, GROUPS_PER_TILE * c, GROUPS_PER_TILE * n)


def _block_diag_out(cc):
    g, c, n = cc.shape
    r = cc.reshape(g // GROUPS_PER_TILE, GROUPS_PER_TILE, c, n)
    eye = jnp.eye(GROUPS_PER_TILE, dtype=cc.dtype)
    return jnp.einsum('jgcn,gh->jgnhc', r, eye).reshape(g // GROUPS_PER_TILE, GROUPS_PER_TILE * n, GROUPS_PER_TILE * c)


def _row_on_sublanes(ref, j, k):
    return jnp.broadcast_to(ref[j, pl.ds(k, 1), :], (SUBLANES, ref.shape[2]))


def _mixer0_kernel(u_ref, pw_ref, ps_ref, wb_ref, wc_ref, pwr_ref, pwi_ref, d_ref, wg_ref, o_ref,
                   pbuf, uperm, ynat, sbuf, carry_ref):
    c = pl.program_id(1)
    L = u_ref.shape[0]
    sw = GROUPS_PER_TILE * SSM_STATE

    @pl.when(c == 0)
    def _():
        pbuf[0:POOL_HALO, :] = jnp.zeros((POOL_HALO, POOL_WIDTH), F32)
        carry_ref[...] = jnp.zeros_like(carry_ref)

    pbuf[POOL_HALO:POOL_HALO + L, :] = u_ref[:, 0:POOL_WIDTH]
    tpos = jnp.maximum(c * L + lax.broadcasted_iota(jnp.int32, (L, 1), 0) - PAD_FRONT, 0)
    for g, win in enumerate(POOL_WINDOWS):
        sl = slice(g * POOL_GROUP, (g + 1) * POOL_GROUP)
        x = u_ref[:, sl]
        s = x
        for k in range(1, win):
            s = s + pbuf[POOL_HALO - k:POOL_HALO - k + L, sl]
        cnt = jnp.minimum(tpos + 1, win).astype(F32)
        diff = (s / cnt - x).astype(BF16)
        y = jnp.dot(diff, pw_ref[g], preferred_element_type=F32) * ps_ref[:, sl]
        o_ref[:, sl] = y.astype(o_ref.dtype)
    pbuf[0:POOL_HALO, :] = pbuf[L:L + POOL_HALO, :]

    n_tiles = SSM_WIDTH // LANES
    for j in range(n_tiles):
        ynat[j] = u_ref[:, POOL_WIDTH + j * LANES:POOL_WIDTH + (j + 1) * LANES]

    def permute(k, carry):
        for j in range(n_tiles):
            uperm[j, pl.ds(pl.multiple_of(k * SUBLANES, SUBLANES), SUBLANES), :] = \
                ynat[j, pl.ds(k, SUBLANES, stride=SCAN_ROWS), :]
        return carry

    lax.fori_loop(0, SCAN_ROWS, permute, 0)
    row = lax.broadcasted_iota(jnp.int32, (SUBLANES, sw), 0)
    for j0 in range(0, n_tiles, TILES_PER_SCAN):
        tiles = range(j0, j0 + TILES_PER_SCAN)
        for j in tiles:
            sbuf[j - j0] = jnp.dot(uperm[j].astype(BF16), wb_ref[j], preferred_element_type=F32)
        a1 = [(_row_on_sublanes(pwr_ref, j, 0), _row_on_sublanes(pwi_ref, j, 0)) for j in tiles]

        def scan_step(k, xs, store):
            rows = pl.ds(pl.multiple_of(k * SUBLANES, SUBLANES), SUBLANES)
            out = []
            for c, ((a1r, a1i), (xr, xi)) in enumerate(zip(a1, xs)):
                xr, xi = (a1r * xr - a1i * xi + sbuf[c, rows, 0:sw], a1r * xi + a1i * xr + sbuf[c, rows, sw:2 * sw])
                if store:
                    sbuf[c, rows, 0:sw] = xr
                    sbuf[c, rows, sw:2 * sw] = xi
                out.append((xr, xi))
            return tuple(out)

        zero = jnp.zeros((SUBLANES, sw), F32)
        ends = lax.fori_loop(0, SCAN_ROWS, functools.partial(scan_step, store=False),
                             tuple((zero, zero) for _ in tiles))

        starts = []
        for j, (er, ei) in zip(tiles, ends):
            cr = _row_on_sublanes(carry_ref, j, 0)
            ci = _row_on_sublanes(carry_ref, j, 1)
            sr = jnp.where(row == 0, cr, pltpu.roll(er, 1, 0))
            si = jnp.where(row == 0, ci, pltpu.roll(ei, 1, 0))
            qr = _row_on_sublanes(pwr_ref, j, 1)
            qi = _row_on_sublanes(pwi_ref, j, 1)
            for k in (1, 2, 4):
                mr, mi = jnp.where(row >= k, qr, 0.0), jnp.where(row >= k, qi, 0.0)
                tr, ti = pltpu.roll(sr, k, 0), pltpu.roll(si, k, 0)
                sr, si = sr + mr * tr - mi * ti, si + mr * ti + mi * tr
                qr, qi = _cmul(qr, qi, qr, qi)
            starts.append((sr, si))

        lax.fori_loop(0, SCAN_ROWS, functools.partial(scan_step, store=True), tuple(starts))
        for j in tiles:
            c = j - j0
            carry_ref[j, 0:1, :] = sbuf[c, L - 1:L, 0:sw]
            carry_ref[j, 1:2, :] = sbuf[c, L - 1:L, sw:2 * sw]
            yj = jnp.dot(sbuf[c].astype(BF16), wc_ref[j], preferred_element_type=F32)
            uperm[j] = yj + d_ref[:, j * LANES:(j + 1) * LANES] * uperm[j]

    def unpermute(k, carry):
        for j in range(n_tiles):
            ynat[j, pl.ds(k, SUBLANES, stride=SCAN_ROWS), :] = \
                uperm[j, pl.ds(pl.multiple_of(k * SUBLANES, SUBLANES), SUBLANES), :]
        return carry

    lax.fori_loop(0, SCAN_ROWS, unpermute, 0)
    y = jax.nn.gelu(jnp.concatenate([ynat[j] for j in range(n_tiles)], axis=1))
    gate = jax.nn.sigmoid(jnp.dot(y.astype(BF16), wg_ref[...], preferred_element_type=F32))
    o_ref[:, POOL_WIDTH:] = (y * gate).astype(o_ref.dtype)


def _mixer0(u, batch, pool_w, pool_scale, wb, wc, pwr, pwi, d, w_glu):
    m, width = u.shape
    nj = SSM_WIDTH // LANES
    sw = GROUPS_PER_TILE * SSM_STATE
    chunks = T_PAD // L_SSM
    return pl.pallas_call(
        _mixer0_kernel,
        grid=(batch, chunks),
        in_specs=[pl.BlockSpec((L_SSM, width), lambda b, c: (b * chunks + c, 0)),
                  _const_spec((len(POOL_WINDOWS), POOL_GROUP, POOL_GROUP)), _const_spec((1, POOL_WIDTH)),
                  _const_spec((nj, LANES, 2 * sw)), _const_spec((nj, 2 * sw, LANES)),
                  _const_spec((nj, 2, sw)), _const_spec((nj, 2, sw)),
                  _const_spec((1, SSM_WIDTH)), _const_spec((SSM_WIDTH, SSM_WIDTH))],
        out_specs=pl.BlockSpec((L_SSM, width), lambda b, c: (b * chunks + c, 0)),
        out_shape=jax.ShapeDtypeStruct((m, width), BF16),
        scratch_shapes=[pltpu.VMEM((POOL_HALO + L_SSM, POOL_WIDTH), F32),
                        pltpu.VMEM((nj, L_SSM, LANES), F32),
                        pltpu.VMEM((nj, L_SSM, LANES), F32),
                        pltpu.VMEM((TILES_PER_SCAN, L_SSM, 2 * sw), F32),
                        pltpu.VMEM((nj, 2, sw), F32)],
        compiler_params=_params(("parallel", "arbitrary")),
        name="mixer0",
    )(u, pool_w, pool_scale, wb, wc, pwr, pwi, d, w_glu)


CHUNK_SHIFT = CHUNK.bit_length() - 1
assert 1 << CHUNK_SHIFT == CHUNK
KB = 512
KB_SHIFT = KB.bit_length() - 1
assert 1 << KB_SHIFT == KB
N_KB = -(-T_PAD // KB)
COUNT_ROWS = 64
PAIRS_PER_DOT = 2
ONES_ROWS = 16
VT_ROWS = HEAD_DIM + ONES_ROWS


def _chunk_id(t):
    return jnp.where(t < N_META, 0, ((t - N_META) >> CHUNK_SHIFT) + 1)


def _fold_rows(x, op):
    return op(x.reshape(x.shape[0] // SUBLANES, SUBLANES, x.shape[1]), axis=0)


def _stage_rows(src, dst):
    for kb in range(N_KB):
        n = min(KB, T_PAD - kb * KB)
        dst[kb, 0:n, :] = src[kb * KB:kb * KB + n, :].astype(dst.dtype)
        if n < KB:
            dst[kb, n:KB, :] = jnp.zeros((KB - n, dst.shape[2]), dst.dtype)


def _stage_values(src, dst):
    for kb in range(N_KB):
        n = min(KB, T_PAD - kb * KB)
        for kv in range(N_KV_HEADS):
            r0 = kv * VT_ROWS
            dst[kb, r0:r0 + HEAD_DIM, 0:n] = src[kv * HEAD_DIM:(kv + 1) * HEAD_DIM, kb * KB:kb * KB + n]
            if n < KB:
                dst[kb, r0:r0 + HEAD_DIM, n:KB] = jnp.zeros((HEAD_DIM, KB - n), dst.dtype)
            dst[kb, r0 + HEAD_DIM:r0 + VT_ROWS, :] = jnp.ones((ONES_ROWS, KB), dst.dtype)


def _for_key_blocks(n_kb, body):
    def pair(p, carry):
        body(2 * p)
        body(2 * p + 1)
        return carry

    lax.fori_loop(0, n_kb >> 1, pair, 0)

    @pl.when((n_kb & 1) == 1)
    def _():
        body(n_kb - 1)


def _dsa_kernel(qi_ref, wq_ref, q_ref, kw_ref, k_ref, vt_ref, o_ref,
                kie, kio, kbuf, vtbuf, score_buf, lg_buf, m_acc, o_acc):
    i = pl.program_id(1)
    tq = q_ref.shape[0]
    nt = (((1,), (1,)), ((), ()))

    @pl.when(i == 0)
    def _():
        kw = kw_ref[...]
        lane = lax.broadcasted_iota(jnp.int32, kw.shape, 1)
        _stage_rows(jnp.where(lane < IDX_DIM, kw, 0.0), kie)
        _stage_rows(jnp.where(lane >= IDX_DIM, pltpu.roll(kw, IDX_DIM, 1), 0.0), kio)
        _stage_rows(k_ref, kbuf)
        _stage_values(vt_ref, vtbuf)

    last_q = (i + 1) * tq - 1 - PAD_FRONT
    key_end = N_META + CHUNK * (((last_q - N_META) >> CHUNK_SHIFT) + 1)
    n_kb = jnp.minimum((key_end + PAD_FRONT + KB - 1) >> KB_SHIFT, N_KB)

    w_t = wq_ref[...].T * ((IDX_HEADS ** -0.5) * (IDX_DIM ** -0.5))
    pairs = IDX_HEADS // 2
    qpos = i * tq + lax.broadcasted_iota(jnp.int32, (1, tq), 1) - PAD_FRONT

    def index_block(kb):
        score = jnp.zeros((KB, tq), F32)
        for p0 in range(0, pairs, PAIRS_PER_DOT):
            qg = qi_ref[p0:p0 + PAIRS_PER_DOT].reshape(PAIRS_PER_DOT * tq, LANES)
            se = lax.dot_general(kie[kb], qg, nt, preferred_element_type=F32)
            so = lax.dot_general(kio[kb], qg, nt, preferred_element_type=F32)
            for c in range(PAIRS_PER_DOT):
                p = p0 + c
                we = w_t[IDX_DIM + 2 * p:IDX_DIM + 2 * p + 1, :]
                wo = w_t[IDX_DIM + 2 * p + 1:IDX_DIM + 2 * p + 2, :]
                score = score + we * jnp.maximum(se[:, c * tq:(c + 1) * tq], 0.0)
                score = score + wo * jnp.maximum(so[:, c * tq:(c + 1) * tq], 0.0)
        kpos = kb * KB + lax.broadcasted_iota(jnp.int32, (KB, 1), 0) - PAD_FRONT
        allowed = (_chunk_id(kpos) <= _chunk_id(qpos)) & (kpos >= 0) & (kpos < T_REAL)
        score_buf[kb] = jnp.where(allowed, score, -jnp.inf)

    _for_key_blocks(n_kb, index_block)

    def key_to_float(key):
        return pltpu.bitcast(key ^ ((key >> 31) & 0x7FFFFFFF), F32)

    def count_ge(cand):
        cand_f = key_to_float(cand)

        def block(kb, acc):
            ind = jnp.where(score_buf[kb] >= cand_f, 1.0, 0.0)
            return acc + jnp.sum(ind.reshape(KB // COUNT_ROWS, COUNT_ROWS, tq), axis=0)
        acc = lax.fori_loop(0, n_kb, block, jnp.zeros((COUNT_ROWS, tq), F32))
        return jnp.sum(acc, axis=0, keepdims=True)

    c0 = count_ge(jnp.zeros((1, tq), jnp.int32))
    th = jnp.where(c0 >= TOPK, 0, INT_MIN).astype(jnp.int32)
    n_ge = jnp.where(c0 >= TOPK, c0, float(N_KB * KB))

    def search(j, carry):
        th, n_ge = carry
        cand = th | jnp.left_shift(jnp.int32(1), 30 - j)
        c = count_ge(cand)
        return jnp.where(c >= TOPK, cand, th), jnp.where(c >= TOPK, c, n_ge)

    th, n_ge = lax.fori_loop(0, 31, search, (th, n_ge))
    tied = (n_ge > TOPK) & (th > NEG_INF_KEY) & (qpos >= 0)
    th = key_to_float(jnp.maximum(th, NEG_INF_KEY + 1))

    @pl.when(jnp.max(tied.astype(jnp.int32)) > 0)
    def _():
        def count_eq(kb, acc):
            ind = jnp.where(score_buf[kb] == th, 1.0, 0.0)
            return acc + jnp.sum(ind.reshape(KB // COUNT_ROWS, COUNT_ROWS, tq), axis=0)
        n_eq = jnp.sum(lax.fori_loop(0, n_kb, count_eq, jnp.zeros((COUNT_ROWS, tq), F32)), axis=0, keepdims=True)
        keep = TOPK - (n_ge - n_eq)
        lower = (lax.broadcasted_iota(jnp.int32, (KB, KB), 0) >= lax.broadcasted_iota(jnp.int32, (KB, KB), 1))
        lower = jnp.where(lower, 1.0, 0.0).astype(BF16)

        def drop_late(kb, seen):
            score = score_buf[kb]
            eq = (score == th) & tied
            rank = seen + jnp.dot(lower, jnp.where(eq, 1.0, 0.0).astype(BF16), preferred_element_type=F32)
            score_buf[kb] = jnp.where(eq & (rank > keep), -jnp.inf, score)
            return rank[KB - 1:KB, :]

        lax.fori_loop(0, n_kb, drop_late, jnp.zeros((1, tq), F32))

    qs = []
    for kv in range(N_KV_HEADS):
        h0 = kv * KV_GROUP
        qs.append(jnp.concatenate([q_ref[:, (h0 + g) * HEAD_DIM:(h0 + g + 1) * HEAD_DIM]
                                   for g in range(KV_GROUP)], axis=0))
    m_acc[...] = jnp.full(m_acc.shape, -jnp.inf, F32)
    o_acc[...] = jnp.zeros(o_acc.shape, F32)

    def logits_block(kb):
        sel = score_buf[kb] >= th
        for kv in range(N_KV_HEADS):
            kk = kbuf[kb, :, kv * HEAD_DIM:(kv + 1) * HEAD_DIM]
            lg = lax.dot_general(kk, qs[kv], nt, preferred_element_type=F32)
            lg = jnp.concatenate([jnp.where(sel, lg[:, g * tq:(g + 1) * tq], -jnp.inf)
                                  for g in range(KV_GROUP)], axis=1)
            lg_buf[kv, kb] = lg
            m_acc[kv] = jnp.maximum(m_acc[kv], _fold_rows(lg, jnp.max))

    _for_key_blocks(n_kb, logits_block)
    mx = [jnp.max(m_acc[kv], axis=0, keepdims=True) for kv in range(N_KV_HEADS)]

    def value_block(kb):
        for kv in range(N_KV_HEADS):
            e = jnp.exp2((lg_buf[kv, kb] - mx[kv]).astype(BF16))
            vt = vtbuf[kb, kv * VT_ROWS:(kv + 1) * VT_ROWS, :]
            o_acc[kv] += jnp.dot(vt, e, preferred_element_type=F32)

    _for_key_blocks(n_kb, value_block)
    for kv in range(N_KV_HEADS):
        out_t = o_acc[kv, 0:HEAD_DIM, :] / o_acc[kv, HEAD_DIM:HEAD_DIM + 1, :]
        for g in range(KV_GROUP):
            hd = kv * KV_GROUP + g
            o_ref[:, hd * HEAD_DIM:(hd + 1) * HEAD_DIM] = out_t[:, g * tq:(g + 1) * tq].T.astype(o_ref.dtype)


def _dsa(qi, kw, q, k, vt, batch):
    m = q.shape[0]
    nq = T_PAD // TQ
    pairs = qi.shape[0]
    cols = KV_GROUP * TQ
    return pl.pallas_call(
        _dsa_kernel,
        grid=(batch, nq),
        in_specs=[pl.BlockSpec((pairs, TQ, LANES), lambda b, i: (0, b * nq + i, 0)),
                  pl.BlockSpec((TQ, LANES), lambda b, i: (b * nq + i, 0)),
                  pl.BlockSpec((TQ, Q_W), lambda b, i: (b * nq + i, 0)),
                  pl.BlockSpec((T_PAD, LANES), lambda b, i: (b, 0)),
                  pl.BlockSpec((T_PAD, KV_W), lambda b, i: (b, 0)),
                  pl.BlockSpec((KV_W, T_PAD), lambda b, i: (0, b))],
        out_specs=pl.BlockSpec((TQ, Q_W), lambda b, i: (b * nq + i, 0)),
        out_shape=jax.ShapeDtypeStruct((m, Q_W), BF16),
        scratch_shapes=[pltpu.VMEM((N_KB, KB, LANES), BF16), pltpu.VMEM((N_KB, KB, LANES), BF16),
                        pltpu.VMEM((N_KB, KB, KV_W), BF16), pltpu.VMEM((N_KB, N_KV_HEADS * VT_ROWS, KB), BF16),
                        pltpu.VMEM((N_KB, KB, TQ), F32),
                        pltpu.VMEM((N_KV_HEADS, N_KB, KB, cols), F32),
                        pltpu.VMEM((N_KV_HEADS, SUBLANES, cols), F32),
                        pltpu.VMEM((N_KV_HEADS, VT_ROWS, cols), F32)],
        compiler_params=_params(("parallel", "arbitrary")),
        name="dsa",
    )(qi, kw, q, kw, k, vt)


def kernel(x, meta, norm_g, ab_w_in, pool_w, pool_scale, s5_lambda_re, s5_lambda_im, s5_log_dt,
           s5_b_re, s5_b_im, s5_c_re, s5_c_im, s5_d, s5_w_glu, ab_w_out, c_w_in, c_w_out, mlp_w1, mlp_w2):
    batch, seq, d = x.shape
    assert (seq + N_META, d) == (T_REAL, D_MODEL)
    depth = norm_g.shape[0]
    meta_b = jnp.broadcast_to(meta[None].astype(x.dtype), (batch, N_META, d))
    pad = jnp.zeros((batch, T_PAD - T_REAL, d), x.dtype)
    h = jnp.concatenate([pad, meta_b, x], axis=1).reshape(batch * T_PAD, d)

    rope128 = _rope_tables(HEAD_DIM, T_PAD, -PAD_FRONT, batch)
    rope64 = _rope_tables(IDX_DIM, T_PAD, -PAD_FRONT, batch)
    w1, w2 = mlp_w1.astype(BF16), mlp_w2.astype(BF16)

    for layer in range(depth):
        g = norm_g[layer]
        if layer % 2 == 0:
            e = layer // 2
            u = _norm_proj(h, g[0], ab_w_in[e].astype(BF16))
            bbr, bbi, pwr, pwi = _ssm_prep(s5_lambda_re[e], s5_lambda_im[e], s5_log_dt[e], s5_b_re[e], s5_b_im[e])
            nj = SSM_GROUPS // GROUPS_PER_TILE
            tile_major = lambda p: p.reshape(2, nj, GROUPS_PER_TILE * SSM_STATE).swapaxes(0, 1)
            wb = jnp.concatenate([_block_diag_in(bbr), _block_diag_in(bbi)], axis=-1).astype(BF16)
            wc = jnp.concatenate([_block_diag_out(s5_c_re[e]), -_block_diag_out(s5_c_im[e])], axis=1).astype(BF16)
            mix = _mixer0(u, batch, pool_w[e].astype(BF16), pool_scale[e].reshape(1, POOL_WIDTH), wb, wc,
                          tile_major(pwr), tile_major(pwi),
                          s5_d[e].reshape(1, SSM_WIDTH), s5_w_glu[e].astype(BF16))
            w_out = ab_w_out[e]
        else:
            o = layer // 2
            n_in = c_w_in.shape[-1]
            w_in = jnp.pad(c_w_in[o], ((0, 0), (0, -n_in % LANES))).astype(BF16)
            w_vt = w_in[:, Q_W + KV_W:Q_W + 2 * KV_W].T
            q, k, vt, qi, kw = _dsa_proj(h, g[0], w_in, w_vt, rope128, rope64)
            mix = _dsa(qi, kw, q, k, vt, batch)
            w_out = c_w_out[o]
        h = _matmul_norm_res(mix, w_out.astype(BF16), g[1], h)
        h = _mlp(h, g[2], g[3], w1, w2, layer)
    return h.reshape(batch, T_PAD, d)[:, PAD_FRONT + N_META:]
```

```python
import functools

import jax
import jax.numpy as jnp
from jax import lax
from jax.experimental import pallas as pl
from jax.experimental.pallas import tpu as pltpu

F32 = jnp.float32
BF16 = jnp.bfloat16

D_MODEL = 2048
CHUNK = 64
N_META = 16
POOL_WIDTH = 1024
POOL_WINDOWS = (2, 4, 8, 16)
POOL_GROUP = 256
POOL_HALO = 16
SSM_WIDTH = 1024
SSM_GROUP_CH = 16
SSM_GROUPS = 64
SSM_STATE = 64
GROUPS_PER_TILE = 8
HEAD_DIM = 128
N_HEADS = 16
N_KV_HEADS = 4
KV_GROUP = 4
IDX_HEADS = 16
IDX_DIM = 64
ROPE_THETA = 500000.0
ROPE_FRAC = 4
TOPK = 256
EPS = 1e-6
Q_W = N_HEADS * HEAD_DIM
KV_W = N_KV_HEADS * HEAD_DIM
IQ_W = IDX_HEADS * IDX_DIM

LANES = 128
SUBLANES = 8
VMEM_LIMIT = 56 * 1024 * 1024

T_REAL = N_META + 2048
T_PAD = 2176
PAD_FRONT = T_PAD - T_REAL
TM = 512
TQ = 128
L_SSM = 544
SCAN_ROWS = L_SSM // 8
assert SCAN_ROWS * 8 == L_SSM
TILES_PER_SCAN = 2
TF = 1024

LOG2_E = 1.4426950408889634
INT_MIN = -2 ** 31
NEG_INF_KEY = -2139095041


def _params(sem):
    return pltpu.CompilerParams(dimension_semantics=sem, vmem_limit_bytes=VMEM_LIMIT)


def _rms(x, g):
    return x * lax.rsqrt(jnp.mean(x * x, axis=-1, keepdims=True) + EPS) * g


def _const_spec(shape):
    zeros = (0,) * len(shape)
    return pl.BlockSpec(shape, lambda *_: zeros, pipeline_mode=pl.Buffered(1))


def _norm_proj_kernel(h_ref, g_ref, w_ref, o_ref):
    a = _rms(h_ref[...], g_ref[...]).astype(BF16)
    o_ref[...] = jnp.dot(a, w_ref[...], preferred_element_type=F32)


def _norm_proj(h, g, w):
    m, d = h.shape
    n = w.shape[1]
    return pl.pallas_call(
        _norm_proj_kernel,
        grid=(m // TM,),
        in_specs=[pl.BlockSpec((TM, d), lambda i: (i, 0)), _const_spec((1, d)), _const_spec((d, n))],
        out_specs=pl.BlockSpec((TM, n), lambda i: (i, 0)),
        out_shape=jax.ShapeDtypeStruct((m, n), F32),
        compiler_params=_params(("parallel",)),
        name="norm_proj",
    )(h, g.reshape(1, d), w)


def _rope_tables(head_dim, rows_per_batch, first_pos, batch):
    r = head_dim // ROPE_FRAC
    half = r // 2
    inv = ROPE_THETA ** (-jnp.arange(half, dtype=F32) / half)
    ang = (jnp.arange(rows_per_batch) + first_pos).astype(F32)[:, None] * inv[None, :]
    cos, sin = jnp.cos(ang), jnp.sin(ang)
    ones = jnp.ones((rows_per_batch, head_dim - r), F32)
    zeros_h = jnp.zeros((rows_per_batch, half), F32)
    zeros_t = jnp.zeros((rows_per_batch, head_dim - r), F32)
    c = jnp.concatenate([cos, cos, ones], axis=1)
    s_up = jnp.concatenate([zeros_h, sin, zeros_t], axis=1)
    s_dn = jnp.concatenate([-sin, zeros_h, zeros_t], axis=1)
    reps = (batch, LANES // head_dim)
    return half, jnp.tile(c, reps), jnp.tile(s_up, reps), jnp.tile(s_dn, reps)


def _rope(ys, half, c, s_up, s_dn):
    return ys * c + pltpu.roll(ys, half, 1) * s_up + pltpu.roll(ys, LANES - half, 1) * s_dn


def _dsa_proj_kernel(h_ref, g_ref, w_ref, wvt_ref, c128_ref, u128_ref, d128_ref, c64_ref, u64_ref, d64_ref,
                     q_ref, k_ref, vt_ref, qi_ref, kw_ref, *, half128, half64):
    a = _rms(h_ref[...], g_ref[...]).astype(BF16)
    t128 = (half128, c128_ref[...], u128_ref[...], d128_ref[...])
    t64 = (half64, c64_ref[...], u64_ref[...], d64_ref[...])
    c0, c1, c2, c3 = Q_W, Q_W + KV_W, Q_W + 2 * KV_W, Q_W + 2 * KV_W + IQ_W

    def slabs(lo, hi):
        y = jnp.dot(a, w_ref[:, lo:hi], preferred_element_type=F32)
        return [y[:, s * LANES:(s + 1) * LANES] for s in range((hi - lo) // LANES)]

    for s, ys in enumerate(slabs(0, c0)):
        q_ref[:, s * LANES:(s + 1) * LANES] = (_rope(ys, *t128) * (HEAD_DIM ** -0.5 * LOG2_E)).astype(q_ref.dtype)
    for s, ys in enumerate(slabs(c0, c1)):
        k_ref[:, s * LANES:(s + 1) * LANES] = _rope(ys, *t128).astype(k_ref.dtype)
    vt_ref[...] = lax.dot_general(wvt_ref[...], a, (((1,), (1,)), ((), ())),
                                  preferred_element_type=F32).astype(vt_ref.dtype)
    for s, ys in enumerate(slabs(c2, c3)):
        qi_ref[s] = _rope(ys, *t64).astype(qi_ref.dtype)
    (ys,) = slabs(c3, c3 + LANES)
    lane = lax.broadcasted_iota(jnp.int32, ys.shape, 1)
    kw_ref[...] = jnp.where(lane < IDX_DIM, _rope(ys, *t64), ys)


def _dsa_proj(h, g, w, w_vt, rope128, rope64):
    m, d = h.shape
    n = w.shape[1]
    pairs = IQ_W // LANES
    row = lambda width: pl.BlockSpec((TM, width), lambda i: (i, 0))
    return pl.pallas_call(
        functools.partial(_dsa_proj_kernel, half128=rope128[0], half64=rope64[0]),
        grid=(m // TM,),
        in_specs=[row(d), _const_spec((1, d)), _const_spec((d, n)), _const_spec((KV_W, d))] + [row(LANES)] * 6,
        out_specs=[row(Q_W), row(KV_W), pl.BlockSpec((KV_W, TM), lambda i: (0, i)),
                   pl.BlockSpec((pairs, TM, LANES), lambda i: (0, i, 0)), row(LANES)],
        out_shape=[jax.ShapeDtypeStruct((m, Q_W), BF16), jax.ShapeDtypeStruct((m, KV_W), BF16),
                   jax.ShapeDtypeStruct((KV_W, m), BF16), jax.ShapeDtypeStruct((pairs, m, LANES), BF16),
                   jax.ShapeDtypeStruct((m, LANES), F32)],
        compiler_params=_params(("parallel",)),
        name="dsa_proj",
    )(h, g.reshape(1, d), w, w_vt, *rope128[1:], *rope64[1:])


def _matmul_norm_res_kernel(a_ref, w_ref, g_ref, r_ref, o_ref):
    y = jnp.dot(a_ref[...], w_ref[...], preferred_element_type=F32)
    o_ref[...] = r_ref[...] + _rms(y, g_ref[...])


def _matmul_norm_res(a, w, g, res):
    m, k = a.shape
    n = w.shape[1]
    return pl.pallas_call(
        _matmul_norm_res_kernel,
        grid=(m // TM,),
        in_specs=[pl.BlockSpec((TM, k), lambda i: (i, 0)), _const_spec((k, n)),
                  _const_spec((1, n)), pl.BlockSpec((TM, n), lambda i: (i, 0))],
        out_specs=pl.BlockSpec((TM, n), lambda i: (i, 0)),
        out_shape=jax.ShapeDtypeStruct((m, n), F32),
        compiler_params=_params(("parallel",)),
        name="matmul_norm_res",
    )(a, w, g.reshape(1, n), res)


def _mlp_kernel(h_ref, g_in_ref, g_out_ref, w1_ref, w2_ref, o_ref, a_ref, acc_ref):
    f = pl.program_id(1)

    @pl.when(f == 0)
    def _():
        a_ref[...] = _rms(h_ref[...], g_in_ref[...]).astype(BF16)
        acc_ref[...] = jnp.zeros_like(acc_ref)

    z = jnp.dot(a_ref[...], w1_ref[...], preferred_element_type=F32)
    z = jnp.square(jnp.maximum(z, 0.0)).astype(BF16)
    acc_ref[...] += jnp.dot(z, w2_ref[...], preferred_element_type=F32)

    @pl.when(f == pl.num_programs(1) - 1)
    def _():
        o_ref[...] = h_ref[...] + _rms(acc_ref[...], g_out_ref[...])


def _mlp(h, g_in, g_out, w1, w2, layer):
    m, d = h.shape
    ff = w1.shape[2]
    return pl.pallas_call(
        _mlp_kernel,
        grid=(m // TM, ff // TF),
        in_specs=[pl.BlockSpec((TM, d), lambda i, f: (i, 0)),
                  _const_spec((1, d)), _const_spec((1, d)),
                  pl.BlockSpec((None, d, TF), lambda i, f: (layer, 0, f)),
                  pl.BlockSpec((None, TF, d), lambda i, f: (layer, f, 0))],
        out_specs=pl.BlockSpec((TM, d), lambda i, f: (i, 0)),
        out_shape=jax.ShapeDtypeStruct((m, d), F32),
        scratch_shapes=[pltpu.VMEM((TM, d), BF16), pltpu.VMEM((TM, d), F32)],
        compiler_params=_params(("parallel", "arbitrary")),
        name="mlp",
    )(h, g_in.reshape(1, d), g_out.reshape(1, d), w1, w2)


def _cmul(ar, ai, br, bi):
    return ar * br - ai * bi, ar * bi + ai * br


def _discretise(ar, ai, log_dt):
    dt = jnp.exp(log_dt)
    mag = jnp.exp(dt * ar)
    return mag * jnp.cos(dt * ai), mag * jnp.sin(dt * ai)


def _ssm_prep_kernel(lre_ref, lim_ref, ldt_ref, lre2_ref, lim2_ref, ldt2_ref, bre_ref, bim_ref,
                     bbr_ref, bbi_ref, pwr_ref, pwi_ref):
    ar, ai = lre_ref[...], lim_ref[...]
    abr, abi = _discretise(ar, ai, ldt_ref[...])
    den = ar * ar + ai * ai
    zr, zi = abr - 1.0, abi
    fr = (zr * ar + zi * ai) / den
    fi = (zi * ar - zr * ai) / den
    br, bi = bre_ref[...], bim_ref[...]
    bbr_ref[...] = fr * br - fi * bi
    bbi_ref[...] = fr * bi + fi * br
    a1r, a1i = _discretise(lre2_ref[...], lim2_ref[...], ldt2_ref[...])
    pr, pi = a1r, a1i
    for _ in range(SCAN_ROWS - 1):
        pr, pi = _cmul(pr, pi, a1r, a1i)
    pwr_ref[0], pwi_ref[0] = a1r, a1i
    pwr_ref[1], pwi_ref[1] = pr, pi


def _ssm_prep(lam_re, lam_im, log_dt, b_re, b_im):
    g, n = lam_re.shape
    c = b_re.shape[-1]
    bt = lambda b: jnp.swapaxes(b, 1, 2)
    full = lambda *s: pl.BlockSpec(s, lambda: (0,) * len(s))
    return pl.pallas_call(
        _ssm_prep_kernel,
        in_specs=[full(g, 1, n), full(g, 1, n), full(g, 1, 1), full(g, n), full(g, n), full(g, 1),
                  full(g, c, n), full(g, c, n)],
        out_specs=[full(g, c, n), full(g, c, n), full(2, g, n), full(2, g, n)],
        out_shape=[jax.ShapeDtypeStruct((g, c, n), F32)] * 2 + [jax.ShapeDtypeStruct((2, g, n), F32)] * 2,
        name="ssm_prep",
    )(lam_re.reshape(g, 1, n), lam_im.reshape(g, 1, n), log_dt.reshape(g, 1, 1), lam_re, lam_im,
      log_dt.reshape(g, 1), bt(b_re), bt(b_im))


def _block_diag_in(bb):
    g, c, n = bb.shape
    r = bb.reshape(g // GROUPS_PER_TILE, GROUPS_PER_TILE, c, n)
    eye = jnp.eye(GROUPS_PER_TILE, dtype=bb.dtype)
    return jnp.einsum('jgcn,gh->jgchn', r, eye).reshape(g // GROUPS_PER_TILE, GROUPS_PER_TILE * c, GROUPS_PER_TILE * n)


def _block_diag_out(cc):
    g, c, n = cc.shape
    r = cc.reshape(g // GROUPS_PER_TILE, GROUPS_PER_TILE, c, n)
    eye = jnp.eye(GROUPS_PER_TILE, dtype=cc.dtype)
    return jnp.einsum('jgcn,gh->jgnhc', r, eye).reshape(g // GROUPS_PER_TILE, GROUPS_PER_TILE * n, GROUPS_PER_TILE * c)


def _row_on_sublanes(ref, j, k):
    return jnp.broadcast_to(ref[j, pl.ds(k, 1), :], (SUBLANES, ref.shape[2]))


def _mixer0_kernel(u_ref, pw_ref, ps_ref, wb_ref, wc_ref, pwr_ref, pwi_ref, d_ref, wg_ref, o_ref,
                   pbuf, uperm, ynat, sbuf, carry_ref):
    c = pl.program_id(1)
    L = u_ref.shape[0]
    sw = GROUPS_PER_TILE * SSM_STATE

    @pl.when(c == 0)
    def _():
        pbuf[0:POOL_HALO, :] = jnp.zeros((POOL_HALO, POOL_WIDTH), F32)
        carry_ref[...] = jnp.zeros_like(carry_ref)

    pbuf[POOL_HALO:POOL_HALO + L, :] = u_ref[:, 0:POOL_WIDTH]
    tpos = jnp.maximum(c * L + lax.broadcasted_iota(jnp.int32, (L, 1), 0) - PAD_FRONT, 0)
    for g, win in enumerate(POOL_WINDOWS):
        sl = slice(g * POOL_GROUP, (g + 1) * POOL_GROUP)
        x = u_ref[:, sl]
        s = x
        for k in range(1, win):
            s = s + pbuf[POOL_HALO - k:POOL_HALO - k + L, sl]
        cnt = jnp.minimum(tpos + 1, win).astype(F32)
        diff = (s / cnt - x).astype(BF16)
        y = jnp.dot(diff, pw_ref[g], preferred_element_type=F32) * ps_ref[:, sl]
        o_ref[:, sl] = y.astype(o_ref.dtype)
    pbuf[0:POOL_HALO, :] = pbuf[L:L + POOL_HALO, :]

    n_tiles = SSM_WIDTH // LANES
    for j in range(n_tiles):
        ynat[j] = u_ref[:, POOL_WIDTH + j * LANES:POOL_WIDTH + (j + 1) * LANES]

    def permute(k, carry):
        for j in range(n_tiles):
            uperm[j, pl.ds(pl.multiple_of(k * SUBLANES, SUBLANES), SUBLANES), :] = \
                ynat[j, pl.ds(k, SUBLANES, stride=SCAN_ROWS), :]
        return carry

    lax.fori_loop(0, SCAN_ROWS, permute, 0)
    row = lax.broadcasted_iota(jnp.int32, (SUBLANES, sw), 0)
    for j0 in range(0, n_tiles, TILES_PER_SCAN):
        tiles = range(j0, j0 + TILES_PER_SCAN)
        for j in tiles:
            sbuf[j - j0] = jnp.dot(uperm[j].astype(BF16), wb_ref[j], preferred_element_type=F32)
        a1 = [(_row_on_sublanes(pwr_ref, j, 0), _row_on_sublanes(pwi_ref, j, 0)) for j in tiles]

        def scan_step(k, xs, store):
            rows = pl.ds(pl.multiple_of(k * SUBLANES, SUBLANES), SUBLANES)
            out = []
            for c, ((a1r, a1i), (xr, xi)) in enumerate(zip(a1, xs)):
                xr, xi = (a1r * xr - a1i * xi + sbuf[c, rows, 0:sw], a1r * xi + a1i * xr + sbuf[c, rows, sw:2 * sw])
                if store:
                    sbuf[c, rows, 0:sw] = xr
                    sbuf[c, rows, sw:2 * sw] = xi
                out.append((xr, xi))
            return tuple(out)

        zero = jnp.zeros((SUBLANES, sw), F32)
        ends = lax.fori_loop(0, SCAN_ROWS, functools.partial(scan_step, store=False),
                             tuple((zero, zero) for _ in tiles))

        starts = []
        for j, (er, ei) in zip(tiles, ends):
            cr = _row_on_sublanes(carry_ref, j, 0)
            ci = _row_on_sublanes(carry_ref, j, 1)
            sr = jnp.where(row == 0, cr, pltpu.roll(er, 1, 0))
            si = jnp.where(row == 0, ci, pltpu.roll(ei, 1, 0))
            qr = _row_on_sublanes(pwr_ref, j, 1)
            qi = _row_on_sublanes(pwi_ref, j, 1)
            for k in (1, 2, 4):
                mr, mi = jnp.where(row >= k, qr, 0.0), jnp.where(row >= k, qi, 0.0)
                tr, ti = pltpu.roll(sr, k, 0), pltpu.roll(si, k, 0)
                sr, si = sr + mr * tr - mi * ti, si + mr * ti + mi * tr
                qr, qi = _cmul(qr, qi, qr, qi)
            starts.append((sr, si))

        lax.fori_loop(0, SCAN_ROWS, functools.partial(scan_step, store=True), tuple(starts))
        for j in tiles:
            c = j - j0
            carry_ref[j, 0:1, :] = sbuf[c, L - 1:L, 0:sw]
            carry_ref[j, 1:2, :] = sbuf[c, L - 1:L, sw:2 * sw]
            yj = jnp.dot(sbuf[c].astype(BF16), wc_ref[j], preferred_element_type=F32)
            uperm[j] = yj + d_ref[:, j * LANES:(j + 1) * LANES] * uperm[j]

    def unpermute(k, carry):
        for j in range(n_tiles):
            ynat[j, pl.ds(k, SUBLANES, stride=SCAN_ROWS), :] = \
                uperm[j, pl.ds(pl.multiple_of(k * SUBLANES, SUBLANES), SUBLANES), :]
        return carry

    lax.fori_loop(0, SCAN_ROWS, unpermute, 0)
    y = jax.nn.gelu(jnp.concatenate([ynat[j] for j in range(n_tiles)], axis=1))
    gate = jax.nn.sigmoid(jnp.dot(y.astype(BF16), wg_ref[...], preferred_element_type=F32))
    o_ref[:, POOL_WIDTH:] = (y * gate).astype(o_ref.dtype)


def _mixer0(u, batch, pool_w, pool_scale, wb, wc, pwr, pwi, d, w_glu):
    m, width = u.shape
    nj = SSM_WIDTH // LANES
    sw = GROUPS_PER_TILE * SSM_STATE
    chunks = T_PAD // L_SSM
    return pl.pallas_call(
        _mixer0_kernel,
        grid=(batch, chunks),
        in_specs=[pl.BlockSpec((L_SSM, width), lambda b, c: (b * chunks + c, 0)),
                  _const_spec((len(POOL_WINDOWS), POOL_GROUP, POOL_GROUP)), _const_spec((1, POOL_WIDTH)),
                  _const_spec((nj, LANES, 2 * sw)), _const_spec((nj, 2 * sw, LANES)),
                  _const_spec((nj, 2, sw)), _const_spec((nj, 2, sw)),
                  _const_spec((1, SSM_WIDTH)), _const_spec((SSM_WIDTH, SSM_WIDTH))],
        out_specs=pl.BlockSpec((L_SSM, width), lambda b, c: (b * chunks + c, 0)),
        out_shape=jax.ShapeDtypeStruct((m, width), BF16),
        scratch_shapes=[pltpu.VMEM((POOL_HALO + L_SSM, POOL_WIDTH), F32),
                        pltpu.VMEM((nj, L_SSM, LANES), F32),
                        pltpu.VMEM((nj, L_SSM, LANES), F32),
                        pltpu.VMEM((TILES_PER_SCAN, L_SSM, 2 * sw), F32),
                        pltpu.VMEM((nj, 2, sw), F32)],
        compiler_params=_params(("parallel", "arbitrary")),
        name="mixer0",
    )(u, pool_w, pool_scale, wb, wc, pwr, pwi, d, w_glu)


CHUNK_SHIFT = CHUNK.bit_length() - 1
assert 1 << CHUNK_SHIFT == CHUNK
KB = 512
KB_SHIFT = KB.bit_length() - 1
assert 1 << KB_SHIFT == KB
N_KB = -(-T_PAD // KB)
COUNT_ROWS = 64
PAIRS_PER_DOT = 2
ONES_ROWS = 16
VT_ROWS = HEAD_DIM + ONES_ROWS


def _chunk_id(t):
    return jnp.where(t < N_META, 0, ((t - N_META) >> CHUNK_SHIFT) + 1)


def _fold_rows(x, op):
    return op(x.reshape(x.shape[0] // SUBLANES, SUBLANES, x.shape[1]), axis=0)


def _stage_rows(src, dst):
    for kb in range(N_KB):
        n = min(KB, T_PAD - kb * KB)
        dst[kb, 0:n, :] = src[kb * KB:kb * KB + n, :].astype(dst.dtype)
        if n < KB:
            dst[kb, n:KB, :] = jnp.zeros((KB - n, dst.shape[2]), dst.dtype)


def _stage_values(src, dst):
    for kb in range(N_KB):
        n = min(KB, T_PAD - kb * KB)
        for kv in range(N_KV_HEADS):
            r0 = kv * VT_ROWS
            dst[kb, r0:r0 + HEAD_DIM, 0:n] = src[kv * HEAD_DIM:(kv + 1) * HEAD_DIM, kb * KB:kb * KB + n]
            if n < KB:
                dst[kb, r0:r0 + HEAD_DIM, n:KB] = jnp.zeros((HEAD_DIM, KB - n), dst.dtype)
            dst[kb, r0 + HEAD_DIM:r0 + VT_ROWS, :] = jnp.ones((ONES_ROWS, KB), dst.dtype)


def _for_key_blocks(n_kb, body):
    def pair(p, carry):
        body(2 * p)
        body(2 * p + 1)
        return carry

    lax.fori_loop(0, n_kb >> 1, pair, 0)

    @pl.when((n_kb & 1) == 1)
    def _():
        body(n_kb - 1)


def _dsa_kernel(qi_ref, wq_ref, q_ref, kw_ref, k_ref, vt_ref, o_ref,
                kie, kio, kbuf, vtbuf, score_buf, lg_buf, m_acc, o_acc):
    i = pl.program_id(1)
    tq = q_ref.shape[0]
    nt = (((1,), (1,)), ((), ()))

    @pl.when(i == 0)
    def _():
        kw = kw_ref[...]
        lane = lax.broadcasted_iota(jnp.int32, kw.shape, 1)
        _stage_rows(jnp.where(lane < IDX_DIM, kw, 0.0), kie)
        _stage_rows(jnp.where(lane >= IDX_DIM, pltpu.roll(kw, IDX_DIM, 1), 0.0), kio)
        _stage_rows(k_ref, kbuf)
        _stage_values(vt_ref, vtbuf)

    last_q = (i + 1) * tq - 1 - PAD_FRONT
    key_end = N_META + CHUNK * (((last_q - N_META) >> CHUNK_SHIFT) + 1)
    n_kb = jnp.minimum((key_end + PAD_FRONT + KB - 1) >> KB_SHIFT, N_KB)

    w_t = wq_ref[...].T * ((IDX_HEADS ** -0.5) * (IDX_DIM ** -0.5))
    pairs = IDX_HEADS // 2
    qpos = i * tq + lax.broadcasted_iota(jnp.int32, (1, tq), 1) - PAD_FRONT

    def index_block(kb):
        score = jnp.zeros((KB, tq), F32)
        for p0 in range(0, pairs, PAIRS_PER_DOT):
            qg = qi_ref[p0:p0 + PAIRS_PER_DOT].reshape(PAIRS_PER_DOT * tq, LANES)
            se = lax.dot_general(kie[kb], qg, nt, preferred_element_type=F32)
            so = lax.dot_general(kio[kb], qg, nt, preferred_element_type=F32)
            for c in range(PAIRS_PER_DOT):
                p = p0 + c
                we = w_t[IDX_DIM + 2 * p:IDX_DIM + 2 * p + 1, :]
                wo = w_t[IDX_DIM + 2 * p + 1:IDX_DIM + 2 * p + 2, :]
                score = score + we * jnp.maximum(se[:, c * tq:(c + 1) * tq], 0.0)
                score = score + wo * jnp.maximum(so[:, c * tq:(c + 1) * tq], 0.0)
        kpos = kb * KB + lax.broadcasted_iota(jnp.int32, (KB, 1), 0) - PAD_FRONT
        allowed = (_chunk_id(kpos) <= _chunk_id(qpos)) & (kpos >= 0) & (kpos < T_REAL)
        score_buf[kb] = jnp.where(allowed, score, -jnp.inf)

    _for_key_blocks(n_kb, index_block)

    def key_to_float(key):
        return pltpu.bitcast(key ^ ((key >> 31) & 0x7FFFFFFF), F32)

    def count_where(pred):
        def block(kb, acc):
            ind = jnp.where(pred(kb), 1.0, 0.0)
            return acc + jnp.sum(ind.reshape(KB // COUNT_ROWS, COUNT_ROWS, tq), axis=0)
        acc = lax.fori_loop(0, n_kb, block, jnp.zeros((COUNT_ROWS, tq), F32))
        return jnp.sum(acc, axis=0, keepdims=True)

    def count_ge(cand):
        cand_f = key_to_float(cand)
        return count_where(lambda kb: score_buf[kb] >= cand_f)

    c0 = count_ge(jnp.zeros((1, tq), jnp.int32))
    th = jnp.where(c0 >= TOPK, 0, INT_MIN).astype(jnp.int32)
    n_ge = jnp.where(c0 >= TOPK, c0, float(N_KB * KB))

    def search(j, carry):
        th, n_ge = carry
        cand = th | jnp.left_shift(jnp.int32(1), 30 - j)
        c = count_ge(cand)
        return jnp.where(c >= TOPK, cand, th), jnp.where(c >= TOPK, c, n_ge)

    th, n_ge = lax.fori_loop(0, 31, search, (th, n_ge))
    tied = (n_ge > TOPK) & (th > NEG_INF_KEY) & (qpos >= 0)
    th_key = jnp.maximum(th, NEG_INF_KEY + 1)
    th = key_to_float(th_key)

    @pl.when(jnp.max(tied.astype(jnp.int32)) > 0)
    def _():
        th_up = key_to_float(th_key + 1)
        keep = TOPK - count_where(lambda kb: score_buf[kb] >= th_up)

        def residual(kb, carry):
            score = score_buf[kb]
            at_th = (score >= th) & (score < th_up) & tied
            lg_buf[0, kb, :, 0:tq] = jnp.where(at_th, score - th, -1.0)
            return carry

        lax.fori_loop(0, n_kb, residual, 0)
        resid = lambda kb: lg_buf[0, kb, :, 0:tq]

        def search_residual(j, carry):
            key, n = carry
            cand = key | jnp.left_shift(jnp.int32(1), 30 - j)
            cand_f = pltpu.bitcast(cand, F32)
            c = count_where(lambda kb: resid(kb) >= cand_f)
            return jnp.where(c >= keep, cand, key), jnp.where(c >= keep, c, n)

        n_at = count_where(lambda kb: resid(kb) >= 0.0)
        r_key, _ = lax.fori_loop(0, 31, search_residual, (jnp.zeros((1, tq), jnp.int32), n_at))
        r_th = pltpu.bitcast(r_key, F32)
        keep_eq = keep - count_where(lambda kb: resid(kb) > r_th)
        lower = (lax.broadcasted_iota(jnp.int32, (KB, KB), 0) >= lax.broadcasted_iota(jnp.int32, (KB, KB), 1))
        lower = jnp.where(lower, 1.0, 0.0).astype(BF16)

        def drop(kb, seen):
            r = resid(kb)
            eq = r == r_th
            rank = seen + jnp.dot(lower, jnp.where(eq, 1.0, 0.0).astype(BF16), preferred_element_type=F32)
            gone = ((r >= 0.0) & (r < r_th)) | (eq & (rank > keep_eq))
            score_buf[kb] = jnp.where(gone, -jnp.inf, score_buf[kb])
            return rank[KB - 1:KB, :]

        lax.fori_loop(0, n_kb, drop, jnp.zeros((1, tq), F32))

    qs = []
    for kv in range(N_KV_HEADS):
        h0 = kv * KV_GROUP
        qs.append(jnp.concatenate([q_ref[:, (h0 + g) * HEAD_DIM:(h0 + g + 1) * HEAD_DIM]
                                   for g in range(KV_GROUP)], axis=0))
    m_acc[...] = jnp.full(m_acc.shape, -jnp.inf, F32)
    o_acc[...] = jnp.zeros(o_acc.shape, F32)

    def logits_block(kb):
        sel = score_buf[kb] >= th
        for kv in range(N_KV_HEADS):
            kk = kbuf[kb, :, kv * HEAD_DIM:(kv + 1) * HEAD_DIM]
            lg = lax.dot_general(kk, qs[kv], nt, preferred_element_type=F32)
            lg = jnp.concatenate([jnp.where(sel, lg[:, g * tq:(g + 1) * tq], -jnp.inf)
                                  for g in range(KV_GROUP)], axis=1)
            lg_buf[kv, kb] = lg
            m_acc[kv] = jnp.maximum(m_acc[kv], _fold_rows(lg, jnp.max))

    _for_key_blocks(n_kb, logits_block)
    mx = [jnp.max(m_acc[kv], axis=0, keepdims=True) for kv in range(N_KV_HEADS)]

    def value_block(kb):
        for kv in range(N_KV_HEADS):
            e = jnp.exp2((lg_buf[kv, kb] - mx[kv]).astype(BF16))
            vt = vtbuf[kb, kv * VT_ROWS:(kv + 1) * VT_ROWS, :]
            o_acc[kv] += jnp.dot(vt, e, preferred_element_type=F32)

    _for_key_blocks(n_kb, value_block)
    for kv in range(N_KV_HEADS):
        out_t = o_acc[kv, 0:HEAD_DIM, :] / o_acc[kv, HEAD_DIM:HEAD_DIM + 1, :]
        for g in range(KV_GROUP):
            hd = kv * KV_GROUP + g
            o_ref[:, hd * HEAD_DIM:(hd + 1) * HEAD_DIM] = out_t[:, g * tq:(g + 1) * tq].T.astype(o_ref.dtype)


def _dsa(qi, kw, q, k, vt, batch):
    m = q.shape[0]
    nq = T_PAD // TQ
    pairs = qi.shape[0]
    cols = KV_GROUP * TQ
    return pl.pallas_call(
        _dsa_kernel,
        grid=(batch, nq),
        in_specs=[pl.BlockSpec((pairs, TQ, LANES), lambda b, i: (0, b * nq + i, 0)),
                  pl.BlockSpec((TQ, LANES), lambda b, i: (b * nq + i, 0)),
                  pl.BlockSpec((TQ, Q_W), lambda b, i: (b * nq + i, 0)),
                  pl.BlockSpec((T_PAD, LANES), lambda b, i: (b, 0)),
                  pl.BlockSpec((T_PAD, KV_W), lambda b, i: (b, 0)),
                  pl.BlockSpec((KV_W, T_PAD), lambda b, i: (0, b))],
        out_specs=pl.BlockSpec((TQ, Q_W), lambda b, i: (b * nq + i, 0)),
        out_shape=jax.ShapeDtypeStruct((m, Q_W), BF16),
        scratch_shapes=[pltpu.VMEM((N_KB, KB, LANES), BF16), pltpu.VMEM((N_KB, KB, LANES), BF16),
                        pltpu.VMEM((N_KB, KB, KV_W), BF16), pltpu.VMEM((N_KB, N_KV_HEADS * VT_ROWS, KB), BF16),
                        pltpu.VMEM((N_KB, KB, TQ), F32),
                        pltpu.VMEM((N_KV_HEADS, N_KB, KB, cols), F32),
                        pltpu.VMEM((N_KV_HEADS, SUBLANES, cols), F32),
                        pltpu.VMEM((N_KV_HEADS, VT_ROWS, cols), F32)],
        compiler_params=_params(("parallel", "arbitrary")),
        name="dsa",
    )(qi, kw, q, kw, k, vt)


def kernel(x, meta, norm_g, ab_w_in, pool_w, pool_scale, s5_lambda_re, s5_lambda_im, s5_log_dt,
           s5_b_re, s5_b_im, s5_c_re, s5_c_im, s5_d, s5_w_glu, ab_w_out, c_w_in, c_w_out, mlp_w1, mlp_w2):
    batch, seq, d = x.shape
    assert (seq + N_META, d) == (T_REAL, D_MODEL)
    depth = norm_g.shape[0]
    meta_b = jnp.broadcast_to(meta[None].astype(x.dtype), (batch, N_META, d))
    pad = jnp.zeros((batch, T_PAD - T_REAL, d), x.dtype)
    h = jnp.concatenate([pad, meta_b, x], axis=1).reshape(batch * T_PAD, d)

    rope128 = _rope_tables(HEAD_DIM, T_PAD, -PAD_FRONT, batch)
    rope64 = _rope_tables(IDX_DIM, T_PAD, -PAD_FRONT, batch)
    w1, w2 = mlp_w1.astype(BF16), mlp_w2.astype(BF16)

    for layer in range(depth):
        g = norm_g[layer]
        if layer % 2 == 0:
            e = layer // 2
            u = _norm_proj(h, g[0], ab_w_in[e].astype(BF16))
            bbr, bbi, pwr, pwi = _ssm_prep(s5_lambda_re[e], s5_lambda_im[e], s5_log_dt[e], s5_b_re[e], s5_b_im[e])
            nj = SSM_GROUPS // GROUPS_PER_TILE
            tile_major = lambda p: p.reshape(2, nj, GROUPS_PER_TILE * SSM_STATE).swapaxes(0, 1)
            wb = jnp.concatenate([_block_diag_in(bbr), _block_diag_in(bbi)], axis=-1).astype(BF16)
            wc = jnp.concatenate([_block_diag_out(s5_c_re[e]), -_block_diag_out(s5_c_im[e])], axis=1).astype(BF16)
            mix = _mixer0(u, batch, pool_w[e].astype(BF16), pool_scale[e].reshape(1, POOL_WIDTH), wb, wc,
                          tile_major(pwr), tile_major(pwi),
                          s5_d[e].reshape(1, SSM_WIDTH), s5_w_glu[e].astype(BF16))
            w_out = ab_w_out[e]
        else:
            o = layer // 2
            n_in = c_w_in.shape[-1]
            w_in = jnp.pad(c_w_in[o], ((0, 0), (0, -n_in % LANES))).astype(BF16)
            w_vt = w_in[:, Q_W + KV_W:Q_W + 2 * KV_W].T
            q, k, vt, qi, kw = _dsa_proj(h, g[0], w_in, w_vt, rope128, rope64)
            mix = _dsa(qi, kw, q, k, vt, batch)
            w_out = c_w_out[o]
        h = _matmul_norm_res(mix, w_out.astype(BF16), g[1], h)
        h = _mlp(h, g[2], g[3], w1, w2, layer)
    return h.reshape(batch, T_PAD, d)[:, PAD_FRONT + N_META:]
```

```python
import functools

import jax
import jax.numpy as jnp
from jax import lax
from jax.experimental import pallas as pl
from jax.experimental.pallas import tpu as pltpu

F32 = jnp.float32
BF16 = jnp.bfloat16

D_MODEL = 2048
CHUNK = 64
N_META = 16
POOL_WIDTH = 1024
POOL_WINDOWS = (2, 4, 8, 16)
POOL_GROUP = 256
POOL_HALO = 16
SSM_WIDTH = 1024
SSM_GROUP_CH = 16
SSM_GROUPS = 64
SSM_STATE = 64
GROUPS_PER_TILE = 8
HEAD_DIM = 128
N_HEADS = 16
N_KV_HEADS = 4
KV_GROUP = 4
IDX_HEADS = 16
IDX_DIM = 64
ROPE_THETA = 500000.0
ROPE_FRAC = 4
TOPK = 256
EPS = 1e-6
Q_W = N_HEADS * HEAD_DIM
KV_W = N_KV_HEADS * HEAD_DIM
IQ_W = IDX_HEADS * IDX_DIM

LANES = 128
SUBLANES = 8
VMEM_LIMIT = 56 * 1024 * 1024

T_REAL = N_META + 2048
T_PAD = 2176
PAD_FRONT = T_PAD - T_REAL
TM = 512
TQ = 128
FRAME_TILE = (PAD_FRONT + N_META) // TQ
assert FRAME_TILE * TQ == PAD_FRONT + N_META
L_SSM = 544
SCAN_ROWS = L_SSM // 8
assert SCAN_ROWS * 8 == L_SSM
TILES_PER_SCAN = 2
TF = 1024

LOG2_E = 1.4426950408889634
INT_MIN = -2 ** 31
NEG_INF_KEY = -2139095041


def _params(sem):
    return pltpu.CompilerParams(dimension_semantics=sem, vmem_limit_bytes=VMEM_LIMIT)


def _rms(x, g):
    return x * lax.rsqrt(jnp.mean(x * x, axis=-1, keepdims=True) + EPS) * g


def _const_spec(shape):
    zeros = (0,) * len(shape)
    return pl.BlockSpec(shape, lambda *_: zeros, pipeline_mode=pl.Buffered(1))


def _norm_proj_kernel(h_ref, g_ref, w_ref, o_ref):
    a = _rms(h_ref[...], g_ref[...]).astype(BF16)
    o_ref[...] = jnp.dot(a, w_ref[...], preferred_element_type=F32)


def _norm_proj(h, g, w):
    m, d = h.shape
    n = w.shape[1]
    return pl.pallas_call(
        _norm_proj_kernel,
        grid=(m // TM,),
        in_specs=[pl.BlockSpec((TM, d), lambda i: (i, 0)), _const_spec((1, d)), _const_spec((d, n))],
        out_specs=pl.BlockSpec((TM, n), lambda i: (i, 0)),
        out_shape=jax.ShapeDtypeStruct((m, n), F32),
        compiler_params=_params(("parallel",)),
        name="norm_proj",
    )(h, g.reshape(1, d), w)


def _rope_tables(head_dim, rows_per_batch, first_pos, batch):
    r = head_dim // ROPE_FRAC
    half = r // 2
    inv = ROPE_THETA ** (-jnp.arange(half, dtype=F32) / half)
    ang = (jnp.arange(rows_per_batch) + first_pos).astype(F32)[:, None] * inv[None, :]
    cos, sin = jnp.cos(ang), jnp.sin(ang)
    ones = jnp.ones((rows_per_batch, head_dim - r), F32)
    zeros_h = jnp.zeros((rows_per_batch, half), F32)
    zeros_t = jnp.zeros((rows_per_batch, head_dim - r), F32)
    c = jnp.concatenate([cos, cos, ones], axis=1)
    s_up = jnp.concatenate([zeros_h, sin, zeros_t], axis=1)
    s_dn = jnp.concatenate([-sin, zeros_h, zeros_t], axis=1)
    reps = (batch, LANES // head_dim)
    return half, jnp.tile(c, reps), jnp.tile(s_up, reps), jnp.tile(s_dn, reps)


def _rope(ys, half, c, s_up, s_dn):
    return ys * c + pltpu.roll(ys, half, 1) * s_up + pltpu.roll(ys, LANES - half, 1) * s_dn


def _dsa_proj_kernel(h_ref, g_ref, w_ref, wvt_ref, c128_ref, u128_ref, d128_ref, c64_ref, u64_ref, d64_ref,
                     q_ref, k_ref, vt_ref, qi_ref, kw_ref, *, half128, half64):
    a = _rms(h_ref[...], g_ref[...]).astype(BF16)
    t128 = (half128, c128_ref[...], u128_ref[...], d128_ref[...])
    t64 = (half64, c64_ref[...], u64_ref[...], d64_ref[...])
    c0, c1, c2, c3 = Q_W, Q_W + KV_W, Q_W + 2 * KV_W, Q_W + 2 * KV_W + IQ_W

    def slabs(lo, hi):
        y = jnp.dot(a, w_ref[:, lo:hi], preferred_element_type=F32)
        return [y[:, s * LANES:(s + 1) * LANES] for s in range((hi - lo) // LANES)]

    for s, ys in enumerate(slabs(0, c0)):
        q_ref[:, s * LANES:(s + 1) * LANES] = (_rope(ys, *t128) * (HEAD_DIM ** -0.5 * LOG2_E)).astype(q_ref.dtype)
    for s, ys in enumerate(slabs(c0, c1)):
        k_ref[:, s * LANES:(s + 1) * LANES] = _rope(ys, *t128).astype(k_ref.dtype)
    vt_ref[...] = lax.dot_general(wvt_ref[...], a, (((1,), (1,)), ((), ())),
                                  preferred_element_type=F32).astype(vt_ref.dtype)
    for s, ys in enumerate(slabs(c2, c3)):
        qi_ref[s] = _rope(ys, *t64).astype(qi_ref.dtype)
    (ys,) = slabs(c3, c3 + LANES)
    lane = lax.broadcasted_iota(jnp.int32, ys.shape, 1)
    kw_ref[...] = jnp.where(lane < IDX_DIM, _rope(ys, *t64), ys)


def _dsa_proj(h, g, w, w_vt, rope128, rope64):
    m, d = h.shape
    n = w.shape[1]
    pairs = IQ_W // LANES
    row = lambda width: pl.BlockSpec((TM, width), lambda i: (i, 0))
    return pl.pallas_call(
        functools.partial(_dsa_proj_kernel, half128=rope128[0], half64=rope64[0]),
        grid=(m // TM,),
        in_specs=[row(d), _const_spec((1, d)), _const_spec((d, n)), _const_spec((KV_W, d))] + [row(LANES)] * 6,
        out_specs=[row(Q_W), row(KV_W), pl.BlockSpec((KV_W, TM), lambda i: (0, i)),
                   pl.BlockSpec((pairs, TM, LANES), lambda i: (0, i, 0)), row(LANES)],
        out_shape=[jax.ShapeDtypeStruct((m, Q_W), BF16), jax.ShapeDtypeStruct((m, KV_W), BF16),
                   jax.ShapeDtypeStruct((KV_W, m), BF16), jax.ShapeDtypeStruct((pairs, m, LANES), BF16),
                   jax.ShapeDtypeStruct((m, LANES), F32)],
        compiler_params=_params(("parallel",)),
        name="dsa_proj",
    )(h, g.reshape(1, d), w, w_vt, *rope128[1:], *rope64[1:])


def _matmul_norm_res_kernel(a_ref, w_ref, g_ref, r_ref, o_ref):
    y = jnp.dot(a_ref[...], w_ref[...], preferred_element_type=F32)
    o_ref[...] = r_ref[...] + _rms(y, g_ref[...])


def _matmul_norm_res(a, w, g, res, res_rows=None):
    m, k = a.shape
    n = w.shape[1]
    if res_rows is None:
        tiles = m // TM
        res_spec = pl.BlockSpec((TM, n), lambda b, j: (j, 0))
    else:
        rows, first = res_rows
        tiles = (rows - first) // TM
        assert tiles * TM == rows - first
        assert rows % SUBLANES == first % SUBLANES == 0
        res_spec = pl.BlockSpec((pl.Element(TM), pl.Element(n)),
                                lambda b, j: (pl.multiple_of(b * rows + first + j * TM, SUBLANES), 0))
    row = lambda width: pl.BlockSpec((TM, width), lambda b, j: (b * tiles + j, 0))
    return pl.pallas_call(
        _matmul_norm_res_kernel,
        grid=(m // (tiles * TM), tiles),
        in_specs=[row(k), _const_spec((k, n)), _const_spec((1, n)), res_spec],
        out_specs=row(n),
        out_shape=jax.ShapeDtypeStruct((m, n), F32),
        compiler_params=_params(("parallel", "parallel")),
        name="matmul_norm_res",
    )(a, w, g.reshape(1, n), res)


def _mlp_kernel(h_ref, g_in_ref, g_out_ref, w1_ref, w2_ref, o_ref, a_ref, acc_ref):
    f = pl.program_id(1)

    @pl.when(f == 0)
    def _():
        a_ref[...] = _rms(h_ref[...], g_in_ref[...]).astype(BF16)
        acc_ref[...] = jnp.zeros_like(acc_ref)

    z = jnp.dot(a_ref[...], w1_ref[...], preferred_element_type=F32)
    z = jnp.square(jnp.maximum(z, 0.0)).astype(BF16)
    acc_ref[...] += jnp.dot(z, w2_ref[...], preferred_element_type=F32)

    @pl.when(f == pl.num_programs(1) - 1)
    def _():
        o_ref[...] = h_ref[...] + _rms(acc_ref[...], g_out_ref[...])


def _mlp(h, g_in, g_out, w1, w2, layer):
    m, d = h.shape
    ff = w1.shape[2]
    return pl.pallas_call(
        _mlp_kernel,
        grid=(m // TM, ff // TF),
        in_specs=[pl.BlockSpec((TM, d), lambda i, f: (i, 0)),
                  _const_spec((1, d)), _const_spec((1, d)),
                  pl.BlockSpec((None, d, TF), lambda i, f: (layer, 0, f)),
                  pl.BlockSpec((None, TF, d), lambda i, f: (layer, f, 0))],
        out_specs=pl.BlockSpec((TM, d), lambda i, f: (i, 0)),
        out_shape=jax.ShapeDtypeStruct((m, d), F32),
        scratch_shapes=[pltpu.VMEM((TM, d), BF16), pltpu.VMEM((TM, d), F32)],
        compiler_params=_params(("parallel", "arbitrary")),
        name="mlp",
    )(h, g_in.reshape(1, d), g_out.reshape(1, d), w1, w2)


def _cmul(ar, ai, br, bi):
    return ar * br - ai * bi, ar * bi + ai * br


def _discretise(ar, ai, log_dt):
    dt = jnp.exp(log_dt)
    mag = jnp.exp(dt * ar)
    return mag * jnp.cos(dt * ai), mag * jnp.sin(dt * ai)


def _ssm_prep_kernel(lre_ref, lim_ref, ldt_ref, lre2_ref, lim2_ref, ldt2_ref, bre_ref, bim_ref,
                     bbr_ref, bbi_ref, pwr_ref, pwi_ref):
    ar, ai = lre_ref[...], lim_ref[...]
    abr, abi = _discretise(ar, ai, ldt_ref[...])
    den = ar * ar + ai * ai
    zr, zi = abr - 1.0, abi
    fr = (zr * ar + zi * ai) / den
    fi = (zi * ar - zr * ai) / den
    br, bi = bre_ref[...], bim_ref[...]
    bbr_ref[...] = fr * br - fi * bi
    bbi_ref[...] = fr * bi + fi * br
    a1r, a1i = _discretise(lre2_ref[...], lim2_ref[...], ldt2_ref[...])
    pr, pi = a1r, a1i
    for _ in range(SCAN_ROWS - 1):
        pr, pi = _cmul(pr, pi, a1r, a1i)
    pwr_ref[0], pwi_ref[0] = a1r, a1i
    pwr_ref[1], pwi_ref[1] = pr, pi


def _ssm_prep(lam_re, lam_im, log_dt, b_re, b_im):
    g, n = lam_re.shape
    c = b_re.shape[-1]
    bt = lambda b: jnp.swapaxes(b, 1, 2)
    full = lambda *s: pl.BlockSpec(s, lambda: (0,) * len(s))
    return pl.pallas_call(
        _ssm_prep_kernel,
        in_specs=[full(g, 1, n), full(g, 1, n), full(g, 1, 1), full(g, n), full(g, n), full(g, 1),
                  full(g, c, n), full(g, c, n)],
        out_specs=[full(g, c, n), full(g, c, n), full(2, g, n), full(2, g, n)],
        out_shape=[jax.ShapeDtypeStruct((g, c, n), F32)] * 2 + [jax.ShapeDtypeStruct((2, g, n), F32)] * 2,
        name="ssm_prep",
    )(lam_re.reshape(g, 1, n), lam_im.reshape(g, 1, n), log_dt.reshape(g, 1, 1), lam_re, lam_im,
      log_dt.reshape(g, 1), bt(b_re), bt(b_im))


def _block_diag_in(bb):
    g, c, n = bb.shape
    r = bb.reshape(g // GROUPS_PER_TILE, GROUPS_PER_TILE, c, n)
    eye = jnp.eye(GROUPS_PER_TILE, dtype=bb.dtype)
    return jnp.einsum('jgcn,gh->jgchn', r, eye).reshape(g // GROUPS_PER_TILE, GROUPS_PER_TILE * c, GROUPS_PER_TILE * n)


def _block_diag_out(cc):
    g, c, n = cc.shape
    r = cc.reshape(g // GROUPS_PER_TILE, GROUPS_PER_TILE, c, n)
    eye = jnp.eye(GROUPS_PER_TILE, dtype=cc.dtype)
    return jnp.einsum('jgcn,gh->jgnhc', r, eye).reshape(g // GROUPS_PER_TILE, GROUPS_PER_TILE * n, GROUPS_PER_TILE * c)


def _row_on_sublanes(ref, j, k):
    return jnp.broadcast_to(ref[j, pl.ds(k, 1), :], (SUBLANES, ref.shape[2]))


def _mixer0_kernel(u_ref, pw_ref, ps_ref, wb_ref, wc_ref, pwr_ref, pwi_ref, d_ref, wg_ref, o_ref,
                   pbuf, uperm, ynat, sbuf, carry_ref):
    c = pl.program_id(1)
    L = u_ref.shape[0]
    sw = GROUPS_PER_TILE * SSM_STATE

    @pl.when(c == 0)
    def _():
        pbuf[0:POOL_HALO, :] = jnp.zeros((POOL_HALO, POOL_WIDTH), F32)
        carry_ref[...] = jnp.zeros_like(carry_ref)

    pbuf[POOL_HALO:POOL_HALO + L, :] = u_ref[:, 0:POOL_WIDTH]
    tpos = jnp.maximum(c * L + lax.broadcasted_iota(jnp.int32, (L, 1), 0) - PAD_FRONT, 0)
    for g, win in enumerate(POOL_WINDOWS):
        sl = slice(g * POOL_GROUP, (g + 1) * POOL_GROUP)
        x = u_ref[:, sl]
        s = x
        for k in range(1, win):
            s = s + pbuf[POOL_HALO - k:POOL_HALO - k + L, sl]
        cnt = jnp.minimum(tpos + 1, win).astype(F32)
        diff = (s / cnt - x).astype(BF16)
        y = jnp.dot(diff, pw_ref[g], preferred_element_type=F32) * ps_ref[:, sl]
        o_ref[:, sl] = y.astype(o_ref.dtype)
    pbuf[0:POOL_HALO, :] = pbuf[L:L + POOL_HALO, :]

    n_tiles = SSM_WIDTH // LANES
    for j in range(n_tiles):
        ynat[j] = u_ref[:, POOL_WIDTH + j * LANES:POOL_WIDTH + (j + 1) * LANES]

    def permute(k, carry):
        for j in range(n_tiles):
            uperm[j, pl.ds(pl.multiple_of(k * SUBLANES, SUBLANES), SUBLANES), :] = \
                ynat[j, pl.ds(k, SUBLANES, stride=SCAN_ROWS), :]
        return carry

    lax.fori_loop(0, SCAN_ROWS, permute, 0)
    row = lax.broadcasted_iota(jnp.int32, (SUBLANES, sw), 0)
    for j0 in range(0, n_tiles, TILES_PER_SCAN):
        tiles = range(j0, j0 + TILES_PER_SCAN)
        for j in tiles:
            sbuf[j - j0] = jnp.dot(uperm[j].astype(BF16), wb_ref[j], preferred_element_type=F32)
        a1 = [(_row_on_sublanes(pwr_ref, j, 0), _row_on_sublanes(pwi_ref, j, 0)) for j in tiles]

        def scan_step(k, xs, store):
            rows = pl.ds(pl.multiple_of(k * SUBLANES, SUBLANES), SUBLANES)
            out = []
            for c, ((a1r, a1i), (xr, xi)) in enumerate(zip(a1, xs)):
                xr, xi = (a1r * xr - a1i * xi + sbuf[c, rows, 0:sw], a1r * xi + a1i * xr + sbuf[c, rows, sw:2 * sw])
                if store:
                    sbuf[c, rows, 0:sw] = xr
                    sbuf[c, rows, sw:2 * sw] = xi
                out.append((xr, xi))
            return tuple(out)

        zero = jnp.zeros((SUBLANES, sw), F32)
        ends = lax.fori_loop(0, SCAN_ROWS, functools.partial(scan_step, store=False),
                             tuple((zero, zero) for _ in tiles))

        starts = []
        for j, (er, ei) in zip(tiles, ends):
            cr = _row_on_sublanes(carry_ref, j, 0)
            ci = _row_on_sublanes(carry_ref, j, 1)
            sr = jnp.where(row == 0, cr, pltpu.roll(er, 1, 0))
            si = jnp.where(row == 0, ci, pltpu.roll(ei, 1, 0))
            qr = _row_on_sublanes(pwr_ref, j, 1)
            qi = _row_on_sublanes(pwi_ref, j, 1)
            for k in (1, 2, 4):
                mr, mi = jnp.where(row >= k, qr, 0.0), jnp.where(row >= k, qi, 0.0)
                tr, ti = pltpu.roll(sr, k, 0), pltpu.roll(si, k, 0)
                sr, si = sr + mr * tr - mi * ti, si + mr * ti + mi * tr
                qr, qi = _cmul(qr, qi, qr, qi)
            starts.append((sr, si))

        lax.fori_loop(0, SCAN_ROWS, functools.partial(scan_step, store=True), tuple(starts))
        for j in tiles:
            c = j - j0
            carry_ref[j, 0:1, :] = sbuf[c, L - 1:L, 0:sw]
            carry_ref[j, 1:2, :] = sbuf[c, L - 1:L, sw:2 * sw]
            yj = jnp.dot(sbuf[c].astype(BF16), wc_ref[j], preferred_element_type=F32)
            uperm[j] = yj + d_ref[:, j * LANES:(j + 1) * LANES] * uperm[j]

    def unpermute(k, carry):
        for j in range(n_tiles):
            ynat[j, pl.ds(k, SUBLANES, stride=SCAN_ROWS), :] = \
                uperm[j, pl.ds(pl.multiple_of(k * SUBLANES, SUBLANES), SUBLANES), :]
        return carry

    lax.fori_loop(0, SCAN_ROWS, unpermute, 0)
    y = jax.nn.gelu(jnp.concatenate([ynat[j] for j in range(n_tiles)], axis=1))
    gate = jax.nn.sigmoid(jnp.dot(y.astype(BF16), wg_ref[...], preferred_element_type=F32))
    o_ref[:, POOL_WIDTH:] = (y * gate).astype(o_ref.dtype)


def _mixer0(u, batch, pool_w, pool_scale, wb, wc, pwr, pwi, d, w_glu):
    m, width = u.shape
    nj = SSM_WIDTH // LANES
    sw = GROUPS_PER_TILE * SSM_STATE
    chunks = T_PAD // L_SSM
    return pl.pallas_call(
        _mixer0_kernel,
        grid=(batch, chunks),
        in_specs=[pl.BlockSpec((L_SSM, width), lambda b, c: (b * chunks + c, 0)),
                  _const_spec((len(POOL_WINDOWS), POOL_GROUP, POOL_GROUP)), _const_spec((1, POOL_WIDTH)),
                  _const_spec((nj, LANES, 2 * sw)), _const_spec((nj, 2 * sw, LANES)),
                  _const_spec((nj, 2, sw)), _const_spec((nj, 2, sw)),
                  _const_spec((1, SSM_WIDTH)), _const_spec((SSM_WIDTH, SSM_WIDTH))],
        out_specs=pl.BlockSpec((L_SSM, width), lambda b, c: (b * chunks + c, 0)),
        out_shape=jax.ShapeDtypeStruct((m, width), BF16),
        scratch_shapes=[pltpu.VMEM((POOL_HALO + L_SSM, POOL_WIDTH), F32),
                        pltpu.VMEM((nj, L_SSM, LANES), F32),
                        pltpu.VMEM((nj, L_SSM, LANES), F32),
                        pltpu.VMEM((TILES_PER_SCAN, L_SSM, 2 * sw), F32),
                        pltpu.VMEM((nj, 2, sw), F32)],
        compiler_params=_params(("parallel", "arbitrary")),
        name="mixer0",
    )(u, pool_w, pool_scale, wb, wc, pwr, pwi, d, w_glu)


CHUNK_SHIFT = CHUNK.bit_length() - 1
assert 1 << CHUNK_SHIFT == CHUNK
KB = 512
KB_SHIFT = KB.bit_length() - 1
assert 1 << KB_SHIFT == KB
N_KB = -(-T_PAD // KB)
COUNT_ROWS = 64
PAIRS_PER_DOT = 2
ONES_ROWS = 16
VT_ROWS = HEAD_DIM + ONES_ROWS


def _chunk_id(t):
    return jnp.where(t < N_META, 0, ((t - N_META) >> CHUNK_SHIFT) + 1)


def _fold_rows(x, op):
    return op(x.reshape(x.shape[0] // SUBLANES, SUBLANES, x.shape[1]), axis=0)


def _stage_rows(src, dst):
    for kb in range(N_KB):
        n = min(KB, T_PAD - kb * KB)
        dst[kb, 0:n, :] = src[kb * KB:kb * KB + n, :].astype(dst.dtype)
        if n < KB:
            dst[kb, n:KB, :] = jnp.zeros((KB - n, dst.shape[2]), dst.dtype)


def _stage_values(src, dst):
    for kb in range(N_KB):
        n = min(KB, T_PAD - kb * KB)
        for kv in range(N_KV_HEADS):
            r0 = kv * VT_ROWS
            dst[kb, r0:r0 + HEAD_DIM, 0:n] = src[kv * HEAD_DIM:(kv + 1) * HEAD_DIM, kb * KB:kb * KB + n]
            if n < KB:
                dst[kb, r0:r0 + HEAD_DIM, n:KB] = jnp.zeros((HEAD_DIM, KB - n), dst.dtype)
            dst[kb, r0 + HEAD_DIM:r0 + VT_ROWS, :] = jnp.ones((ONES_ROWS, KB), dst.dtype)


def _for_key_blocks(n_kb, body):
    def pair(p, carry):
        body(2 * p)
        body(2 * p + 1)
        return carry

    lax.fori_loop(0, n_kb >> 1, pair, 0)

    @pl.when((n_kb & 1) == 1)
    def _():
        body(n_kb - 1)


def _dsa_kernel(qi_ref, wq_ref, q_ref, kw_ref, k_ref, vt_ref, o_ref,
                kie, kio, kbuf, vtbuf, score_buf, lg_buf, m_acc, o_acc, *, first_tile):
    i = pl.program_id(1) + first_tile
    tq = q_ref.shape[0]
    nt = (((1,), (1,)), ((), ()))

    @pl.when(pl.program_id(1) == 0)
    def _():
        kw = kw_ref[...]
        lane = lax.broadcasted_iota(jnp.int32, kw.shape, 1)
        _stage_rows(jnp.where(lane < IDX_DIM, kw, 0.0), kie)
        _stage_rows(jnp.where(lane >= IDX_DIM, pltpu.roll(kw, IDX_DIM, 1), 0.0), kio)
        _stage_rows(k_ref, kbuf)
        _stage_values(vt_ref, vtbuf)

    last_q = (i + 1) * tq - 1 - PAD_FRONT
    key_end = N_META + CHUNK * (((last_q - N_META) >> CHUNK_SHIFT) + 1)
    n_kb = jnp.minimum((key_end + PAD_FRONT + KB - 1) >> KB_SHIFT, N_KB)

    w_t = wq_ref[...].T * ((IDX_HEADS ** -0.5) * (IDX_DIM ** -0.5))
    pairs = IDX_HEADS // 2
    qpos = i * tq + lax.broadcasted_iota(jnp.int32, (1, tq), 1) - PAD_FRONT

    def index_block(kb):
        score = jnp.zeros((KB, tq), F32)
        for p0 in range(0, pairs, PAIRS_PER_DOT):
            qg = qi_ref[p0:p0 + PAIRS_PER_DOT].reshape(PAIRS_PER_DOT * tq, LANES)
            se = lax.dot_general(kie[kb], qg, nt, preferred_element_type=F32)
            so = lax.dot_general(kio[kb], qg, nt, preferred_element_type=F32)
            for c in range(PAIRS_PER_DOT):
                p = p0 + c
                we = w_t[IDX_DIM + 2 * p:IDX_DIM + 2 * p + 1, :]
                wo = w_t[IDX_DIM + 2 * p + 1:IDX_DIM + 2 * p + 2, :]
                score = score + we * jnp.maximum(se[:, c * tq:(c + 1) * tq], 0.0)
                score = score + wo * jnp.maximum(so[:, c * tq:(c + 1) * tq], 0.0)
        kpos = kb * KB + lax.broadcasted_iota(jnp.int32, (KB, 1), 0) - PAD_FRONT
        allowed = (_chunk_id(kpos) <= _chunk_id(qpos)) & (kpos >= 0) & (kpos < T_REAL)
        score_buf[kb] = jnp.where(allowed, score, -jnp.inf)

    _for_key_blocks(n_kb, index_block)

    def key_to_float(key):
        return pltpu.bitcast(key ^ ((key >> 31) & 0x7FFFFFFF), F32)

    def count_where(pred):
        def block(kb, acc):
            ind = jnp.where(pred(kb), 1.0, 0.0)
            return acc + jnp.sum(ind.reshape(KB // COUNT_ROWS, COUNT_ROWS, tq), axis=0)
        acc = lax.fori_loop(0, n_kb, block, jnp.zeros((COUNT_ROWS, tq), F32))
        return jnp.sum(acc, axis=0, keepdims=True)

    def count_ge(cand):
        cand_f = key_to_float(cand)
        return count_where(lambda kb: score_buf[kb] >= cand_f)

    c0 = count_ge(jnp.zeros((1, tq), jnp.int32))
    th = jnp.where(c0 >= TOPK, 0, INT_MIN).astype(jnp.int32)
    n_ge = jnp.where(c0 >= TOPK, c0, float(N_KB * KB))

    def search(j, carry):
        th, n_ge = carry
        cand = th | jnp.left_shift(jnp.int32(1), 30 - j)
        c = count_ge(cand)
        return jnp.where(c >= TOPK, cand, th), jnp.where(c >= TOPK, c, n_ge)

    th, n_ge = lax.fori_loop(0, 31, search, (th, n_ge))
    tied = (n_ge > TOPK) & (th > NEG_INF_KEY) & (qpos >= 0)
    th_key = jnp.maximum(th, NEG_INF_KEY + 1)
    th = key_to_float(th_key)

    @pl.when(jnp.max(tied.astype(jnp.int32)) > 0)
    def _():
        th_up = key_to_float(th_key + 1)
        keep = TOPK - count_where(lambda kb: score_buf[kb] >= th_up)

        def residual(kb, carry):
            score = score_buf[kb]
            at_th = (score >= th) & (score < th_up) & tied
            lg_buf[0, kb, :, 0:tq] = jnp.where(at_th, score - th, -1.0)
            return carry

        lax.fori_loop(0, n_kb, residual, 0)
        resid = lambda kb: lg_buf[0, kb, :, 0:tq]

        def search_residual(j, carry):
            key, n = carry
            cand = key | jnp.left_shift(jnp.int32(1), 30 - j)
            cand_f = pltpu.bitcast(cand, F32)
            c = count_where(lambda kb: resid(kb) >= cand_f)
            return jnp.where(c >= keep, cand, key), jnp.where(c >= keep, c, n)

        n_at = count_where(lambda kb: resid(kb) >= 0.0)
        r_key, _ = lax.fori_loop(0, 31, search_residual, (jnp.zeros((1, tq), jnp.int32), n_at))
        r_th = pltpu.bitcast(r_key, F32)
        keep_eq = keep - count_where(lambda kb: resid(kb) > r_th)
        lower = (lax.broadcasted_iota(jnp.int32, (KB, KB), 0) >= lax.broadcasted_iota(jnp.int32, (KB, KB), 1))
        lower = jnp.where(lower, 1.0, 0.0).astype(BF16)

        def drop(kb, seen):
            r = resid(kb)
            eq = r == r_th
            rank = seen + jnp.dot(lower, jnp.where(eq, 1.0, 0.0).astype(BF16), preferred_element_type=F32)
            gone = ((r >= 0.0) & (r < r_th)) | (eq & (rank > keep_eq))
            score_buf[kb] = jnp.where(gone, -jnp.inf, score_buf[kb])
            return rank[KB - 1:KB, :]

        lax.fori_loop(0, n_kb, drop, jnp.zeros((1, tq), F32))

    qs = []
    for kv in range(N_KV_HEADS):
        h0 = kv * KV_GROUP
        qs.append(jnp.concatenate([q_ref[:, (h0 + g) * HEAD_DIM:(h0 + g + 1) * HEAD_DIM]
                                   for g in range(KV_GROUP)], axis=0))
    m_acc[...] = jnp.full(m_acc.shape, -jnp.inf, F32)
    o_acc[...] = jnp.zeros(o_acc.shape, F32)

    def logits_block(kb):
        sel = score_buf[kb] >= th
        for kv in range(N_KV_HEADS):
            kk = kbuf[kb, :, kv * HEAD_DIM:(kv + 1) * HEAD_DIM]
            lg = lax.dot_general(kk, qs[kv], nt, preferred_element_type=F32)
            lg = jnp.concatenate([jnp.where(sel, lg[:, g * tq:(g + 1) * tq], -jnp.inf)
                                  for g in range(KV_GROUP)], axis=1)
            lg_buf[kv, kb] = lg
            m_acc[kv] = jnp.maximum(m_acc[kv], _fold_rows(lg, jnp.max))

    _for_key_blocks(n_kb, logits_block)
    mx = [jnp.max(m_acc[kv], axis=0, keepdims=True) for kv in range(N_KV_HEADS)]

    def value_block(kb):
        for kv in range(N_KV_HEADS):
            e = jnp.exp2((lg_buf[kv, kb] - mx[kv]).astype(BF16))
            vt = vtbuf[kb, kv * VT_ROWS:(kv + 1) * VT_ROWS, :]
            o_acc[kv] += jnp.dot(vt, e, preferred_element_type=F32)

    _for_key_blocks(n_kb, value_block)
    for kv in range(N_KV_HEADS):
        out_t = o_acc[kv, 0:HEAD_DIM, :] / o_acc[kv, HEAD_DIM:HEAD_DIM + 1, :]
        for g in range(KV_GROUP):
            hd = kv * KV_GROUP + g
            o_ref[:, hd * HEAD_DIM:(hd + 1) * HEAD_DIM] = out_t[:, g * tq:(g + 1) * tq].T.astype(o_ref.dtype)


def _dsa(qi, kw, q, k, vt, batch, first_tile):
    nq = T_PAD // TQ
    ns = nq - first_tile
    pairs = qi.shape[0]
    cols = KV_GROUP * TQ
    qrow = lambda b, s: b * nq + first_tile + s
    return pl.pallas_call(
        functools.partial(_dsa_kernel, first_tile=first_tile),
        grid=(batch, ns),
        in_specs=[pl.BlockSpec((pairs, TQ, LANES), lambda b, s: (0, qrow(b, s), 0)),
                  pl.BlockSpec((TQ, LANES), lambda b, s: (qrow(b, s), 0)),
                  pl.BlockSpec((TQ, Q_W), lambda b, s: (qrow(b, s), 0)),
                  pl.BlockSpec((T_PAD, LANES), lambda b, s: (b, 0)),
                  pl.BlockSpec((T_PAD, KV_W), lambda b, s: (b, 0)),
                  pl.BlockSpec((KV_W, T_PAD), lambda b, s: (0, b))],
        out_specs=pl.BlockSpec((TQ, Q_W), lambda b, s: (b * ns + s, 0)),
        out_shape=jax.ShapeDtypeStruct((batch * ns * TQ, Q_W), BF16),
        scratch_shapes=[pltpu.VMEM((N_KB, KB, LANES), BF16), pltpu.VMEM((N_KB, KB, LANES), BF16),
                        pltpu.VMEM((N_KB, KB, KV_W), BF16), pltpu.VMEM((N_KB, N_KV_HEADS * VT_ROWS, KB), BF16),
                        pltpu.VMEM((N_KB, KB, TQ), F32),
                        pltpu.VMEM((N_KV_HEADS, N_KB, KB, cols), F32),
                        pltpu.VMEM((N_KV_HEADS, SUBLANES, cols), F32),
                        pltpu.VMEM((N_KV_HEADS, VT_ROWS, cols), F32)],
        compiler_params=_params(("parallel", "arbitrary")),
        name="dsa",
    )(qi, kw, q, kw, k, vt)


def kernel(x, meta, norm_g, ab_w_in, pool_w, pool_scale, s5_lambda_re, s5_lambda_im, s5_log_dt,
           s5_b_re, s5_b_im, s5_c_re, s5_c_im, s5_d, s5_w_glu, ab_w_out, c_w_in, c_w_out, mlp_w1, mlp_w2):
    batch, seq, d = x.shape
    assert (seq + N_META, d) == (T_REAL, D_MODEL)
    depth = norm_g.shape[0]
    meta_b = jnp.broadcast_to(meta[None].astype(x.dtype), (batch, N_META, d))
    pad = jnp.zeros((batch, T_PAD - T_REAL, d), x.dtype)
    h = jnp.concatenate([pad, meta_b, x], axis=1).reshape(batch * T_PAD, d)

    rope128 = _rope_tables(HEAD_DIM, T_PAD, -PAD_FRONT, batch)
    rope64 = _rope_tables(IDX_DIM, T_PAD, -PAD_FRONT, batch)
    w1, w2 = mlp_w1.astype(BF16), mlp_w2.astype(BF16)

    for layer in range(depth):
        g = norm_g[layer]
        frames_only = False
        if layer % 2 == 0:
            e = layer // 2
            u = _norm_proj(h, g[0], ab_w_in[e].astype(BF16))
            bbr, bbi, pwr, pwi = _ssm_prep(s5_lambda_re[e], s5_lambda_im[e], s5_log_dt[e], s5_b_re[e], s5_b_im[e])
            nj = SSM_GROUPS // GROUPS_PER_TILE
            tile_major = lambda p: p.reshape(2, nj, GROUPS_PER_TILE * SSM_STATE).swapaxes(0, 1)
            wb = jnp.concatenate([_block_diag_in(bbr), _block_diag_in(bbi)], axis=-1).astype(BF16)
            wc = jnp.concatenate([_block_diag_out(s5_c_re[e]), -_block_diag_out(s5_c_im[e])], axis=1).astype(BF16)
            mix = _mixer0(u, batch, pool_w[e].astype(BF16), pool_scale[e].reshape(1, POOL_WIDTH), wb, wc,
                          tile_major(pwr), tile_major(pwi),
                          s5_d[e].reshape(1, SSM_WIDTH), s5_w_glu[e].astype(BF16))
            w_out = ab_w_out[e]
        else:
            o = layer // 2
            n_in = c_w_in.shape[-1]
            w_in = jnp.pad(c_w_in[o], ((0, 0), (0, -n_in % LANES))).astype(BF16)
            w_vt = w_in[:, Q_W + KV_W:Q_W + 2 * KV_W].T
            q, k, vt, qi, kw = _dsa_proj(h, g[0], w_in, w_vt, rope128, rope64)
            frames_only = layer == depth - 1
            mix = _dsa(qi, kw, q, k, vt, batch, first_tile=FRAME_TILE if frames_only else 0)
            w_out = c_w_out[o]
        h = _matmul_norm_res(mix, w_out.astype(BF16), g[1], h, (T_PAD, FRAME_TILE * TQ) if frames_only else None)
        h = _mlp(h, g[2], g[3], w1, w2, layer)
    if frames_only:
        return h.reshape(batch, seq, d)
    return h.reshape(batch, T_PAD, d)[:, PAD_FRONT + N_META:]
```

```python
import functools

import jax
import jax.numpy as jnp
from jax import lax
from jax.experimental import pallas as pl
from jax.experimental.pallas import tpu as pltpu

F32 = jnp.float32
BF16 = jnp.bfloat16

D_MODEL = 2048
CHUNK = 64
N_META = 16
POOL_WIDTH = 1024
POOL_WINDOWS = (2, 4, 8, 16)
POOL_GROUP = 256
POOL_HALO = 16
SSM_WIDTH = 1024
SSM_GROUP_CH = 16
SSM_GROUPS = 64
SSM_STATE = 64
GROUPS_PER_TILE = 8
HEAD_DIM = 128
N_HEADS = 16
N_KV_HEADS = 4
KV_GROUP = 4
IDX_HEADS = 16
IDX_DIM = 64
ROPE_THETA = 500000.0
ROPE_FRAC = 4
TOPK = 256
EPS = 1e-6
Q_W = N_HEADS * HEAD_DIM
KV_W = N_KV_HEADS * HEAD_DIM
IQ_W = IDX_HEADS * IDX_DIM

LANES = 128
SUBLANES = 8
VMEM_LIMIT = 56 * 1024 * 1024

T_REAL = N_META + 2048
T_PAD = 2176
PAD_FRONT = T_PAD - T_REAL
TM = 512
TQ = 128
FRAME_TILE = (PAD_FRONT + N_META) // TQ
assert FRAME_TILE * TQ == PAD_FRONT + N_META
L_SSM = 544
SCAN_ROWS = L_SSM // 8
assert SCAN_ROWS * 8 == L_SSM
TILES_PER_SCAN = 2
TF = 1024

LOG2_E = 1.4426950408889634
INT_MIN = -2 ** 31
NEG_INF_KEY = -2139095041


def _params(sem):
    return pltpu.CompilerParams(dimension_semantics=sem, vmem_limit_bytes=VMEM_LIMIT)


def _rms(x, g):
    return x * lax.rsqrt(jnp.mean(x * x, axis=-1, keepdims=True) + EPS) * g


def _const_spec(shape):
    zeros = (0,) * len(shape)
    return pl.BlockSpec(shape, lambda *_: zeros, pipeline_mode=pl.Buffered(1))


def _norm_proj_kernel(h_ref, g_ref, w_ref, o_ref):
    a = _rms(h_ref[...], g_ref[...]).astype(BF16)
    o_ref[...] = jnp.dot(a, w_ref[...], preferred_element_type=F32)


def _norm_proj(h, g, w):
    m, d = h.shape
    n = w.shape[1]
    return pl.pallas_call(
        _norm_proj_kernel,
        grid=(m // TM,),
        in_specs=[pl.BlockSpec((TM, d), lambda i: (i, 0)), _const_spec((1, d)), _const_spec((d, n))],
        out_specs=pl.BlockSpec((TM, n), lambda i: (i, 0)),
        out_shape=jax.ShapeDtypeStruct((m, n), F32),
        compiler_params=_params(("parallel",)),
        name="norm_proj",
    )(h, g.reshape(1, d), w)


def _rope_tables(head_dim, rows_per_batch, first_pos, batch):
    r = head_dim // ROPE_FRAC
    half = r // 2
    inv = ROPE_THETA ** (-jnp.arange(half, dtype=F32) / half)
    ang = (jnp.arange(rows_per_batch) + first_pos).astype(F32)[:, None] * inv[None, :]
    cos, sin = jnp.cos(ang), jnp.sin(ang)
    ones = jnp.ones((rows_per_batch, head_dim - r), F32)
    zeros_h = jnp.zeros((rows_per_batch, half), F32)
    zeros_t = jnp.zeros((rows_per_batch, head_dim - r), F32)
    c = jnp.concatenate([cos, cos, ones], axis=1)
    s_up = jnp.concatenate([zeros_h, sin, zeros_t], axis=1)
    s_dn = jnp.concatenate([-sin, zeros_h, zeros_t], axis=1)
    reps = (batch, LANES // head_dim)
    return half, jnp.tile(c, reps), jnp.tile(s_up, reps), jnp.tile(s_dn, reps)


def _rope(ys, half, c, s_up, s_dn):
    return ys * c + pltpu.roll(ys, half, 1) * s_up + pltpu.roll(ys, LANES - half, 1) * s_dn


def _dsa_proj_kernel(h_ref, g_ref, w_ref, wvt_ref, c128_ref, u128_ref, d128_ref, c64_ref, u64_ref, d64_ref,
                     q_ref, k_ref, vt_ref, qi_ref, kw_ref, *, half128, half64):
    a = _rms(h_ref[...], g_ref[...]).astype(BF16)
    t128 = (half128, c128_ref[...], u128_ref[...], d128_ref[...])
    t64 = (half64, c64_ref[...], u64_ref[...], d64_ref[...])
    c0, c1, c2, c3 = Q_W, Q_W + KV_W, Q_W + 2 * KV_W, Q_W + 2 * KV_W + IQ_W

    def slabs(lo, hi):
        y = jnp.dot(a, w_ref[:, lo:hi], preferred_element_type=F32)
        return [y[:, s * LANES:(s + 1) * LANES] for s in range((hi - lo) // LANES)]

    for s, ys in enumerate(slabs(0, c0)):
        q_ref[:, s * LANES:(s + 1) * LANES] = (_rope(ys, *t128) * (HEAD_DIM ** -0.5 * LOG2_E)).astype(q_ref.dtype)
    for s, ys in enumerate(slabs(c0, c1)):
        k_ref[:, s * LANES:(s + 1) * LANES] = _rope(ys, *t128).astype(k_ref.dtype)
    vt_ref[...] = lax.dot_general(wvt_ref[...], a, (((1,), (1,)), ((), ())),
                                  preferred_element_type=F32).astype(vt_ref.dtype)
    for s, ys in enumerate(slabs(c2, c3)):
        qi_ref[s] = _rope(ys, *t64).astype(qi_ref.dtype)
    (ys,) = slabs(c3, c3 + LANES)
    lane = lax.broadcasted_iota(jnp.int32, ys.shape, 1)
    kw_ref[...] = jnp.where(lane < IDX_DIM, _rope(ys, *t64), ys)


def _dsa_proj(h, g, w, w_vt, rope128, rope64):
    m, d = h.shape
    n = w.shape[1]
    pairs = IQ_W // LANES
    row = lambda width: pl.BlockSpec((TM, width), lambda i: (i, 0))
    return pl.pallas_call(
        functools.partial(_dsa_proj_kernel, half128=rope128[0], half64=rope64[0]),
        grid=(m // TM,),
        in_specs=[row(d), _const_spec((1, d)), _const_spec((d, n)), _const_spec((KV_W, d))] + [row(LANES)] * 6,
        out_specs=[row(Q_W), row(KV_W), pl.BlockSpec((KV_W, TM), lambda i: (0, i)),
                   pl.BlockSpec((pairs, TM, LANES), lambda i: (0, i, 0)), row(LANES)],
        out_shape=[jax.ShapeDtypeStruct((m, Q_W), BF16), jax.ShapeDtypeStruct((m, KV_W), BF16),
                   jax.ShapeDtypeStruct((KV_W, m), BF16), jax.ShapeDtypeStruct((pairs, m, LANES), BF16),
                   jax.ShapeDtypeStruct((m, LANES), F32)],
        compiler_params=_params(("parallel",)),
        name="dsa_proj",
    )(h, g.reshape(1, d), w, w_vt, *rope128[1:], *rope64[1:])


def _matmul_norm_res_kernel(a_ref, w_ref, g_ref, r_ref, o_ref):
    y = jnp.dot(a_ref[...], w_ref[...], preferred_element_type=F32)
    o_ref[...] = r_ref[...] + _rms(y, g_ref[...])


def _matmul_norm_res(a, w, g, res, res_rows=None):
    m, k = a.shape
    n = w.shape[1]
    if res_rows is None:
        tiles = m // TM
        res_spec = pl.BlockSpec((TM, n), lambda b, j: (j, 0))
    else:
        rows, first = res_rows
        tiles = (rows - first) // TM
        assert tiles * TM == rows - first
        assert rows % SUBLANES == first % SUBLANES == 0
        res_spec = pl.BlockSpec((pl.Element(TM), pl.Element(n)),
                                lambda b, j: (pl.multiple_of(b * rows + first + j * TM, SUBLANES), 0))
    row = lambda width: pl.BlockSpec((TM, width), lambda b, j: (b * tiles + j, 0))
    return pl.pallas_call(
        _matmul_norm_res_kernel,
        grid=(m // (tiles * TM), tiles),
        in_specs=[row(k), _const_spec((k, n)), _const_spec((1, n)), res_spec],
        out_specs=row(n),
        out_shape=jax.ShapeDtypeStruct((m, n), F32),
        compiler_params=_params(("parallel", "parallel")),
        name="matmul_norm_res",
    )(a, w, g.reshape(1, n), res)


def _mlp_kernel(h_ref, g_in_ref, g_out_ref, w1_ref, w2_ref, o_ref, a_ref, acc_ref):
    f = pl.program_id(1)
    last = pl.num_programs(1) - 1
    half = a_ref.shape[0] // 2
    halves = [slice(r * half, (r + 1) * half) for r in range(2)]

    def partial_out(a):
        z = jnp.dot(a, w1_ref[...], preferred_element_type=F32)
        z = jnp.square(jnp.maximum(z, 0.0)).astype(BF16)
        return jnp.dot(z, w2_ref[...], preferred_element_type=F32)

    @pl.when(f == 0)
    def _():
        for rows in halves:
            a = _rms(h_ref[rows, :], g_in_ref[...]).astype(BF16)
            a_ref[rows, :] = a
            acc_ref[rows, :] = partial_out(a)

    @pl.when((f != 0) & (f != last))
    def _():
        acc_ref[...] += partial_out(a_ref[...])

    @pl.when(f == last)
    def _():
        for rows in halves:
            y = acc_ref[rows, :] + partial_out(a_ref[rows, :])
            o_ref[rows, :] = h_ref[rows, :] + _rms(y, g_out_ref[...])


def _mlp(h, g_in, g_out, w1, w2, layer):
    m, d = h.shape
    ff = w1.shape[2]
    assert ff // TF >= 2
    return pl.pallas_call(
        _mlp_kernel,
        grid=(m // TM, ff // TF),
        in_specs=[pl.BlockSpec((TM, d), lambda i, f: (i, 0)),
                  _const_spec((1, d)), _const_spec((1, d)),
                  pl.BlockSpec((None, d, TF), lambda i, f: (layer, 0, f)),
                  pl.BlockSpec((None, TF, d), lambda i, f: (layer, f, 0))],
        out_specs=pl.BlockSpec((TM, d), lambda i, f: (i, 0)),
        out_shape=jax.ShapeDtypeStruct((m, d), F32),
        scratch_shapes=[pltpu.VMEM((TM, d), BF16), pltpu.VMEM((TM, d), F32)],
        compiler_params=_params(("parallel", "arbitrary")),
        name="mlp",
    )(h, g_in.reshape(1, d), g_out.reshape(1, d), w1, w2)


def _cmul(ar, ai, br, bi):
    return ar * br - ai * bi, ar * bi + ai * br


def _discretise(ar, ai, log_dt):
    dt = jnp.exp(log_dt)
    mag = jnp.exp(dt * ar)
    return mag * jnp.cos(dt * ai), mag * jnp.sin(dt * ai)


def _ssm_prep_kernel(lre_ref, lim_ref, ldt_ref, lre2_ref, lim2_ref, ldt2_ref, bre_ref, bim_ref,
                     bbr_ref, bbi_ref, pwr_ref, pwi_ref):
    ar, ai = lre_ref[...], lim_ref[...]
    abr, abi = _discretise(ar, ai, ldt_ref[...])
    den = ar * ar + ai * ai
    zr, zi = abr - 1.0, abi
    fr = (zr * ar + zi * ai) / den
    fi = (zi * ar - zr * ai) / den
    br, bi = bre_ref[...], bim_ref[...]
    bbr_ref[...] = fr * br - fi * bi
    bbi_ref[...] = fr * bi + fi * br
    a1r, a1i = _discretise(lre2_ref[...], lim2_ref[...], ldt2_ref[...])
    pr, pi = a1r, a1i
    for _ in range(SCAN_ROWS - 1):
        pr, pi = _cmul(pr, pi, a1r, a1i)
    pwr_ref[0], pwi_ref[0] = a1r, a1i
    pwr_ref[1], pwi_ref[1] = pr, pi


def _ssm_prep(lam_re, lam_im, log_dt, b_re, b_im):
    g, n = lam_re.shape
    c = b_re.shape[-1]
    bt = lambda b: jnp.swapaxes(b, 1, 2)
    full = lambda *s: pl.BlockSpec(s, lambda: (0,) * len(s))
    return pl.pallas_call(
        _ssm_prep_kernel,
        in_specs=[full(g, 1, n), full(g, 1, n), full(g, 1, 1), full(g, n), full(g, n), full(g, 1),
                  full(g, c, n), full(g, c, n)],
        out_specs=[full(g, c, n), full(g, c, n), full(2, g, n), full(2, g, n)],
        out_shape=[jax.ShapeDtypeStruct((g, c, n), F32)] * 2 + [jax.ShapeDtypeStruct((2, g, n), F32)] * 2,
        name="ssm_prep",
    )(lam_re.reshape(g, 1, n), lam_im.reshape(g, 1, n), log_dt.reshape(g, 1, 1), lam_re, lam_im,
      log_dt.reshape(g, 1), bt(b_re), bt(b_im))


def _block_diag_in(bb):
    g, c, n = bb.shape
    r = bb.reshape(g // GROUPS_PER_TILE, GROUPS_PER_TILE, c, n)
    eye = jnp.eye(GROUPS_PER_TILE, dtype=bb.dtype)
    return jnp.einsum('jgcn,gh->jgchn', r, eye).reshape(g // GROUPS_PER_TILE, GROUPS_PER_TILE * c, GROUPS_PER_TILE * n)


def _block_diag_out(cc):
    g, c, n = cc.shape
    r = cc.reshape(g // GROUPS_PER_TILE, GROUPS_PER_TILE, c, n)
    eye = jnp.eye(GROUPS_PER_TILE, dtype=cc.dtype)
    return jnp.einsum('jgcn,gh->jgnhc', r, eye).reshape(g // GROUPS_PER_TILE, GROUPS_PER_TILE * n, GROUPS_PER_TILE * c)


def _row_on_sublanes(ref, j, k):
    return jnp.broadcast_to(ref[j, pl.ds(k, 1), :], (SUBLANES, ref.shape[2]))


def _mixer0_kernel(u_ref, pw_ref, ps_ref, wb_ref, wc_ref, pwr_ref, pwi_ref, d_ref, wg_ref, o_ref,
                   pbuf, uperm, ynat, sbuf, carry_ref):
    c = pl.program_id(1)
    L = u_ref.shape[0]
    sw = GROUPS_PER_TILE * SSM_STATE

    @pl.when(c == 0)
    def _():
        pbuf[0:POOL_HALO, :] = jnp.zeros((POOL_HALO, POOL_WIDTH), F32)
        carry_ref[...] = jnp.zeros_like(carry_ref)

    pbuf[POOL_HALO:POOL_HALO + L, :] = u_ref[:, 0:POOL_WIDTH]
    tpos = jnp.maximum(c * L + lax.broadcasted_iota(jnp.int32, (L, 1), 0) - PAD_FRONT, 0)
    for g, win in enumerate(POOL_WINDOWS):
        sl = slice(g * POOL_GROUP, (g + 1) * POOL_GROUP)
        x = u_ref[:, sl]
        s = x
        for k in range(1, win):
            s = s + pbuf[POOL_HALO - k:POOL_HALO - k + L, sl]
        cnt = jnp.minimum(tpos + 1, win).astype(F32)
        diff = (s / cnt - x).astype(BF16)
        y = jnp.dot(diff, pw_ref[g], preferred_element_type=F32) * ps_ref[:, sl]
        o_ref[:, sl] = y.astype(o_ref.dtype)
    pbuf[0:POOL_HALO, :] = pbuf[L:L + POOL_HALO, :]

    n_tiles = SSM_WIDTH // LANES
    for j in range(n_tiles):
        ynat[j] = u_ref[:, POOL_WIDTH + j * LANES:POOL_WIDTH + (j + 1) * LANES]

    def permute(k, carry):
        for j in range(n_tiles):
            uperm[j, pl.ds(pl.multiple_of(k * SUBLANES, SUBLANES), SUBLANES), :] = \
                ynat[j, pl.ds(k, SUBLANES, stride=SCAN_ROWS), :]
        return carry

    lax.fori_loop(0, SCAN_ROWS, permute, 0)
    row = lax.broadcasted_iota(jnp.int32, (SUBLANES, sw), 0)
    for j0 in range(0, n_tiles, TILES_PER_SCAN):
        tiles = range(j0, j0 + TILES_PER_SCAN)
        for j in tiles:
            sbuf[j - j0] = jnp.dot(uperm[j].astype(BF16), wb_ref[j], preferred_element_type=F32)
        a1 = [(_row_on_sublanes(pwr_ref, j, 0), _row_on_sublanes(pwi_ref, j, 0)) for j in tiles]

        def scan_step(k, xs, store):
            rows = pl.ds(pl.multiple_of(k * SUBLANES, SUBLANES), SUBLANES)
            out = []
            for c, ((a1r, a1i), (xr, xi)) in enumerate(zip(a1, xs)):
                xr, xi = (a1r * xr - a1i * xi + sbuf[c, rows, 0:sw], a1r * xi + a1i * xr + sbuf[c, rows, sw:2 * sw])
                if store:
                    sbuf[c, rows, 0:sw] = xr
                    sbuf[c, rows, sw:2 * sw] = xi
                out.append((xr, xi))
            return tuple(out)

        zero = jnp.zeros((SUBLANES, sw), F32)
        ends = lax.fori_loop(0, SCAN_ROWS, functools.partial(scan_step, store=False),
                             tuple((zero, zero) for _ in tiles))

        starts = []
        for j, (er, ei) in zip(tiles, ends):
            cr = _row_on_sublanes(carry_ref, j, 0)
            ci = _row_on_sublanes(carry_ref, j, 1)
            sr = jnp.where(row == 0, cr, pltpu.roll(er, 1, 0))
            si = jnp.where(row == 0, ci, pltpu.roll(ei, 1, 0))
            qr = _row_on_sublanes(pwr_ref, j, 1)
            qi = _row_on_sublanes(pwi_ref, j, 1)
            for k in (1, 2, 4):
                mr, mi = jnp.where(row >= k, qr, 0.0), jnp.where(row >= k, qi, 0.0)
                tr, ti = pltpu.roll(sr, k, 0), pltpu.roll(si, k, 0)
                sr, si = sr + mr * tr - mi * ti, si + mr * ti + mi * tr
                qr, qi = _cmul(qr, qi, qr, qi)
            starts.append((sr, si))

        lax.fori_loop(0, SCAN_ROWS, functools.partial(scan_step, store=True), tuple(starts))
        for j in tiles:
            c = j - j0
            carry_ref[j, 0:1, :] = sbuf[c, L - 1:L, 0:sw]
            carry_ref[j, 1:2, :] = sbuf[c, L - 1:L, sw:2 * sw]
            yj = jnp.dot(sbuf[c].astype(BF16), wc_ref[j], preferred_element_type=F32)
            uperm[j] = yj + d_ref[:, j * LANES:(j + 1) * LANES] * uperm[j]

    def unpermute(k, carry):
        for j in range(n_tiles):
            ynat[j, pl.ds(k, SUBLANES, stride=SCAN_ROWS), :] = \
                uperm[j, pl.ds(pl.multiple_of(k * SUBLANES, SUBLANES), SUBLANES), :]
        return carry

    lax.fori_loop(0, SCAN_ROWS, unpermute, 0)
    y = jax.nn.gelu(jnp.concatenate([ynat[j] for j in range(n_tiles)], axis=1))
    gate = jax.nn.sigmoid(jnp.dot(y.astype(BF16), wg_ref[...], preferred_element_type=F32))
    o_ref[:, POOL_WIDTH:] = (y * gate).astype(o_ref.dtype)


def _mixer0(u, batch, pool_w, pool_scale, wb, wc, pwr, pwi, d, w_glu):
    m, width = u.shape
    nj = SSM_WIDTH // LANES
    sw = GROUPS_PER_TILE * SSM_STATE
    chunks = T_PAD // L_SSM
    return pl.pallas_call(
        _mixer0_kernel,
        grid=(batch, chunks),
        in_specs=[pl.BlockSpec((L_SSM, width), lambda b, c: (b * chunks + c, 0)),
                  _const_spec((len(POOL_WINDOWS), POOL_GROUP, POOL_GROUP)), _const_spec((1, POOL_WIDTH)),
                  _const_spec((nj, LANES, 2 * sw)), _const_spec((nj, 2 * sw, LANES)),
                  _const_spec((nj, 2, sw)), _const_spec((nj, 2, sw)),
                  _const_spec((1, SSM_WIDTH)), _const_spec((SSM_WIDTH, SSM_WIDTH))],
        out_specs=pl.BlockSpec((L_SSM, width), lambda b, c: (b * chunks + c, 0)),
        out_shape=jax.ShapeDtypeStruct((m, width), BF16),
        scratch_shapes=[pltpu.VMEM((POOL_HALO + L_SSM, POOL_WIDTH), F32),
                        pltpu.VMEM((nj, L_SSM, LANES), F32),
                        pltpu.VMEM((nj, L_SSM, LANES), F32),
                        pltpu.VMEM((TILES_PER_SCAN, L_SSM, 2 * sw), F32),
                        pltpu.VMEM((nj, 2, sw), F32)],
        compiler_params=_params(("parallel", "arbitrary")),
        name="mixer0",
    )(u, pool_w, pool_scale, wb, wc, pwr, pwi, d, w_glu)


CHUNK_SHIFT = CHUNK.bit_length() - 1
assert 1 << CHUNK_SHIFT == CHUNK
KB = 512
KB_SHIFT = KB.bit_length() - 1
assert 1 << KB_SHIFT == KB
N_KB = -(-T_PAD // KB)
COUNT_ROWS = 64
PAIRS_PER_DOT = 2
ONES_ROWS = 16
VT_ROWS = HEAD_DIM + ONES_ROWS


def _chunk_id(t):
    return jnp.where(t < N_META, 0, ((t - N_META) >> CHUNK_SHIFT) + 1)


def _fold_rows(x, op):
    return op(x.reshape(x.shape[0] // SUBLANES, SUBLANES, x.shape[1]), axis=0)


def _stage_rows(src, dst):
    for kb in range(N_KB):
        n = min(KB, T_PAD - kb * KB)
        dst[kb, 0:n, :] = src[kb * KB:kb * KB + n, :].astype(dst.dtype)
        if n < KB:
            dst[kb, n:KB, :] = jnp.zeros((KB - n, dst.shape[2]), dst.dtype)


def _stage_values(src, dst):
    for kb in range(N_KB):
        n = min(KB, T_PAD - kb * KB)
        for kv in range(N_KV_HEADS):
            r0 = kv * VT_ROWS
            dst[kb, r0:r0 + HEAD_DIM, 0:n] = src[kv * HEAD_DIM:(kv + 1) * HEAD_DIM, kb * KB:kb * KB + n]
            if n < KB:
                dst[kb, r0:r0 + HEAD_DIM, n:KB] = jnp.zeros((HEAD_DIM, KB - n), dst.dtype)
            dst[kb, r0 + HEAD_DIM:r0 + VT_ROWS, :] = jnp.ones((ONES_ROWS, KB), dst.dtype)


def _for_key_blocks(n_kb, body):
    def pair(p, carry):
        body(2 * p)
        body(2 * p + 1)
        return carry

    lax.fori_loop(0, n_kb >> 1, pair, 0)

    @pl.when((n_kb & 1) == 1)
    def _():
        body(n_kb - 1)


def _dsa_kernel(qi_ref, wq_ref, q_ref, kw_ref, k_ref, vt_ref, o_ref,
                kie, kio, kbuf, vtbuf, score_buf, lg_buf, m_acc, o_acc, *, first_tile):
    i = pl.program_id(1) + first_tile
    tq = q_ref.shape[0]
    nt = (((1,), (1,)), ((), ()))

    @pl.when(pl.program_id(1) == 0)
    def _():
        kw = kw_ref[...]
        lane = lax.broadcasted_iota(jnp.int32, kw.shape, 1)
        _stage_rows(jnp.where(lane < IDX_DIM, kw, 0.0), kie)
        _stage_rows(jnp.where(lane >= IDX_DIM, pltpu.roll(kw, IDX_DIM, 1), 0.0), kio)
        _stage_rows(k_ref, kbuf)
        _stage_values(vt_ref, vtbuf)

    last_q = (i + 1) * tq - 1 - PAD_FRONT
    key_end = N_META + CHUNK * (((last_q - N_META) >> CHUNK_SHIFT) + 1)
    n_kb = jnp.minimum((key_end + PAD_FRONT + KB - 1) >> KB_SHIFT, N_KB)

    w_t = wq_ref[...].T * ((IDX_HEADS ** -0.5) * (IDX_DIM ** -0.5))
    pairs = IDX_HEADS // 2
    qpos = i * tq + lax.broadcasted_iota(jnp.int32, (1, tq), 1) - PAD_FRONT

    def index_block(kb):
        score = jnp.zeros((KB, tq), F32)
        for p0 in range(0, pairs, PAIRS_PER_DOT):
            qg = qi_ref[p0:p0 + PAIRS_PER_DOT].reshape(PAIRS_PER_DOT * tq, LANES)
            se = lax.dot_general(kie[kb], qg, nt, preferred_element_type=F32)
            so = lax.dot_general(kio[kb], qg, nt, preferred_element_type=F32)
            for c in range(PAIRS_PER_DOT):
                p = p0 + c
                we = w_t[IDX_DIM + 2 * p:IDX_DIM + 2 * p + 1, :]
                wo = w_t[IDX_DIM + 2 * p + 1:IDX_DIM + 2 * p + 2, :]
                score = score + we * jnp.maximum(se[:, c * tq:(c + 1) * tq], 0.0)
                score = score + wo * jnp.maximum(so[:, c * tq:(c + 1) * tq], 0.0)
        kpos = kb * KB + lax.broadcasted_iota(jnp.int32, (KB, 1), 0) - PAD_FRONT
        allowed = (_chunk_id(kpos) <= _chunk_id(qpos)) & (kpos >= 0) & (kpos < T_REAL)
        score_buf[kb] = jnp.where(allowed, score, -jnp.inf)

    _for_key_blocks(n_kb, index_block)

    def key_to_float(key):
        return pltpu.bitcast(key ^ ((key >> 31) & 0x7FFFFFFF), F32)

    def count_where(pred):
        def block(kb, acc):
            ind = jnp.where(pred(kb), 1.0, 0.0)
            return acc + jnp.sum(ind.reshape(KB // COUNT_ROWS, COUNT_ROWS, tq), axis=0)
        acc = lax.fori_loop(0, n_kb, block, jnp.zeros((COUNT_ROWS, tq), F32))
        return jnp.sum(acc, axis=0, keepdims=True)

    def count_ge(cand):
        cand_f = key_to_float(cand)
        return count_where(lambda kb: score_buf[kb] >= cand_f)

    c0 = count_ge(jnp.zeros((1, tq), jnp.int32))
    th = jnp.where(c0 >= TOPK, 0, INT_MIN).astype(jnp.int32)
    n_ge = jnp.where(c0 >= TOPK, c0, float(N_KB * KB))

    def search(j, carry):
        th, n_ge = carry
        cand = th | jnp.left_shift(jnp.int32(1), 30 - j)
        c = count_ge(cand)
        return jnp.where(c >= TOPK, cand, th), jnp.where(c >= TOPK, c, n_ge)

    th, n_ge = lax.fori_loop(0, 31, search, (th, n_ge))
    tied = (n_ge > TOPK) & (th > NEG_INF_KEY) & (qpos >= 0)
    th_key = jnp.maximum(th, NEG_INF_KEY + 1)
    th = key_to_float(th_key)

    @pl.when(jnp.max(tied.astype(jnp.int32)) > 0)
    def _():
        th_up = key_to_float(th_key + 1)
        keep = TOPK - count_where(lambda kb: score_buf[kb] >= th_up)

        def residual(kb, carry):
            score = score_buf[kb]
            at_th = (score >= th) & (score < th_up) & tied
            lg_buf[0, kb, :, 0:tq] = jnp.where(at_th, score - th, -1.0)
            return carry

        lax.fori_loop(0, n_kb, residual, 0)
        resid = lambda kb: lg_buf[0, kb, :, 0:tq]

        def search_residual(j, carry):
            key, n = carry
            cand = key | jnp.left_shift(jnp.int32(1), 30 - j)
            cand_f = pltpu.bitcast(cand, F32)
            c = count_where(lambda kb: resid(kb) >= cand_f)
            return jnp.where(c >= keep, cand, key), jnp.where(c >= keep, c, n)

        n_at = count_where(lambda kb: resid(kb) >= 0.0)
        r_key, _ = lax.fori_loop(0, 31, search_residual, (jnp.zeros((1, tq), jnp.int32), n_at))
        r_th = pltpu.bitcast(r_key, F32)
        keep_eq = keep - count_where(lambda kb: resid(kb) > r_th)
        lower = (lax.broadcasted_iota(jnp.int32, (KB, KB), 0) >= lax.broadcasted_iota(jnp.int32, (KB, KB), 1))
        lower = jnp.where(lower, 1.0, 0.0).astype(BF16)

        def drop(kb, seen):
            r = resid(kb)
            eq = r == r_th
            rank = seen + jnp.dot(lower, jnp.where(eq, 1.0, 0.0).astype(BF16), preferred_element_type=F32)
            gone = ((r >= 0.0) & (r < r_th)) | (eq & (rank > keep_eq))
            score_buf[kb] = jnp.where(gone, -jnp.inf, score_buf[kb])
            return rank[KB - 1:KB, :]

        lax.fori_loop(0, n_kb, drop, jnp.zeros((1, tq), F32))

    qs = []
    for kv in range(N_KV_HEADS):
        h0 = kv * KV_GROUP
        qs.append(jnp.concatenate([q_ref[:, (h0 + g) * HEAD_DIM:(h0 + g + 1) * HEAD_DIM]
                                   for g in range(KV_GROUP)], axis=0))
    m_acc[...] = jnp.full(m_acc.shape, -jnp.inf, F32)
    o_acc[...] = jnp.zeros(o_acc.shape, F32)

    def logits_block(kb):
        sel = score_buf[kb] >= th
        for kv in range(N_KV_HEADS):
            kk = kbuf[kb, :, kv * HEAD_DIM:(kv + 1) * HEAD_DIM]
            lg = lax.dot_general(kk, qs[kv], nt, preferred_element_type=F32)
            lg = jnp.concatenate([jnp.where(sel, lg[:, g * tq:(g + 1) * tq], -jnp.inf)
                                  for g in range(KV_GROUP)], axis=1)
            lg_buf[kv, kb] = lg
            m_acc[kv] = jnp.maximum(m_acc[kv], _fold_rows(lg, jnp.max))

    _for_key_blocks(n_kb, logits_block)
    mx = [jnp.max(m_acc[kv], axis=0, keepdims=True) for kv in range(N_KV_HEADS)]

    def value_block(kb):
        for kv in range(N_KV_HEADS):
            e = jnp.exp2((lg_buf[kv, kb] - mx[kv]).astype(BF16))
            vt = vtbuf[kb, kv * VT_ROWS:(kv + 1) * VT_ROWS, :]
            o_acc[kv] += jnp.dot(vt, e, preferred_element_type=F32)

    _for_key_blocks(n_kb, value_block)
    for kv in range(N_KV_HEADS):
        out_t = o_acc[kv, 0:HEAD_DIM, :] / o_acc[kv, HEAD_DIM:HEAD_DIM + 1, :]
        for g in range(KV_GROUP):
            hd = kv * KV_GROUP + g
            o_ref[:, hd * HEAD_DIM:(hd + 1) * HEAD_DIM] = out_t[:, g * tq:(g + 1) * tq].T.astype(o_ref.dtype)


def _dsa(qi, kw, q, k, vt, batch, first_tile):
    nq = T_PAD // TQ
    ns = nq - first_tile
    pairs = qi.shape[0]
    cols = KV_GROUP * TQ
    qrow = lambda b, s: b * nq + first_tile + s
    return pl.pallas_call(
        functools.partial(_dsa_kernel, first_tile=first_tile),
        grid=(batch, ns),
        in_specs=[pl.BlockSpec((pairs, TQ, LANES), lambda b, s: (0, qrow(b, s), 0)),
                  pl.BlockSpec((TQ, LANES), lambda b, s: (qrow(b, s), 0)),
                  pl.BlockSpec((TQ, Q_W), lambda b, s: (qrow(b, s), 0)),
                  pl.BlockSpec((T_PAD, LANES), lambda b, s: (b, 0)),
                  pl.BlockSpec((T_PAD, KV_W), lambda b, s: (b, 0)),
                  pl.BlockSpec((KV_W, T_PAD), lambda b, s: (0, b))],
        out_specs=pl.BlockSpec((TQ, Q_W), lambda b, s: (b * ns + s, 0)),
        out_shape=jax.ShapeDtypeStruct((batch * ns * TQ, Q_W), BF16),
        scratch_shapes=[pltpu.VMEM((N_KB, KB, LANES), BF16), pltpu.VMEM((N_KB, KB, LANES), BF16),
                        pltpu.VMEM((N_KB, KB, KV_W), BF16), pltpu.VMEM((N_KB, N_KV_HEADS * VT_ROWS, KB), BF16),
                        pltpu.VMEM((N_KB, KB, TQ), F32),
                        pltpu.VMEM((N_KV_HEADS, N_KB, KB, cols), F32),
                        pltpu.VMEM((N_KV_HEADS, SUBLANES, cols), F32),
                        pltpu.VMEM((N_KV_HEADS, VT_ROWS, cols), F32)],
        compiler_params=_params(("parallel", "arbitrary")),
        name="dsa",
    )(qi, kw, q, kw, k, vt)


def kernel(x, meta, norm_g, ab_w_in, pool_w, pool_scale, s5_lambda_re, s5_lambda_im, s5_log_dt,
           s5_b_re, s5_b_im, s5_c_re, s5_c_im, s5_d, s5_w_glu, ab_w_out, c_w_in, c_w_out, mlp_w1, mlp_w2):
    batch, seq, d = x.shape
    assert (seq + N_META, d) == (T_REAL, D_MODEL)
    depth = norm_g.shape[0]
    meta_b = jnp.broadcast_to(meta[None].astype(x.dtype), (batch, N_META, d))
    pad = jnp.zeros((batch, T_PAD - T_REAL, d), x.dtype)
    h = jnp.concatenate([pad, meta_b, x], axis=1).reshape(batch * T_PAD, d)

    rope128 = _rope_tables(HEAD_DIM, T_PAD, -PAD_FRONT, batch)
    rope64 = _rope_tables(IDX_DIM, T_PAD, -PAD_FRONT, batch)
    w1, w2 = mlp_w1.astype(BF16), mlp_w2.astype(BF16)

    for layer in range(depth):
        g = norm_g[layer]
        frames_only = False
        if layer % 2 == 0:
            e = layer // 2
            u = _norm_proj(h, g[0], ab_w_in[e].astype(BF16))
            bbr, bbi, pwr, pwi = _ssm_prep(s5_lambda_re[e], s5_lambda_im[e], s5_log_dt[e], s5_b_re[e], s5_b_im[e])
            nj = SSM_GROUPS // GROUPS_PER_TILE
            tile_major = lambda p: p.reshape(2, nj, GROUPS_PER_TILE * SSM_STATE).swapaxes(0, 1)
            wb = jnp.concatenate([_block_diag_in(bbr), _block_diag_in(bbi)], axis=-1).astype(BF16)
            wc = jnp.concatenate([_block_diag_out(s5_c_re[e]), -_block_diag_out(s5_c_im[e])], axis=1).astype(BF16)
            mix = _mixer0(u, batch, pool_w[e].astype(BF16), pool_scale[e].reshape(1, POOL_WIDTH), wb, wc,
                          tile_major(pwr), tile_major(pwi),
                          s5_d[e].reshape(1, SSM_WIDTH), s5_w_glu[e].astype(BF16))
            w_out = ab_w_out[e]
        else:
            o = layer // 2
            n_in = c_w_in.shape[-1]
            w_in = jnp.pad(c_w_in[o], ((0, 0), (0, -n_in % LANES))).astype(BF16)
            w_vt = w_in[:, Q_W + KV_W:Q_W + 2 * KV_W].T
            q, k, vt, qi, kw = _dsa_proj(h, g[0], w_in, w_vt, rope128, rope64)
            frames_only = layer == depth - 1
            mix = _dsa(qi, kw, q, k, vt, batch, first_tile=FRAME_TILE if frames_only else 0)
            w_out = c_w_out[o]
        h = _matmul_norm_res(mix, w_out.astype(BF16), g[1], h, (T_PAD, FRAME_TILE * TQ) if frames_only else None)
        h = _mlp(h, g[2], g[3], w1, w2, layer)
    if frames_only:
        return h.reshape(batch, seq, d)
    return h.reshape(batch, T_PAD, d)[:, PAD_FRONT + N_META:]
```

```python
import functools

import jax
import jax.numpy as jnp
from jax import lax
from jax.experimental import pallas as pl
from jax.experimental.pallas import tpu as pltpu

F32 = jnp.float32
BF16 = jnp.bfloat16

D_MODEL = 2048
CHUNK = 64
N_META = 16
POOL_WIDTH = 1024
POOL_WINDOWS = (2, 4, 8, 16)
POOL_GROUP = 256
POOL_HALO = 16
SSM_WIDTH = 1024
SSM_GROUP_CH = 16
SSM_GROUPS = 64
SSM_STATE = 64
GROUPS_PER_TILE = 8
HEAD_DIM = 128
N_HEADS = 16
N_KV_HEADS = 4
KV_GROUP = 4
IDX_HEADS = 16
IDX_DIM = 64
ROPE_THETA = 500000.0
ROPE_FRAC = 4
TOPK = 256
EPS = 1e-6
Q_W = N_HEADS * HEAD_DIM
KV_W = N_KV_HEADS * HEAD_DIM
IQ_W = IDX_HEADS * IDX_DIM

LANES = 128
SUBLANES = 8
VMEM_LIMIT = 56 * 1024 * 1024

T_REAL = N_META + 2048
T_PAD = 2176
PAD_FRONT = T_PAD - T_REAL
TM = 512
TQ = 128
FRAME_TILE = (PAD_FRONT + N_META) // TQ
assert FRAME_TILE * TQ == PAD_FRONT + N_META
L_SSM = 544
SCAN_ROWS = L_SSM // 8
assert SCAN_ROWS * 8 == L_SSM
TILES_PER_SCAN = 2
TF = 512
MLP_ROW_TILES = (1088, 1024, TM)

LOG2_E = 1.4426950408889634
INT_MIN = -2 ** 31
NEG_INF_KEY = -2139095041


def _params(sem):
    return pltpu.CompilerParams(dimension_semantics=sem, vmem_limit_bytes=VMEM_LIMIT)


def _rms(x, g):
    return x * lax.rsqrt(jnp.mean(x * x, axis=-1, keepdims=True) + EPS) * g


def _const_spec(shape):
    zeros = (0,) * len(shape)
    return pl.BlockSpec(shape, lambda *_: zeros, pipeline_mode=pl.Buffered(1))


def _norm_proj_kernel(h_ref, g_ref, w_ref, o_ref):
    a = _rms(h_ref[...], g_ref[...]).astype(BF16)
    o_ref[...] = jnp.dot(a, w_ref[...], preferred_element_type=F32)


def _norm_proj(h, g, w):
    m, d = h.shape
    n = w.shape[1]
    return pl.pallas_call(
        _norm_proj_kernel,
        grid=(m // TM,),
        in_specs=[pl.BlockSpec((TM, d), lambda i: (i, 0)), _const_spec((1, d)), _const_spec((d, n))],
        out_specs=pl.BlockSpec((TM, n), lambda i: (i, 0)),
        out_shape=jax.ShapeDtypeStruct((m, n), F32),
        compiler_params=_params(("parallel",)),
        name="norm_proj",
    )(h, g.reshape(1, d), w)


def _rope_tables(head_dim, rows_per_batch, first_pos, batch):
    r = head_dim // ROPE_FRAC
    half = r // 2
    inv = ROPE_THETA ** (-jnp.arange(half, dtype=F32) / half)
    ang = (jnp.arange(rows_per_batch) + first_pos).astype(F32)[:, None] * inv[None, :]
    cos, sin = jnp.cos(ang), jnp.sin(ang)
    ones = jnp.ones((rows_per_batch, head_dim - r), F32)
    zeros_h = jnp.zeros((rows_per_batch, half), F32)
    zeros_t = jnp.zeros((rows_per_batch, head_dim - r), F32)
    c = jnp.concatenate([cos, cos, ones], axis=1)
    s_up = jnp.concatenate([zeros_h, sin, zeros_t], axis=1)
    s_dn = jnp.concatenate([-sin, zeros_h, zeros_t], axis=1)
    reps = (batch, LANES // head_dim)
    return half, jnp.tile(c, reps), jnp.tile(s_up, reps), jnp.tile(s_dn, reps)


def _rope(ys, half, c, s_up, s_dn):
    return ys * c + pltpu.roll(ys, half, 1) * s_up + pltpu.roll(ys, LANES - half, 1) * s_dn


def _dsa_proj_kernel(h_ref, g_ref, w_ref, wvt_ref, c128_ref, u128_ref, d128_ref, c64_ref, u64_ref, d64_ref,
                     q_ref, k_ref, vt_ref, qi_ref, kw_ref, *, half128, half64):
    a = _rms(h_ref[...], g_ref[...]).astype(BF16)
    t128 = (half128, c128_ref[...], u128_ref[...], d128_ref[...])
    t64 = (half64, c64_ref[...], u64_ref[...], d64_ref[...])
    c0, c1, c2, c3 = Q_W, Q_W + KV_W, Q_W + 2 * KV_W, Q_W + 2 * KV_W + IQ_W

    def slabs(lo, hi):
        y = jnp.dot(a, w_ref[:, lo:hi], preferred_element_type=F32)
        return [y[:, s * LANES:(s + 1) * LANES] for s in range((hi - lo) // LANES)]

    for s, ys in enumerate(slabs(0, c0)):
        q_ref[:, s * LANES:(s + 1) * LANES] = (_rope(ys, *t128) * (HEAD_DIM ** -0.5 * LOG2_E)).astype(q_ref.dtype)
    for s, ys in enumerate(slabs(c0, c1)):
        k_ref[:, s * LANES:(s + 1) * LANES] = _rope(ys, *t128).astype(k_ref.dtype)
    vt_ref[...] = lax.dot_general(wvt_ref[...], a, (((1,), (1,)), ((), ())),
                                  preferred_element_type=F32).astype(vt_ref.dtype)
    for s, ys in enumerate(slabs(c2, c3)):
        qi_ref[s] = _rope(ys, *t64).astype(qi_ref.dtype)
    (ys,) = slabs(c3, c3 + LANES)
    lane = lax.broadcasted_iota(jnp.int32, ys.shape, 1)
    kw_ref[...] = jnp.where(lane < IDX_DIM, _rope(ys, *t64), ys)


def _dsa_proj(h, g, w, w_vt, rope128, rope64):
    m, d = h.shape
    n = w.shape[1]
    pairs = IQ_W // LANES
    row = lambda width: pl.BlockSpec((TM, width), lambda i: (i, 0))
    return pl.pallas_call(
        functools.partial(_dsa_proj_kernel, half128=rope128[0], half64=rope64[0]),
        grid=(m // TM,),
        in_specs=[row(d), _const_spec((1, d)), _const_spec((d, n)), _const_spec((KV_W, d))] + [row(LANES)] * 6,
        out_specs=[row(Q_W), row(KV_W), pl.BlockSpec((KV_W, TM), lambda i: (0, i)),
                   pl.BlockSpec((pairs, TM, LANES), lambda i: (0, i, 0)), row(LANES)],
        out_shape=[jax.ShapeDtypeStruct((m, Q_W), BF16), jax.ShapeDtypeStruct((m, KV_W), BF16),
                   jax.ShapeDtypeStruct((KV_W, m), BF16), jax.ShapeDtypeStruct((pairs, m, LANES), BF16),
                   jax.ShapeDtypeStruct((m, LANES), F32)],
        compiler_params=_params(("parallel",)),
        name="dsa_proj",
    )(h, g.reshape(1, d), w, w_vt, *rope128[1:], *rope64[1:])


def _matmul_norm_res_kernel(a_ref, w_ref, g_ref, r_ref, o_ref):
    y = jnp.dot(a_ref[...], w_ref[...], preferred_element_type=F32)
    o_ref[...] = r_ref[...] + _rms(y, g_ref[...])


def _matmul_norm_res(a, w, g, res, res_rows=None):
    m, k = a.shape
    n = w.shape[1]
    if res_rows is None:
        tiles = m // TM
        res_spec = pl.BlockSpec((TM, n), lambda b, j: (j, 0))
    else:
        rows, first = res_rows
        tiles = (rows - first) // TM
        assert tiles * TM == rows - first
        assert rows % SUBLANES == first % SUBLANES == 0
        res_spec = pl.BlockSpec((pl.Element(TM), pl.Element(n)),
                                lambda b, j: (pl.multiple_of(b * rows + first + j * TM, SUBLANES), 0))
    row = lambda width: pl.BlockSpec((TM, width), lambda b, j: (b * tiles + j, 0))
    return pl.pallas_call(
        _matmul_norm_res_kernel,
        grid=(m // (tiles * TM), tiles),
        in_specs=[row(k), _const_spec((k, n)), _const_spec((1, n)), res_spec],
        out_specs=row(n),
        out_shape=jax.ShapeDtypeStruct((m, n), F32),
        compiler_params=_params(("parallel", "parallel")),
        name="matmul_norm_res",
    )(a, w, g.reshape(1, n), res)


def _mlp_kernel(h_ref, g_in_ref, g_out_ref, w1_ref, w2_ref, o_ref, a_ref, acc_ref):
    f = pl.program_id(1)
    last = pl.num_programs(1) - 1
    half = a_ref.shape[0] // 2
    halves = [slice(r * half, (r + 1) * half) for r in range(2)]

    def partial_out(a):
        z = jnp.dot(a, w1_ref[...], preferred_element_type=F32)
        z = jnp.square(jnp.maximum(z, 0.0)).astype(BF16)
        return jnp.dot(z, w2_ref[...], preferred_element_type=F32)

    @pl.when(f == 0)
    def _():
        for rows in halves:
            a = _rms(h_ref[rows, :], g_in_ref[...]).astype(BF16)
            a_ref[rows, :] = a
            acc_ref[rows, :] = partial_out(a)

    @pl.when((f != 0) & (f != last))
    def _():
        acc_ref[...] += partial_out(a_ref[...])

    @pl.when(f == last)
    def _():
        for rows in halves:
            y = acc_ref[rows, :] + partial_out(a_ref[rows, :])
            o_ref[rows, :] = h_ref[rows, :] + _rms(y, g_out_ref[...])


def _mlp(h, g_in, g_out, w1, w2, layer):
    m, d = h.shape
    ff = w1.shape[2]
    assert ff // TF >= 2
    tm = next(t for t in MLP_ROW_TILES if m % t == 0)
    once = pl.Buffered(1)
    return pl.pallas_call(
        _mlp_kernel,
        grid=(m // tm, ff // TF),
        in_specs=[pl.BlockSpec((tm, d), lambda i, f: (i, 0), pipeline_mode=once),
                  _const_spec((1, d)), _const_spec((1, d)),
                  pl.BlockSpec((None, d, TF), lambda i, f: (layer, 0, f)),
                  pl.BlockSpec((None, TF, d), lambda i, f: (layer, f, 0))],
        out_specs=pl.BlockSpec((tm, d), lambda i, f: (i, 0), pipeline_mode=once),
        out_shape=jax.ShapeDtypeStruct((m, d), F32),
        scratch_shapes=[pltpu.VMEM((tm, d), BF16), pltpu.VMEM((tm, d), F32)],
        compiler_params=_params(("parallel", "arbitrary")),
        name="mlp",
    )(h, g_in.reshape(1, d), g_out.reshape(1, d), w1, w2)


def _cmul(ar, ai, br, bi):
    return ar * br - ai * bi, ar * bi + ai * br


def _discretise(ar, ai, log_dt):
    dt = jnp.exp(log_dt)
    mag = jnp.exp(dt * ar)
    return mag * jnp.cos(dt * ai), mag * jnp.sin(dt * ai)


def _ssm_prep_kernel(lre_ref, lim_ref, ldt_ref, lre2_ref, lim2_ref, ldt2_ref, bre_ref, bim_ref,
                     bbr_ref, bbi_ref, pwr_ref, pwi_ref):
    ar, ai = lre_ref[...], lim_ref[...]
    abr, abi = _discretise(ar, ai, ldt_ref[...])
    den = ar * ar + ai * ai
    zr, zi = abr - 1.0, abi
    fr = (zr * ar + zi * ai) / den
    fi = (zi * ar - zr * ai) / den
    br, bi = bre_ref[...], bim_ref[...]
    bbr_ref[...] = fr * br - fi * bi
    bbi_ref[...] = fr * bi + fi * br
    a1r, a1i = _discretise(lre2_ref[...], lim2_ref[...], ldt2_ref[...])
    pr, pi = a1r, a1i
    for _ in range(SCAN_ROWS - 1):
        pr, pi = _cmul(pr, pi, a1r, a1i)
    pwr_ref[0], pwi_ref[0] = a1r, a1i
    pwr_ref[1], pwi_ref[1] = pr, pi


def _ssm_prep(lam_re, lam_im, log_dt, b_re, b_im):
    g, n = lam_re.shape
    c = b_re.shape[-1]
    bt = lambda b: jnp.swapaxes(b, 1, 2)
    full = lambda *s: pl.BlockSpec(s, lambda: (0,) * len(s))
    return pl.pallas_call(
        _ssm_prep_kernel,
        in_specs=[full(g, 1, n), full(g, 1, n), full(g, 1, 1), full(g, n), full(g, n), full(g, 1),
                  full(g, c, n), full(g, c, n)],
        out_specs=[full(g, c, n), full(g, c, n), full(2, g, n), full(2, g, n)],
        out_shape=[jax.ShapeDtypeStruct((g, c, n), F32)] * 2 + [jax.ShapeDtypeStruct((2, g, n), F32)] * 2,
        name="ssm_prep",
    )(lam_re.reshape(g, 1, n), lam_im.reshape(g, 1, n), log_dt.reshape(g, 1, 1), lam_re, lam_im,
      log_dt.reshape(g, 1), bt(b_re), bt(b_im))


def _block_diag_in(bb):
    g, c, n = bb.shape
    r = bb.reshape(g // GROUPS_PER_TILE, GROUPS_PER_TILE, c, n)
    eye = jnp.eye(GROUPS_PER_TILE, dtype=bb.dtype)
    return jnp.einsum('jgcn,gh->jgchn', r, eye).reshape(g // GROUPS_PER_TILE, GROUPS_PER_TILE * c, GROUPS_PER_TILE * n)


def _block_diag_out(cc):
    g, c, n = cc.shape
    r = cc.reshape(g // GROUPS_PER_TILE, GROUPS_PER_TILE, c, n)
    eye = jnp.eye(GROUPS_PER_TILE, dtype=cc.dtype)
    return jnp.einsum('jgcn,gh->jgnhc', r, eye).reshape(g // GROUPS_PER_TILE, GROUPS_PER_TILE * n, GROUPS_PER_TILE * c)


def _row_on_sublanes(ref, j, k):
    return jnp.broadcast_to(ref[j, pl.ds(k, 1), :], (SUBLANES, ref.shape[2]))


def _mixer0_kernel(u_ref, pw_ref, ps_ref, wb_ref, wc_ref, pwr_ref, pwi_ref, d_ref, wg_ref, o_ref,
                   pbuf, uperm, ynat, sbuf, carry_ref):
    c = pl.program_id(1)
    L = u_ref.shape[0]
    sw = GROUPS_PER_TILE * SSM_STATE

    @pl.when(c == 0)
    def _():
        pbuf[0:POOL_HALO, :] = jnp.zeros((POOL_HALO, POOL_WIDTH), F32)
        carry_ref[...] = jnp.zeros_like(carry_ref)

    pbuf[POOL_HALO:POOL_HALO + L, :] = u_ref[:, 0:POOL_WIDTH]
    tpos = jnp.maximum(c * L + lax.broadcasted_iota(jnp.int32, (L, 1), 0) - PAD_FRONT, 0)
    for g, win in enumerate(POOL_WINDOWS):
        sl = slice(g * POOL_GROUP, (g + 1) * POOL_GROUP)
        x = u_ref[:, sl]
        s = x
        for k in range(1, win):
            s = s + pbuf[POOL_HALO - k:POOL_HALO - k + L, sl]
        cnt = jnp.minimum(tpos + 1, win).astype(F32)
        diff = (s / cnt - x).astype(BF16)
        y = jnp.dot(diff, pw_ref[g], preferred_element_type=F32) * ps_ref[:, sl]
        o_ref[:, sl] = y.astype(o_ref.dtype)
    pbuf[0:POOL_HALO, :] = pbuf[L:L + POOL_HALO, :]

    n_tiles = SSM_WIDTH // LANES
    for j in range(n_tiles):
        ynat[j] = u_ref[:, POOL_WIDTH + j * LANES:POOL_WIDTH + (j + 1) * LANES]

    def permute(k, carry):
        for j in range(n_tiles):
            uperm[j, pl.ds(pl.multiple_of(k * SUBLANES, SUBLANES), SUBLANES), :] = \
                ynat[j, pl.ds(k, SUBLANES, stride=SCAN_ROWS), :]
        return carry

    lax.fori_loop(0, SCAN_ROWS, permute, 0)
    row = lax.broadcasted_iota(jnp.int32, (SUBLANES, sw), 0)
    for j0 in range(0, n_tiles, TILES_PER_SCAN):
        tiles = range(j0, j0 + TILES_PER_SCAN)
        for j in tiles:
            sbuf[j - j0] = jnp.dot(uperm[j].astype(BF16), wb_ref[j], preferred_element_type=F32)
        a1 = [(_row_on_sublanes(pwr_ref, j, 0), _row_on_sublanes(pwi_ref, j, 0)) for j in tiles]

        def scan_step(k, xs, store):
            rows = pl.ds(pl.multiple_of(k * SUBLANES, SUBLANES), SUBLANES)
            out = []
            for c, ((a1r, a1i), (xr, xi)) in enumerate(zip(a1, xs)):
                xr, xi = (a1r * xr - a1i * xi + sbuf[c, rows, 0:sw], a1r * xi + a1i * xr + sbuf[c, rows, sw:2 * sw])
                if store:
                    sbuf[c, rows, 0:sw] = xr
                    sbuf[c, rows, sw:2 * sw] = xi
                out.append((xr, xi))
            return tuple(out)

        zero = jnp.zeros((SUBLANES, sw), F32)
        ends = lax.fori_loop(0, SCAN_ROWS, functools.partial(scan_step, store=False),
                             tuple((zero, zero) for _ in tiles))

        starts = []
        for j, (er, ei) in zip(tiles, ends):
            cr = _row_on_sublanes(carry_ref, j, 0)
            ci = _row_on_sublanes(carry_ref, j, 1)
            sr = jnp.where(row == 0, cr, pltpu.roll(er, 1, 0))
            si = jnp.where(row == 0, ci, pltpu.roll(ei, 1, 0))
            qr = _row_on_sublanes(pwr_ref, j, 1)
            qi = _row_on_sublanes(pwi_ref, j, 1)
            for k in (1, 2, 4):
                mr, mi = jnp.where(row >= k, qr, 0.0), jnp.where(row >= k, qi, 0.0)
                tr, ti = pltpu.roll(sr, k, 0), pltpu.roll(si, k, 0)
                sr, si = sr + mr * tr - mi * ti, si + mr * ti + mi * tr
                qr, qi = _cmul(qr, qi, qr, qi)
            starts.append((sr, si))

        lax.fori_loop(0, SCAN_ROWS, functools.partial(scan_step, store=True), tuple(starts))
        for j in tiles:
            c = j - j0
            carry_ref[j, 0:1, :] = sbuf[c, L - 1:L, 0:sw]
            carry_ref[j, 1:2, :] = sbuf[c, L - 1:L, sw:2 * sw]
            yj = jnp.dot(sbuf[c].astype(BF16), wc_ref[j], preferred_element_type=F32)
            uperm[j] = yj + d_ref[:, j * LANES:(j + 1) * LANES] * uperm[j]

    def unpermute(k, carry):
        for j in range(n_tiles):
            ynat[j, pl.ds(k, SUBLANES, stride=SCAN_ROWS), :] = \
                uperm[j, pl.ds(pl.multiple_of(k * SUBLANES, SUBLANES), SUBLANES), :]
        return carry

    lax.fori_loop(0, SCAN_ROWS, unpermute, 0)
    y = jax.nn.gelu(jnp.concatenate([ynat[j] for j in range(n_tiles)], axis=1))
    gate = jax.nn.sigmoid(jnp.dot(y.astype(BF16), wg_ref[...], preferred_element_type=F32))
    o_ref[:, POOL_WIDTH:] = (y * gate).astype(o_ref.dtype)


def _mixer0(u, batch, pool_w, pool_scale, wb, wc, pwr, pwi, d, w_glu):
    m, width = u.shape
    nj = SSM_WIDTH // LANES
    sw = GROUPS_PER_TILE * SSM_STATE
    chunks = T_PAD // L_SSM
    return pl.pallas_call(
        _mixer0_kernel,
        grid=(batch, chunks),
        in_specs=[pl.BlockSpec((L_SSM, width), lambda b, c: (b * chunks + c, 0)),
                  _const_spec((len(POOL_WINDOWS), POOL_GROUP, POOL_GROUP)), _const_spec((1, POOL_WIDTH)),
                  _const_spec((nj, LANES, 2 * sw)), _const_spec((nj, 2 * sw, LANES)),
                  _const_spec((nj, 2, sw)), _const_spec((nj, 2, sw)),
                  _const_spec((1, SSM_WIDTH)), _const_spec((SSM_WIDTH, SSM_WIDTH))],
        out_specs=pl.BlockSpec((L_SSM, width), lambda b, c: (b * chunks + c, 0)),
        out_shape=jax.ShapeDtypeStruct((m, width), BF16),
        scratch_shapes=[pltpu.VMEM((POOL_HALO + L_SSM, POOL_WIDTH), F32),
                        pltpu.VMEM((nj, L_SSM, LANES), F32),
                        pltpu.VMEM((nj, L_SSM, LANES), F32),
                        pltpu.VMEM((TILES_PER_SCAN, L_SSM, 2 * sw), F32),
                        pltpu.VMEM((nj, 2, sw), F32)],
        compiler_params=_params(("parallel", "arbitrary")),
        name="mixer0",
    )(u, pool_w, pool_scale, wb, wc, pwr, pwi, d, w_glu)


CHUNK_SHIFT = CHUNK.bit_length() - 1
assert 1 << CHUNK_SHIFT == CHUNK
KB = 512
KB_SHIFT = KB.bit_length() - 1
assert 1 << KB_SHIFT == KB
N_KB = -(-T_PAD // KB)
COUNT_ROWS = 64
PAIRS_PER_DOT = 2
ONES_ROWS = 16
VT_ROWS = HEAD_DIM + ONES_ROWS


def _chunk_id(t):
    return jnp.where(t < N_META, 0, ((t - N_META) >> CHUNK_SHIFT) + 1)


def _fold_rows(x, op):
    return op(x.reshape(x.shape[0] // SUBLANES, SUBLANES, x.shape[1]), axis=0)


def _stage_rows(src, dst):
    for kb in range(N_KB):
        n = min(KB, T_PAD - kb * KB)
        dst[kb, 0:n, :] = src[kb * KB:kb * KB + n, :].astype(dst.dtype)
        if n < KB:
            dst[kb, n:KB, :] = jnp.zeros((KB - n, dst.shape[2]), dst.dtype)


def _stage_values(src, dst):
    for kb in range(N_KB):
        n = min(KB, T_PAD - kb * KB)
        for kv in range(N_KV_HEADS):
            r0 = kv * VT_ROWS
            dst[kb, r0:r0 + HEAD_DIM, 0:n] = src[kv * HEAD_DIM:(kv + 1) * HEAD_DIM, kb * KB:kb * KB + n]
            if n < KB:
                dst[kb, r0:r0 + HEAD_DIM, n:KB] = jnp.zeros((HEAD_DIM, KB - n), dst.dtype)
            dst[kb, r0 + HEAD_DIM:r0 + VT_ROWS, :] = jnp.ones((ONES_ROWS, KB), dst.dtype)


def _for_key_blocks(n_kb, body):
    def pair(p, carry):
        body(2 * p)
        body(2 * p + 1)
        return carry

    lax.fori_loop(0, n_kb >> 1, pair, 0)

    @pl.when((n_kb & 1) == 1)
    def _():
        body(n_kb - 1)


def _dsa_kernel(qi_ref, wq_ref, q_ref, kw_ref, k_ref, vt_ref, o_ref,
                kie, kio, kbuf, vtbuf, score_buf, lg_buf, m_acc, o_acc, *, first_tile):
    i = pl.program_id(1) + first_tile
    tq = q_ref.shape[0]
    nt = (((1,), (1,)), ((), ()))

    @pl.when(pl.program_id(1) == 0)
    def _():
        kw = kw_ref[...]
        lane = lax.broadcasted_iota(jnp.int32, kw.shape, 1)
        _stage_rows(jnp.where(lane < IDX_DIM, kw, 0.0), kie)
        _stage_rows(jnp.where(lane >= IDX_DIM, pltpu.roll(kw, IDX_DIM, 1), 0.0), kio)
        _stage_rows(k_ref, kbuf)
        _stage_values(vt_ref, vtbuf)

    last_q = (i + 1) * tq - 1 - PAD_FRONT
    key_end = N_META + CHUNK * (((last_q - N_META) >> CHUNK_SHIFT) + 1)
    n_kb = jnp.minimum((key_end + PAD_FRONT + KB - 1) >> KB_SHIFT, N_KB)

    w_t = wq_ref[...].T * ((IDX_HEADS ** -0.5) * (IDX_DIM ** -0.5))
    pairs = IDX_HEADS // 2
    qpos = i * tq + lax.broadcasted_iota(jnp.int32, (1, tq), 1) - PAD_FRONT

    def index_block(kb):
        score = jnp.zeros((KB, tq), F32)
        for p0 in range(0, pairs, PAIRS_PER_DOT):
            qg = qi_ref[p0:p0 + PAIRS_PER_DOT].reshape(PAIRS_PER_DOT * tq, LANES)
            se = lax.dot_general(kie[kb], qg, nt, preferred_element_type=F32)
            so = lax.dot_general(kio[kb], qg, nt, preferred_element_type=F32)
            for c in range(PAIRS_PER_DOT):
                p = p0 + c
                we = w_t[IDX_DIM + 2 * p:IDX_DIM + 2 * p + 1, :]
                wo = w_t[IDX_DIM + 2 * p + 1:IDX_DIM + 2 * p + 2, :]
                score = score + we * jnp.maximum(se[:, c * tq:(c + 1) * tq], 0.0)
                score = score + wo * jnp.maximum(so[:, c * tq:(c + 1) * tq], 0.0)
        kpos = kb * KB + lax.broadcasted_iota(jnp.int32, (KB, 1), 0) - PAD_FRONT
        allowed = (_chunk_id(kpos) <= _chunk_id(qpos)) & (kpos >= 0) & (kpos < T_REAL)
        score_buf[kb] = jnp.where(allowed, score, -jnp.inf)

    _for_key_blocks(n_kb, index_block)

    def key_to_float(key):
        return pltpu.bitcast(key ^ ((key >> 31) & 0x7FFFFFFF), F32)

    def count_where(pred):
        def block(kb, acc):
            ind = jnp.where(pred(kb), 1.0, 0.0)
            return acc + jnp.sum(ind.reshape(KB // COUNT_ROWS, COUNT_ROWS, tq), axis=0)
        acc = lax.fori_loop(0, n_kb, block, jnp.zeros((COUNT_ROWS, tq), F32))
        return jnp.sum(acc, axis=0, keepdims=True)

    def count_ge(cand):
        cand_f = key_to_float(cand)
        return count_where(lambda kb: score_buf[kb] >= cand_f)

    c0 = count_ge(jnp.zeros((1, tq), jnp.int32))
    th = jnp.where(c0 >= TOPK, 0, INT_MIN).astype(jnp.int32)
    n_ge = jnp.where(c0 >= TOPK, c0, float(N_KB * KB))

    def search(j, carry):
        th, n_ge = carry
        cand = th | jnp.left_shift(jnp.int32(1), 30 - j)
        c = count_ge(cand)
        return jnp.where(c >= TOPK, cand, th), jnp.where(c >= TOPK, c, n_ge)

    th, n_ge = lax.fori_loop(0, 31, search, (th, n_ge))
    tied = (n_ge > TOPK) & (th > NEG_INF_KEY) & (qpos >= 0)
    th_key = jnp.maximum(th, NEG_INF_KEY + 1)
    th = key_to_float(th_key)

    @pl.when(jnp.max(tied.astype(jnp.int32)) > 0)
    def _():
        th_up = key_to_float(th_key + 1)
        keep = TOPK - count_where(lambda kb: score_buf[kb] >= th_up)

        def residual(kb, carry):
            score = score_buf[kb]
            at_th = (score >= th) & (score < th_up) & tied
            lg_buf[0, kb, :, 0:tq] = jnp.where(at_th, score - th, -1.0)
            return carry

        lax.fori_loop(0, n_kb, residual, 0)
        resid = lambda kb: lg_buf[0, kb, :, 0:tq]

        def search_residual(j, carry):
            key, n = carry
            cand = key | jnp.left_shift(jnp.int32(1), 30 - j)
            cand_f = pltpu.bitcast(cand, F32)
            c = count_where(lambda kb: resid(kb) >= cand_f)
            return jnp.where(c >= keep, cand, key), jnp.where(c >= keep, c, n)

        n_at = count_where(lambda kb: resid(kb) >= 0.0)
        r_key, _ = lax.fori_loop(0, 31, search_residual, (jnp.zeros((1, tq), jnp.int32), n_at))
        r_th = pltpu.bitcast(r_key, F32)
        keep_eq = keep - count_where(lambda kb: resid(kb) > r_th)
        lower = (lax.broadcasted_iota(jnp.int32, (KB, KB), 0) >= lax.broadcasted_iota(jnp.int32, (KB, KB), 1))
        lower = jnp.where(lower, 1.0, 0.0).astype(BF16)

        def drop(kb, seen):
            r = resid(kb)
            eq = r == r_th
            rank = seen + jnp.dot(lower, jnp.where(eq, 1.0, 0.0).astype(BF16), preferred_element_type=F32)
            gone = ((r >= 0.0) & (r < r_th)) | (eq & (rank > keep_eq))
            score_buf[kb] = jnp.where(gone, -jnp.inf, score_buf[kb])
            return rank[KB - 1:KB, :]

        lax.fori_loop(0, n_kb, drop, jnp.zeros((1, tq), F32))

    qs = []
    for kv in range(N_KV_HEADS):
        h0 = kv * KV_GROUP
        qs.append(jnp.concatenate([q_ref[:, (h0 + g) * HEAD_DIM:(h0 + g + 1) * HEAD_DIM]
                                   for g in range(KV_GROUP)], axis=0))
    m_acc[...] = jnp.full(m_acc.shape, -jnp.inf, F32)
    o_acc[...] = jnp.zeros(o_acc.shape, F32)

    def logits_block(kb):
        sel = score_buf[kb] >= th
        for kv in range(N_KV_HEADS):
            kk = kbuf[kb, :, kv * HEAD_DIM:(kv + 1) * HEAD_DIM]
            lg = lax.dot_general(kk, qs[kv], nt, preferred_element_type=F32)
            lg = jnp.concatenate([jnp.where(sel, lg[:, g * tq:(g + 1) * tq], -jnp.inf)
                                  for g in range(KV_GROUP)], axis=1)
            lg_buf[kv, kb] = lg
            m_acc[kv] = jnp.maximum(m_acc[kv], _fold_rows(lg, jnp.max))

    _for_key_blocks(n_kb, logits_block)
    mx = [jnp.max(m_acc[kv], axis=0, keepdims=True) for kv in range(N_KV_HEADS)]

    def value_block(kb):
        for kv in range(N_KV_HEADS):
            e = jnp.exp2((lg_buf[kv, kb] - mx[kv]).astype(BF16))
            vt = vtbuf[kb, kv * VT_ROWS:(kv + 1) * VT_ROWS, :]
            o_acc[kv] += jnp.dot(vt, e, preferred_element_type=F32)

    _for_key_blocks(n_kb, value_block)
    for kv in range(N_KV_HEADS):
        out_t = o_acc[kv, 0:HEAD_DIM, :] / o_acc[kv, HEAD_DIM:HEAD_DIM + 1, :]
        for g in range(KV_GROUP):
            hd = kv * KV_GROUP + g
            o_ref[:, hd * HEAD_DIM:(hd + 1) * HEAD_DIM] = out_t[:, g * tq:(g + 1) * tq].T.astype(o_ref.dtype)


def _dsa(qi, kw, q, k, vt, batch, first_tile):
    nq = T_PAD // TQ
    ns = nq - first_tile
    pairs = qi.shape[0]
    cols = KV_GROUP * TQ
    qrow = lambda b, s: b * nq + first_tile + s
    return pl.pallas_call(
        functools.partial(_dsa_kernel, first_tile=first_tile),
        grid=(batch, ns),
        in_specs=[pl.BlockSpec((pairs, TQ, LANES), lambda b, s: (0, qrow(b, s), 0)),
                  pl.BlockSpec((TQ, LANES), lambda b, s: (qrow(b, s), 0)),
                  pl.BlockSpec((TQ, Q_W), lambda b, s: (qrow(b, s), 0)),
                  pl.BlockSpec((T_PAD, LANES), lambda b, s: (b, 0)),
                  pl.BlockSpec((T_PAD, KV_W), lambda b, s: (b, 0)),
                  pl.BlockSpec((KV_W, T_PAD), lambda b, s: (0, b))],
        out_specs=pl.BlockSpec((TQ, Q_W), lambda b, s: (b * ns + s, 0)),
        out_shape=jax.ShapeDtypeStruct((batch * ns * TQ, Q_W), BF16),
        scratch_shapes=[pltpu.VMEM((N_KB, KB, LANES), BF16), pltpu.VMEM((N_KB, KB, LANES), BF16),
                        pltpu.VMEM((N_KB, KB, KV_W), BF16), pltpu.VMEM((N_KB, N_KV_HEADS * VT_ROWS, KB), BF16),
                        pltpu.VMEM((N_KB, KB, TQ), F32),
                        pltpu.VMEM((N_KV_HEADS, N_KB, KB, cols), F32),
                        pltpu.VMEM((N_KV_HEADS, SUBLANES, cols), F32),
                        pltpu.VMEM((N_KV_HEADS, VT_ROWS, cols), F32)],
        compiler_params=_params(("parallel", "arbitrary")),
        name="dsa",
    )(qi, kw, q, kw, k, vt)


def kernel(x, meta, norm_g, ab_w_in, pool_w, pool_scale, s5_lambda_re, s5_lambda_im, s5_log_dt,
           s5_b_re, s5_b_im, s5_c_re, s5_c_im, s5_d, s5_w_glu, ab_w_out, c_w_in, c_w_out, mlp_w1, mlp_w2):
    batch, seq, d = x.shape
    assert (seq + N_META, d) == (T_REAL, D_MODEL)
    depth = norm_g.shape[0]
    meta_b = jnp.broadcast_to(meta[None].astype(x.dtype), (batch, N_META, d))
    pad = jnp.zeros((batch, T_PAD - T_REAL, d), x.dtype)
    h = jnp.concatenate([pad, meta_b, x], axis=1).reshape(batch * T_PAD, d)

    rope128 = _rope_tables(HEAD_DIM, T_PAD, -PAD_FRONT, batch)
    rope64 = _rope_tables(IDX_DIM, T_PAD, -PAD_FRONT, batch)
    w1, w2 = mlp_w1.astype(BF16), mlp_w2.astype(BF16)

    for layer in range(depth):
        g = norm_g[layer]
        frames_only = False
        if layer % 2 == 0:
            e = layer // 2
            u = _norm_proj(h, g[0], ab_w_in[e].astype(BF16))
            bbr, bbi, pwr, pwi = _ssm_prep(s5_lambda_re[e], s5_lambda_im[e], s5_log_dt[e], s5_b_re[e], s5_b_im[e])
            nj = SSM_GROUPS // GROUPS_PER_TILE
            tile_major = lambda p: p.reshape(2, nj, GROUPS_PER_TILE * SSM_STATE).swapaxes(0, 1)
            wb = jnp.concatenate([_block_diag_in(bbr), _block_diag_in(bbi)], axis=-1).astype(BF16)
            wc = jnp.concatenate([_block_diag_out(s5_c_re[e]), -_block_diag_out(s5_c_im[e])], axis=1).astype(BF16)
            mix = _mixer0(u, batch, pool_w[e].astype(BF16), pool_scale[e].reshape(1, POOL_WIDTH), wb, wc,
                          tile_major(pwr), tile_major(pwi),
                          s5_d[e].reshape(1, SSM_WIDTH), s5_w_glu[e].astype(BF16))
            w_out = ab_w_out[e]
        else:
            o = layer // 2
            n_in = c_w_in.shape[-1]
            w_in = jnp.pad(c_w_in[o], ((0, 0), (0, -n_in % LANES))).astype(BF16)
            w_vt = w_in[:, Q_W + KV_W:Q_W + 2 * KV_W].T
            q, k, vt, qi, kw = _dsa_proj(h, g[0], w_in, w_vt, rope128, rope64)
            frames_only = layer == depth - 1
            mix = _dsa(qi, kw, q, k, vt, batch, first_tile=FRAME_TILE if frames_only else 0)
            w_out = c_w_out[o]
        h = _matmul_norm_res(mix, w_out.astype(BF16), g[1], h, (T_PAD, FRAME_TILE * TQ) if frames_only else None)
        h = _mlp(h, g[2], g[3], w1, w2, layer)
    if frames_only:
        return h.reshape(batch, seq, d)
    return h.reshape(batch, T_PAD, d)[:, PAD_FRONT + N_META:]
```

```python
import functools

import jax
import jax.numpy as jnp
from jax import lax
from jax.experimental import pallas as pl
from jax.experimental.pallas import tpu as pltpu

F32 = jnp.float32
BF16 = jnp.bfloat16

D_MODEL = 2048
CHUNK = 64
N_META = 16
POOL_WIDTH = 1024
POOL_WINDOWS = (2, 4, 8, 16)
POOL_GROUP = 256
POOL_HALO = 16
SSM_WIDTH = 1024
SSM_GROUP_CH = 16
SSM_GROUPS = 64
SSM_STATE = 64
GROUPS_PER_TILE = 8
HEAD_DIM = 128
N_HEADS = 16
N_KV_HEADS = 4
KV_GROUP = 4
IDX_HEADS = 16
IDX_DIM = 64
ROPE_THETA = 500000.0
ROPE_FRAC = 4
TOPK = 256
EPS = 1e-6
Q_W = N_HEADS * HEAD_DIM
KV_W = N_KV_HEADS * HEAD_DIM
IQ_W = IDX_HEADS * IDX_DIM

LANES = 128
SUBLANES = 8
VMEM_LIMIT = 56 * 1024 * 1024

T_REAL = N_META + 2048
T_PAD = 2176
PAD_FRONT = T_PAD - T_REAL
TM = 512
TQ = 128
FRAME_TILE = (PAD_FRONT + N_META) // TQ
assert FRAME_TILE * TQ == PAD_FRONT + N_META
L_SSM = 544
SCAN_ROWS = L_SSM // 8
assert SCAN_ROWS * 8 == L_SSM
TILES_PER_SCAN = 2
TF = 1024

LOG2_E = 1.4426950408889634
INT_MIN = -2 ** 31
NEG_INF_KEY = -2139095041


def _params(sem):
    return pltpu.CompilerParams(dimension_semantics=sem, vmem_limit_bytes=VMEM_LIMIT)


def _rms(x, g):
    return x * lax.rsqrt(jnp.mean(x * x, axis=-1, keepdims=True) + EPS) * g


def _const_spec(shape):
    zeros = (0,) * len(shape)
    return pl.BlockSpec(shape, lambda *_: zeros, pipeline_mode=pl.Buffered(1))


def _norm_proj_kernel(h_ref, g_ref, w_ref, o_ref):
    a = _rms(h_ref[...], g_ref[...]).astype(BF16)
    o_ref[...] = jnp.dot(a, w_ref[...], preferred_element_type=F32)


def _norm_proj(h, g, w):
    m, d = h.shape
    n = w.shape[1]
    return pl.pallas_call(
        _norm_proj_kernel,
        grid=(m // TM,),
        in_specs=[pl.BlockSpec((TM, d), lambda i: (i, 0)), _const_spec((1, d)), _const_spec((d, n))],
        out_specs=pl.BlockSpec((TM, n), lambda i: (i, 0)),
        out_shape=jax.ShapeDtypeStruct((m, n), F32),
        compiler_params=_params(("parallel",)),
        name="norm_proj",
    )(h, g.reshape(1, d), w)


def _rope_tables(head_dim, rows_per_batch, first_pos, batch):
    r = head_dim // ROPE_FRAC
    half = r // 2
    inv = ROPE_THETA ** (-jnp.arange(half, dtype=F32) / half)
    ang = (jnp.arange(rows_per_batch) + first_pos).astype(F32)[:, None] * inv[None, :]
    cos, sin = jnp.cos(ang), jnp.sin(ang)
    ones = jnp.ones((rows_per_batch, head_dim - r), F32)
    zeros_h = jnp.zeros((rows_per_batch, half), F32)
    zeros_t = jnp.zeros((rows_per_batch, head_dim - r), F32)
    c = jnp.concatenate([cos, cos, ones], axis=1)
    s_up = jnp.concatenate([zeros_h, sin, zeros_t], axis=1)
    s_dn = jnp.concatenate([-sin, zeros_h, zeros_t], axis=1)
    reps = (batch, LANES // head_dim)
    return half, jnp.tile(c, reps), jnp.tile(s_up, reps), jnp.tile(s_dn, reps)


def _rope(ys, half, c, s_up, s_dn):
    return ys * c + pltpu.roll(ys, half, 1) * s_up + pltpu.roll(ys, LANES - half, 1) * s_dn


def _dsa_proj_kernel(h_ref, g_ref, w_ref, wvt_ref, c128_ref, u128_ref, d128_ref, c64_ref, u64_ref, d64_ref,
                     q_ref, k_ref, vt_ref, qi_ref, kw_ref, *, half128, half64):
    a = _rms(h_ref[...], g_ref[...]).astype(BF16)
    t128 = (half128, c128_ref[...], u128_ref[...], d128_ref[...])
    t64 = (half64, c64_ref[...], u64_ref[...], d64_ref[...])
    c0, c1, c2, c3 = Q_W, Q_W + KV_W, Q_W + 2 * KV_W, Q_W + 2 * KV_W + IQ_W

    def slabs(lo, hi):
        y = jnp.dot(a, w_ref[:, lo:hi], preferred_element_type=F32)
        return [y[:, s * LANES:(s + 1) * LANES] for s in range((hi - lo) // LANES)]

    for s, ys in enumerate(slabs(0, c0)):
        q_ref[:, s * LANES:(s + 1) * LANES] = (_rope(ys, *t128) * (HEAD_DIM ** -0.5 * LOG2_E)).astype(q_ref.dtype)
    for s, ys in enumerate(slabs(c0, c1)):
        k_ref[:, s * LANES:(s + 1) * LANES] = _rope(ys, *t128).astype(k_ref.dtype)
    vt_ref[...] = lax.dot_general(wvt_ref[...], a, (((1,), (1,)), ((), ())),
                                  preferred_element_type=F32).astype(vt_ref.dtype)
    for s, ys in enumerate(slabs(c2, c3)):
        qi_ref[s] = _rope(ys, *t64).astype(qi_ref.dtype)
    (ys,) = slabs(c3, c3 + LANES)
    lane = lax.broadcasted_iota(jnp.int32, ys.shape, 1)
    kw_ref[...] = jnp.where(lane < IDX_DIM, _rope(ys, *t64), ys)


def _dsa_proj(h, g, w, w_vt, rope128, rope64):
    m, d = h.shape
    n = w.shape[1]
    pairs = IQ_W // LANES
    row = lambda width: pl.BlockSpec((TM, width), lambda i: (i, 0))
    return pl.pallas_call(
        functools.partial(_dsa_proj_kernel, half128=rope128[0], half64=rope64[0]),
        grid=(m // TM,),
        in_specs=[row(d), _const_spec((1, d)), _const_spec((d, n)), _const_spec((KV_W, d))] + [row(LANES)] * 6,
        out_specs=[row(Q_W), row(KV_W), pl.BlockSpec((KV_W, TM), lambda i: (0, i)),
                   pl.BlockSpec((pairs, TM, LANES), lambda i: (0, i, 0)), row(LANES)],
        out_shape=[jax.ShapeDtypeStruct((m, Q_W), BF16), jax.ShapeDtypeStruct((m, KV_W), BF16),
                   jax.ShapeDtypeStruct((KV_W, m), BF16), jax.ShapeDtypeStruct((pairs, m, LANES), BF16),
                   jax.ShapeDtypeStruct((m, LANES), F32)],
        compiler_params=_params(("parallel",)),
        name="dsa_proj",
    )(h, g.reshape(1, d), w, w_vt, *rope128[1:], *rope64[1:])


def _matmul_norm_res_kernel(a_ref, w_ref, g_ref, r_ref, o_ref):
    y = jnp.dot(a_ref[...], w_ref[...], preferred_element_type=F32)
    o_ref[...] = r_ref[...] + _rms(y, g_ref[...])


def _matmul_norm_res(a, w, g, res, res_rows=None):
    m, k = a.shape
    n = w.shape[1]
    if res_rows is None:
        tiles = m // TM
        res_spec = pl.BlockSpec((TM, n), lambda b, j: (j, 0))
    else:
        rows, first = res_rows
        tiles = (rows - first) // TM
        assert tiles * TM == rows - first
        assert rows % SUBLANES == first % SUBLANES == 0
        res_spec = pl.BlockSpec((pl.Element(TM), pl.Element(n)),
                                lambda b, j: (pl.multiple_of(b * rows + first + j * TM, SUBLANES), 0))
    row = lambda width: pl.BlockSpec((TM, width), lambda b, j: (b * tiles + j, 0))
    return pl.pallas_call(
        _matmul_norm_res_kernel,
        grid=(m // (tiles * TM), tiles),
        in_specs=[row(k), _const_spec((k, n)), _const_spec((1, n)), res_spec],
        out_specs=row(n),
        out_shape=jax.ShapeDtypeStruct((m, n), F32),
        compiler_params=_params(("parallel", "parallel")),
        name="matmul_norm_res",
    )(a, w, g.reshape(1, n), res)


def _mlp_kernel(h_ref, g_in_ref, g_out_ref, w1_ref, w2_ref, o_ref, a_ref, acc_ref):
    f = pl.program_id(1)
    last = pl.num_programs(1) - 1
    half = a_ref.shape[0] // 2
    halves = [slice(r * half, (r + 1) * half) for r in range(2)]

    def partial_out(a):
        z = jnp.dot(a, w1_ref[...], preferred_element_type=F32)
        z = jnp.square(jnp.maximum(z, 0.0)).astype(BF16)
        return jnp.dot(z, w2_ref[...], preferred_element_type=F32)

    @pl.when(f == 0)
    def _():
        for rows in halves:
            a = _rms(h_ref[rows, :], g_in_ref[...]).astype(BF16)
            a_ref[rows, :] = a
            acc_ref[rows, :] = partial_out(a)

    @pl.when((f != 0) & (f != last))
    def _():
        acc_ref[...] += partial_out(a_ref[...])

    @pl.when(f == last)
    def _():
        for rows in halves:
            y = acc_ref[rows, :] + partial_out(a_ref[rows, :])
            o_ref[rows, :] = h_ref[rows, :] + _rms(y, g_out_ref[...])


def _mlp(h, g_in, g_out, w1, w2, layer):
    m, d = h.shape
    ff = w1.shape[2]
    assert ff // TF >= 2
    return pl.pallas_call(
        _mlp_kernel,
        grid=(m // TM, ff // TF),
        in_specs=[pl.BlockSpec((TM, d), lambda i, f: (i, 0)),
                  _const_spec((1, d)), _const_spec((1, d)),
                  pl.BlockSpec((None, d, TF), lambda i, f: (layer, 0, f)),
                  pl.BlockSpec((None, TF, d), lambda i, f: (layer, f, 0))],
        out_specs=pl.BlockSpec((TM, d), lambda i, f: (i, 0)),
        out_shape=jax.ShapeDtypeStruct((m, d), F32),
        scratch_shapes=[pltpu.VMEM((TM, d), BF16), pltpu.VMEM((TM, d), F32)],
        compiler_params=_params(("parallel", "arbitrary")),
        name="mlp",
    )(h, g_in.reshape(1, d), g_out.reshape(1, d), w1, w2)


def _cmul(ar, ai, br, bi):
    return ar * br - ai * bi, ar * bi + ai * br


def _discretise(ar, ai, log_dt):
    dt = jnp.exp(log_dt)
    mag = jnp.exp(dt * ar)
    return mag * jnp.cos(dt * ai), mag * jnp.sin(dt * ai)


def _ssm_prep_kernel(lre_ref, lim_ref, ldt_ref, lre2_ref, lim2_ref, ldt2_ref, bre_ref, bim_ref,
                     bbr_ref, bbi_ref, pwr_ref, pwi_ref):
    ar, ai = lre_ref[...], lim_ref[...]
    abr, abi = _discretise(ar, ai, ldt_ref[...])
    den = ar * ar + ai * ai
    zr, zi = abr - 1.0, abi
    fr = (zr * ar + zi * ai) / den
    fi = (zi * ar - zr * ai) / den
    br, bi = bre_ref[...], bim_ref[...]
    bbr_ref[...] = fr * br - fi * bi
    bbi_ref[...] = fr * bi + fi * br
    a1r, a1i = _discretise(lre2_ref[...], lim2_ref[...], ldt2_ref[...])
    pr, pi = a1r, a1i
    for _ in range(SCAN_ROWS - 1):
        pr, pi = _cmul(pr, pi, a1r, a1i)
    pwr_ref[0], pwi_ref[0] = a1r, a1i
    pwr_ref[1], pwi_ref[1] = pr, pi


def _ssm_prep(lam_re, lam_im, log_dt, b_re, b_im):
    g, n = lam_re.shape
    c = b_re.shape[-1]
    bt = lambda b: jnp.swapaxes(b, 1, 2)
    full = lambda *s: pl.BlockSpec(s, lambda: (0,) * len(s))
    return pl.pallas_call(
        _ssm_prep_kernel,
        in_specs=[full(g, 1, n), full(g, 1, n), full(g, 1, 1), full(g, n), full(g, n), full(g, 1),
                  full(g, c, n), full(g, c, n)],
        out_specs=[full(g, c, n), full(g, c, n), full(2, g, n), full(2, g, n)],
        out_shape=[jax.ShapeDtypeStruct((g, c, n), F32)] * 2 + [jax.ShapeDtypeStruct((2, g, n), F32)] * 2,
        name="ssm_prep",
    )(lam_re.reshape(g, 1, n), lam_im.reshape(g, 1, n), log_dt.reshape(g, 1, 1), lam_re, lam_im,
      log_dt.reshape(g, 1), bt(b_re), bt(b_im))


def _block_diag_in(bb):
    g, c, n = bb.shape
    r = bb.reshape(g // GROUPS_PER_TILE, GROUPS_PER_TILE, c, n)
    eye = jnp.eye(GROUPS_PER_TILE, dtype=bb.dtype)
    return jnp.einsum('jgcn,gh->jgchn', r, eye).reshape(g // GROUPS_PER_TILE, GROUPS_PER_TILE * c, GROUPS_PER_TILE * n)


def _block_diag_out(cc):
    g, c, n = cc.shape
    r = cc.reshape(g // GROUPS_PER_TILE, GROUPS_PER_TILE, c, n)
    eye = jnp.eye(GROUPS_PER_TILE, dtype=cc.dtype)
    return jnp.einsum('jgcn,gh->jgnhc', r, eye).reshape(g // GROUPS_PER_TILE, GROUPS_PER_TILE * n, GROUPS_PER_TILE * c)


def _row_on_sublanes(ref, j, k):
    return jnp.broadcast_to(ref[j, pl.ds(k, 1), :], (SUBLANES, ref.shape[2]))


def _mixer0_kernel(u_ref, pw_ref, ps_ref, wb_ref, wc_ref, pwr_ref, pwi_ref, d_ref, wg_ref, o_ref,
                   pbuf, uperm, ynat, sbuf, carry_ref):
    c = pl.program_id(1)
    L = u_ref.shape[0]
    sw = GROUPS_PER_TILE * SSM_STATE

    @pl.when(c == 0)
    def _():
        pbuf[0:POOL_HALO, :] = jnp.zeros((POOL_HALO, POOL_WIDTH), F32)
        carry_ref[...] = jnp.zeros_like(carry_ref)

    pbuf[POOL_HALO:POOL_HALO + L, :] = u_ref[:, 0:POOL_WIDTH]
    tpos = jnp.maximum(c * L + lax.broadcasted_iota(jnp.int32, (L, 1), 0) - PAD_FRONT, 0)
    for g, win in enumerate(POOL_WINDOWS):
        sl = slice(g * POOL_GROUP, (g + 1) * POOL_GROUP)
        x = u_ref[:, sl]
        s = x
        for k in range(1, win):
            s = s + pbuf[POOL_HALO - k:POOL_HALO - k + L, sl]
        cnt = jnp.minimum(tpos + 1, win).astype(F32)
        diff = (s / cnt - x).astype(BF16)
        y = jnp.dot(diff, pw_ref[g], preferred_element_type=F32) * ps_ref[:, sl]
        o_ref[:, sl] = y.astype(o_ref.dtype)
    pbuf[0:POOL_HALO, :] = pbuf[L:L + POOL_HALO, :]

    n_tiles = SSM_WIDTH // LANES
    for j in range(n_tiles):
        ynat[j] = u_ref[:, POOL_WIDTH + j * LANES:POOL_WIDTH + (j + 1) * LANES]

    def permute(k, carry):
        for j in range(n_tiles):
            uperm[j, pl.ds(pl.multiple_of(k * SUBLANES, SUBLANES), SUBLANES), :] = \
                ynat[j, pl.ds(k, SUBLANES, stride=SCAN_ROWS), :]
        return carry

    lax.fori_loop(0, SCAN_ROWS, permute, 0)
    row = lax.broadcasted_iota(jnp.int32, (SUBLANES, sw), 0)
    for j0 in range(0, n_tiles, TILES_PER_SCAN):
        tiles = range(j0, j0 + TILES_PER_SCAN)
        for j in tiles:
            sbuf[j - j0] = jnp.dot(uperm[j].astype(BF16), wb_ref[j], preferred_element_type=F32)
        a1 = [(_row_on_sublanes(pwr_ref, j, 0), _row_on_sublanes(pwi_ref, j, 0)) for j in tiles]

        def scan_step(k, xs, store):
            rows = pl.ds(pl.multiple_of(k * SUBLANES, SUBLANES), SUBLANES)
            out = []
            for c, ((a1r, a1i), (xr, xi)) in enumerate(zip(a1, xs)):
                xr, xi = (a1r * xr - a1i * xi + sbuf[c, rows, 0:sw], a1r * xi + a1i * xr + sbuf[c, rows, sw:2 * sw])
                if store:
                    sbuf[c, rows, 0:sw] = xr
                    sbuf[c, rows, sw:2 * sw] = xi
                out.append((xr, xi))
            return tuple(out)

        zero = jnp.zeros((SUBLANES, sw), F32)
        ends = lax.fori_loop(0, SCAN_ROWS, functools.partial(scan_step, store=False),
                             tuple((zero, zero) for _ in tiles))

        starts = []
        for j, (er, ei) in zip(tiles, ends):
            cr = _row_on_sublanes(carry_ref, j, 0)
            ci = _row_on_sublanes(carry_ref, j, 1)
            sr = jnp.where(row == 0, cr, pltpu.roll(er, 1, 0))
            si = jnp.where(row == 0, ci, pltpu.roll(ei, 1, 0))
            qr = _row_on_sublanes(pwr_ref, j, 1)
            qi = _row_on_sublanes(pwi_ref, j, 1)
            for k in (1, 2, 4):
                mr, mi = jnp.where(row >= k, qr, 0.0), jnp.where(row >= k, qi, 0.0)
                tr, ti = pltpu.roll(sr, k, 0), pltpu.roll(si, k, 0)
                sr, si = sr + mr * tr - mi * ti, si + mr * ti + mi * tr
                qr, qi = _cmul(qr, qi, qr, qi)
            starts.append((sr, si))

        lax.fori_loop(0, SCAN_ROWS, functools.partial(scan_step, store=True), tuple(starts))
        for j in tiles:
            c = j - j0
            carry_ref[j, 0:1, :] = sbuf[c, L - 1:L, 0:sw]
            carry_ref[j, 1:2, :] = sbuf[c, L - 1:L, sw:2 * sw]
            yj = jnp.dot(sbuf[c].astype(BF16), wc_ref[j], preferred_element_type=F32)
            uperm[j] = yj + d_ref[:, j * LANES:(j + 1) * LANES] * uperm[j]

    def unpermute(k, carry):
        for j in range(n_tiles):
            ynat[j, pl.ds(k, SUBLANES, stride=SCAN_ROWS), :] = \
                uperm[j, pl.ds(pl.multiple_of(k * SUBLANES, SUBLANES), SUBLANES), :]
        return carry

    lax.fori_loop(0, SCAN_ROWS, unpermute, 0)
    y = jax.nn.gelu(jnp.concatenate([ynat[j] for j in range(n_tiles)], axis=1))
    gate = jax.nn.sigmoid(jnp.dot(y.astype(BF16), wg_ref[...], preferred_element_type=F32))
    o_ref[:, POOL_WIDTH:] = (y * gate).astype(o_ref.dtype)


def _mixer0(u, batch, pool_w, pool_scale, wb, wc, pwr, pwi, d, w_glu):
    m, width = u.shape
    nj = SSM_WIDTH // LANES
    sw = GROUPS_PER_TILE * SSM_STATE
    chunks = T_PAD // L_SSM
    return pl.pallas_call(
        _mixer0_kernel,
        grid=(batch, chunks),
        in_specs=[pl.BlockSpec((L_SSM, width), lambda b, c: (b * chunks + c, 0)),
                  _const_spec((len(POOL_WINDOWS), POOL_GROUP, POOL_GROUP)), _const_spec((1, POOL_WIDTH)),
                  _const_spec((nj, LANES, 2 * sw)), _const_spec((nj, 2 * sw, LANES)),
                  _const_spec((nj, 2, sw)), _const_spec((nj, 2, sw)),
                  _const_spec((1, SSM_WIDTH)), _const_spec((SSM_WIDTH, SSM_WIDTH))],
        out_specs=pl.BlockSpec((L_SSM, width), lambda b, c: (b * chunks + c, 0)),
        out_shape=jax.ShapeDtypeStruct((m, width), BF16),
        scratch_shapes=[pltpu.VMEM((POOL_HALO + L_SSM, POOL_WIDTH), F32),
                        pltpu.VMEM((nj, L_SSM, LANES), F32),
                        pltpu.VMEM((nj, L_SSM, LANES), F32),
                        pltpu.VMEM((TILES_PER_SCAN, L_SSM, 2 * sw), F32),
                        pltpu.VMEM((nj, 2, sw), F32)],
        compiler_params=_params(("parallel", "arbitrary")),
        name="mixer0",
    )(u, pool_w, pool_scale, wb, wc, pwr, pwi, d, w_glu)


CHUNK_SHIFT = CHUNK.bit_length() - 1
assert 1 << CHUNK_SHIFT == CHUNK
KB = 512
KB_SHIFT = KB.bit_length() - 1
assert 1 << KB_SHIFT == KB
N_KB = -(-T_PAD // KB)
COUNT_ROWS = 64
PAIRS_PER_DOT = 2
ONES_ROWS = 16
VT_ROWS = HEAD_DIM + ONES_ROWS


def _chunk_id(t):
    return jnp.where(t < N_META, 0, ((t - N_META) >> CHUNK_SHIFT) + 1)


def _fold_rows(x, op):
    return op(x.reshape(x.shape[0] // SUBLANES, SUBLANES, x.shape[1]), axis=0)


def _stage_rows(src, dst):
    for kb in range(N_KB):
        n = min(KB, T_PAD - kb * KB)
        dst[kb, 0:n, :] = src[kb * KB:kb * KB + n, :].astype(dst.dtype)
        if n < KB:
            dst[kb, n:KB, :] = jnp.zeros((KB - n, dst.shape[2]), dst.dtype)


def _stage_values(src, dst):
    for kb in range(N_KB):
        n = min(KB, T_PAD - kb * KB)
        for kv in range(N_KV_HEADS):
            r0 = kv * VT_ROWS
            dst[kb, r0:r0 + HEAD_DIM, 0:n] = src[kv * HEAD_DIM:(kv + 1) * HEAD_DIM, kb * KB:kb * KB + n]
            if n < KB:
                dst[kb, r0:r0 + HEAD_DIM, n:KB] = jnp.zeros((HEAD_DIM, KB - n), dst.dtype)
            dst[kb, r0 + HEAD_DIM:r0 + VT_ROWS, :] = jnp.ones((ONES_ROWS, KB), dst.dtype)


def _for_key_blocks(n_kb, body):
    def pair(p, carry):
        body(2 * p)
        body(2 * p + 1)
        return carry

    lax.fori_loop(0, n_kb >> 1, pair, 0)

    @pl.when((n_kb & 1) == 1)
    def _():
        body(n_kb - 1)


def _dsa_kernel(qi_ref, wq_ref, q_ref, kw_ref, k_ref, vt_ref, o_ref,
                kie, kio, kbuf, vtbuf, score_buf, lg_buf, m_acc, o_acc, *, first_tile):
    i = pl.program_id(1) + first_tile
    tq = q_ref.shape[0]
    nt = (((1,), (1,)), ((), ()))

    @pl.when(pl.program_id(1) == 0)
    def _():
        kw = kw_ref[...]
        lane = lax.broadcasted_iota(jnp.int32, kw.shape, 1)
        _stage_rows(jnp.where(lane < IDX_DIM, kw, 0.0), kie)
        _stage_rows(jnp.where(lane >= IDX_DIM, pltpu.roll(kw, IDX_DIM, 1), 0.0), kio)
        _stage_rows(k_ref, kbuf)
        _stage_values(vt_ref, vtbuf)

    last_q = (i + 1) * tq - 1 - PAD_FRONT
    key_end = N_META + CHUNK * (((last_q - N_META) >> CHUNK_SHIFT) + 1)
    n_kb = jnp.minimum((key_end + PAD_FRONT + KB - 1) >> KB_SHIFT, N_KB)

    w_t = wq_ref[...].T * ((IDX_HEADS ** -0.5) * (IDX_DIM ** -0.5))
    pairs = IDX_HEADS // 2
    qpos = i * tq + lax.broadcasted_iota(jnp.int32, (1, tq), 1) - PAD_FRONT

    def index_block(kb):
        score = jnp.zeros((KB, tq), F32)
        for p0 in range(0, pairs, PAIRS_PER_DOT):
            qg = qi_ref[p0:p0 + PAIRS_PER_DOT].reshape(PAIRS_PER_DOT * tq, LANES)
            se = lax.dot_general(kie[kb], qg, nt, preferred_element_type=F32)
            so = lax.dot_general(kio[kb], qg, nt, preferred_element_type=F32)
            for c in range(PAIRS_PER_DOT):
                p = p0 + c
                we = w_t[IDX_DIM + 2 * p:IDX_DIM + 2 * p + 1, :]
                wo = w_t[IDX_DIM + 2 * p + 1:IDX_DIM + 2 * p + 2, :]
                score = score + we * jnp.maximum(se[:, c * tq:(c + 1) * tq], 0.0)
                score = score + wo * jnp.maximum(so[:, c * tq:(c + 1) * tq], 0.0)
        kpos = kb * KB + lax.broadcasted_iota(jnp.int32, (KB, 1), 0) - PAD_FRONT
        allowed = (_chunk_id(kpos) <= _chunk_id(qpos)) & (kpos >= 0) & (kpos < T_REAL)
        score_buf[kb] = jnp.where(allowed, score, -jnp.inf)

    _for_key_blocks(n_kb, index_block)

    def key_to_float(key):
        return pltpu.bitcast(key ^ ((key >> 31) & 0x7FFFFFFF), F32)

    def count_where(pred):
        def block(kb, acc):
            ind = jnp.where(pred(kb), 1.0, 0.0)
            return acc + jnp.sum(ind.reshape(KB // COUNT_ROWS, COUNT_ROWS, tq), axis=0)
        acc = lax.fori_loop(0, n_kb, block, jnp.zeros((COUNT_ROWS, tq), F32))
        return jnp.sum(acc, axis=0, keepdims=True)

    def count_ge(cand):
        cand_f = key_to_float(cand)
        return count_where(lambda kb: score_buf[kb] >= cand_f)

    c0 = count_ge(jnp.zeros((1, tq), jnp.int32))
    th = jnp.where(c0 >= TOPK, 0, INT_MIN).astype(jnp.int32)
    n_ge = jnp.where(c0 >= TOPK, c0, float(N_KB * KB))

    def search(j, carry):
        th, n_ge = carry
        cand = th | jnp.left_shift(jnp.int32(1), 30 - j)
        c = count_ge(cand)
        return jnp.where(c >= TOPK, cand, th), jnp.where(c >= TOPK, c, n_ge)

    th, n_ge = lax.fori_loop(0, 31, search, (th, n_ge))
    tied = (n_ge > TOPK) & (th > NEG_INF_KEY) & (qpos >= 0)
    th_key = jnp.maximum(th, NEG_INF_KEY + 1)
    th = key_to_float(th_key)

    @pl.when(jnp.max(tied.astype(jnp.int32)) > 0)
    def _():
        th_up = key_to_float(th_key + 1)
        keep = TOPK - count_where(lambda kb: score_buf[kb] >= th_up)

        def residual(kb, carry):
            score = score_buf[kb]
            at_th = (score >= th) & (score < th_up) & tied
            lg_buf[0, kb, :, 0:tq] = jnp.where(at_th, score - th, -1.0)
            return carry

        lax.fori_loop(0, n_kb, residual, 0)
        resid = lambda kb: lg_buf[0, kb, :, 0:tq]

        def search_residual(j, carry):
            key, n = carry
            cand = key | jnp.left_shift(jnp.int32(1), 30 - j)
            cand_f = pltpu.bitcast(cand, F32)
            c = count_where(lambda kb: resid(kb) >= cand_f)
            return jnp.where(c >= keep, cand, key), jnp.where(c >= keep, c, n)

        n_at = count_where(lambda kb: resid(kb) >= 0.0)
        r_key, _ = lax.fori_loop(0, 31, search_residual, (jnp.zeros((1, tq), jnp.int32), n_at))
        r_th = pltpu.bitcast(r_key, F32)
        keep_eq = keep - count_where(lambda kb: resid(kb) > r_th)
        lower = (lax.broadcasted_iota(jnp.int32, (KB, KB), 0) >= lax.broadcasted_iota(jnp.int32, (KB, KB), 1))
        lower = jnp.where(lower, 1.0, 0.0).astype(BF16)

        def drop(kb, seen):
            r = resid(kb)
            eq = r == r_th
            rank = seen + jnp.dot(lower, jnp.where(eq, 1.0, 0.0).astype(BF16), preferred_element_type=F32)
            gone = ((r >= 0.0) & (r < r_th)) | (eq & (rank > keep_eq))
            score_buf[kb] = jnp.where(gone, -jnp.inf, score_buf[kb])
            return rank[KB - 1:KB, :]

        lax.fori_loop(0, n_kb, drop, jnp.zeros((1, tq), F32))

    qs = []
    for kv in range(N_KV_HEADS):
        h0 = kv * KV_GROUP
        qs.append(jnp.concatenate([q_ref[:, (h0 + g) * HEAD_DIM:(h0 + g + 1) * HEAD_DIM]
                                   for g in range(KV_GROUP)], axis=0))
    m_acc[...] = jnp.full(m_acc.shape, -jnp.inf, F32)
    o_acc[...] = jnp.zeros(o_acc.shape, F32)

    def logits_block(kb):
        sel = score_buf[kb] >= th
        for kv in range(N_KV_HEADS):
            kk = kbuf[kb, :, kv * HEAD_DIM:(kv + 1) * HEAD_DIM]
            lg = lax.dot_general(kk, qs[kv], nt, preferred_element_type=F32)
            lg = jnp.concatenate([jnp.where(sel, lg[:, g * tq:(g + 1) * tq], -jnp.inf)
                                  for g in range(KV_GROUP)], axis=1)
            lg_buf[kv, kb] = lg
            m_acc[kv] = jnp.maximum(m_acc[kv], _fold_rows(lg, jnp.max))

    _for_key_blocks(n_kb, logits_block)
    mx = [jnp.max(m_acc[kv], axis=0, keepdims=True) for kv in range(N_KV_HEADS)]

    def value_block(kb):
        for kv in range(N_KV_HEADS):
            e = jnp.exp2((lg_buf[kv, kb] - mx[kv]).astype(BF16))
            vt = vtbuf[kb, kv * VT_ROWS:(kv + 1) * VT_ROWS, :]
            o_acc[kv] += jnp.dot(vt, e, preferred_element_type=F32)

    _for_key_blocks(n_kb, value_block)
    for kv in range(N_KV_HEADS):
        out_t = o_acc[kv, 0:HEAD_DIM, :] / o_acc[kv, HEAD_DIM:HEAD_DIM + 1, :]
        for g in range(KV_GROUP):
            hd = kv * KV_GROUP + g
            o_ref[:, hd * HEAD_DIM:(hd + 1) * HEAD_DIM] = out_t[:, g * tq:(g + 1) * tq].T.astype(o_ref.dtype)


def _dsa(qi, kw, q, k, vt, batch, first_tile):
    nq = T_PAD // TQ
    ns = nq - first_tile
    pairs = qi.shape[0]
    cols = KV_GROUP * TQ
    qrow = lambda b, s: b * nq + first_tile + s
    return pl.pallas_call(
        functools.partial(_dsa_kernel, first_tile=first_tile),
        grid=(batch, ns),
        in_specs=[pl.BlockSpec((pairs, TQ, LANES), lambda b, s: (0, qrow(b, s), 0)),
                  pl.BlockSpec((TQ, LANES), lambda b, s: (qrow(b, s), 0)),
                  pl.BlockSpec((TQ, Q_W), lambda b, s: (qrow(b, s), 0)),
                  pl.BlockSpec((T_PAD, LANES), lambda b, s: (b, 0)),
                  pl.BlockSpec((T_PAD, KV_W), lambda b, s: (b, 0)),
                  pl.BlockSpec((KV_W, T_PAD), lambda b, s: (0, b))],
        out_specs=pl.BlockSpec((TQ, Q_W), lambda b, s: (b * ns + s, 0)),
        out_shape=jax.ShapeDtypeStruct((batch * ns * TQ, Q_W), BF16),
        scratch_shapes=[pltpu.VMEM((N_KB, KB, LANES), BF16), pltpu.VMEM((N_KB, KB, LANES), BF16),
                        pltpu.VMEM((N_KB, KB, KV_W), BF16), pltpu.VMEM((N_KB, N_KV_HEADS * VT_ROWS, KB), BF16),
                        pltpu.VMEM((N_KB, KB, TQ), F32),
                        pltpu.VMEM((N_KV_HEADS, N_KB, KB, cols), F32),
                        pltpu.VMEM((N_KV_HEADS, SUBLANES, cols), F32),
                        pltpu.VMEM((N_KV_HEADS, VT_ROWS, cols), F32)],
        compiler_params=_params(("parallel", "arbitrary")),
        name="dsa",
    )(qi, kw, q, kw, k, vt)


def kernel(x, meta, norm_g, ab_w_in, pool_w, pool_scale, s5_lambda_re, s5_lambda_im, s5_log_dt,
           s5_b_re, s5_b_im, s5_c_re, s5_c_im, s5_d, s5_w_glu, ab_w_out, c_w_in, c_w_out, mlp_w1, mlp_w2):
    batch, seq, d = x.shape
    assert (seq + N_META, d) == (T_REAL, D_MODEL)
    depth = norm_g.shape[0]
    meta_b = jnp.broadcast_to(meta[None].astype(x.dtype), (batch, N_META, d))
    pad = jnp.zeros((batch, T_PAD - T_REAL, d), x.dtype)
    h = jnp.concatenate([pad, meta_b, x], axis=1).reshape(batch * T_PAD, d)

    rope128 = _rope_tables(HEAD_DIM, T_PAD, -PAD_FRONT, batch)
    rope64 = _rope_tables(IDX_DIM, T_PAD, -PAD_FRONT, batch)
    w1, w2 = mlp_w1.astype(BF16), mlp_w2.astype(BF16)

    for layer in range(depth):
        g = norm_g[layer]
        frames_only = False
        if layer % 2 == 0:
            e = layer // 2
            u = _norm_proj(h, g[0], ab_w_in[e].astype(BF16))
            bbr, bbi, pwr, pwi = _ssm_prep(s5_lambda_re[e], s5_lambda_im[e], s5_log_dt[e], s5_b_re[e], s5_b_im[e])
            nj = SSM_GROUPS // GROUPS_PER_TILE
            tile_major = lambda p: p.reshape(2, nj, GROUPS_PER_TILE * SSM_STATE).swapaxes(0, 1)
            wb = jnp.concatenate([_block_diag_in(bbr), _block_diag_in(bbi)], axis=-1).astype(BF16)
            wc = jnp.concatenate([_block_diag_out(s5_c_re[e]), -_block_diag_out(s5_c_im[e])], axis=1).astype(BF16)
            mix = _mixer0(u, batch, pool_w[e].astype(BF16), pool_scale[e].reshape(1, POOL_WIDTH), wb, wc,
                          tile_major(pwr), tile_major(pwi),
                          s5_d[e].reshape(1, SSM_WIDTH), s5_w_glu[e].astype(BF16))
            w_out = ab_w_out[e]
        else:
            o = layer // 2
            n_in = c_w_in.shape[-1]
            w_in = jnp.pad(c_w_in[o].astype(BF16), ((0, 0), (0, -n_in % LANES)))
            w_vt = c_w_in[o][:, Q_W + KV_W:Q_W + 2 * KV_W].T.astype(BF16)
            q, k, vt, qi, kw = _dsa_proj(h, g[0], w_in, w_vt, rope128, rope64)
            frames_only = layer == depth - 1
            mix = _dsa(qi, kw, q, k, vt, batch, first_tile=FRAME_TILE if frames_only else 0)
            w_out = c_w_out[o]
        h = _matmul_norm_res(mix, w_out.astype(BF16), g[1], h, (T_PAD, FRAME_TILE * TQ) if frames_only else None)
        h = _mlp(h, g[2], g[3], w1, w2, layer)
    if frames_only:
        return h.reshape(batch, seq, d)
    return h.reshape(batch, T_PAD, d)[:, PAD_FRONT + N_META:]
```
